```python
import math
import jax
import jax.numpy as jnp
from jax import lax
import numpy as np

D_MODEL = 1024
BATCH = 8
SEQ = 2048
DEPTH = 2

N_EVEN = (DEPTH + 1) // 2
N_ODD = DEPTH // 2
DEEPNORM_ALPHA = (2 * DEPTH) ** 0.25
DEEPNORM_BETA = (8 * DEPTH) ** -0.25
ROPE_THETA = 10000.0
RMS_EPS = 1e-6
LN_EPS = 1e-5
L2_EPS = 1e-6
POS_OFFSET_MAX = 512

MLA_HEADS = 8
MLA_NOPE = 128
MLA_ROPE = 64
MLA_V = 128
Q_LORA = 768
KV_LORA = 256
Q_BLOCK = 128

DIL_HEADS = 8
DIL_HD = 128
DILATED_GROUPS = ((128, 1), (512, 4), (2048, 16))

GDN_HEADS = 8
GDN_DK = 128
GDN_DV = 128
CONV_K = 5
CHUNK = 64
DT_MIN = 1e-3
DT_MAX = 1e-1

RET_HEADS = 8
RET_DK = 64
RET_DV = 128
RET_DECAY_BASE = 5.0

EVEN_MIX = MLA_HEADS * MLA_V + DIL_HEADS * DIL_HD
EVEN_SPLITS = (Q_LORA, KV_LORA, MLA_ROPE, DIL_HEADS * DIL_HD, DIL_HEADS * DIL_HD, DIL_HEADS * DIL_HD, EVEN_MIX)
EVEN_IN = sum(EVEN_SPLITS)
GDN_CONV_CH = 2 * GDN_HEADS * GDN_DK + GDN_HEADS * GDN_DV
ODD_MIX = GDN_HEADS * GDN_DV + RET_HEADS * RET_DV
ODD_SPLITS = (GDN_CONV_CH, 2 * GDN_HEADS, 2 * GDN_HEADS, RET_HEADS * RET_DK, RET_HEADS * RET_DK, RET_HEADS * RET_DV, ODD_MIX)
ODD_IN = sum(ODD_SPLITS)

kernel_name = 'hybrid_mla_dilated_gdn_retention_encoder'


def split_last(h, sizes):
    offsets = np.cumsum(np.array(sizes))[:-1].tolist()
    return jnp.split(h, offsets, axis=-1)


def rms_norm(x, w):
    xf = x.astype(jnp.float32)
    y = xf * lax.rsqrt(jnp.mean(xf * xf, axis=-1, keepdims=True) + RMS_EPS)
    return (y * w.astype(jnp.float32)).astype(x.dtype)


def layer_norm(x, w, b):
    xf = x.astype(jnp.float32)
    mu = jnp.mean(xf, axis=-1, keepdims=True)
    var = jnp.mean(jnp.square(xf - mu), axis=-1, keepdims=True)
    y = (xf - mu) * lax.rsqrt(var + LN_EPS)
    return (y * w.astype(jnp.float32) + b.astype(jnp.float32)).astype(x.dtype)


def l2_normalize(x):
    xf = x.astype(jnp.float32)
    return xf * lax.rsqrt(jnp.sum(xf * xf, axis=-1, keepdims=True) + L2_EPS)


def apply_rope(x, positions):
    d = x.shape[-1]
    half = d // 2
    inv_freq = ROPE_THETA ** (-jnp.arange(0, d, 2, dtype=jnp.float32) / d)
    ang = positions.astype(jnp.float32)[..., None] * inv_freq
    cos = jnp.cos(ang)[:, :, None, :]
    sin = jnp.sin(ang)[:, :, None, :]
    x1 = x[..., :half].astype(jnp.float32)
    x2 = x[..., half:].astype(jnp.float32)
    return jnp.concatenate([x1 * cos - x2 * sin, x2 * cos + x1 * sin], axis=-1).astype(x.dtype)


def centred_depthwise_conv(x, w):
    k, c = w.shape
    return lax.conv_general_dilated(
        x, w[:, None, :].astype(x.dtype), window_strides=(1,),
        padding=[((k - 1) // 2, k // 2)],
        dimension_numbers=('NWC', 'WIO', 'NWC'), feature_group_count=c)


def mla_attention(q_nope, q_rope, k_nope, k_rope, v):
    bsz, seq, heads, _ = q_nope.shape
    nblk = seq // Q_BLOCK
    scale = (MLA_NOPE + MLA_ROPE) ** -0.5

    def to_blocks(t):
        return jnp.moveaxis(t.reshape((bsz, nblk, Q_BLOCK) + t.shape[2:]), 1, 0)

    def attend(blk):
        qn, qr = blk
        s = (jnp.einsum('bqhd,bkhd->bhqk', qn, k_nope, preferred_element_type=jnp.float32)
             + jnp.einsum('bqhr,bkr->bhqk', qr, k_rope, preferred_element_type=jnp.float32))
        p = jax.nn.softmax(s * scale, axis=-1)
        return jnp.einsum('bhqk,bkhd->bqhd', p.astype(v.dtype), v)

    o = lax.map(attend, (to_blocks(q_nope), to_blocks(q_rope)))
    return jnp.moveaxis(o, 0, 1).reshape(bsz, seq, heads * v.shape[-1])


def banded_window_attention(q, k, v, radius):
    n, length, d = q.shape
    blk = radius
    nb = -(-length // blk)
    pad = nb * blk - length
    qb = jnp.pad(q, ((0, 0), (0, pad), (0, 0))).reshape(n, nb, blk, d)

    def band(t):
        tp = jnp.pad(t, ((0, 0), (blk, pad + blk), (0, 0))).reshape(n, nb + 2, blk, d)
        return jnp.concatenate([tp[:, :-2], tp[:, 1:-1], tp[:, 2:]], axis=2)

    kb, vb = band(k), band(v)
    qpos = jnp.arange(nb)[:, None] * blk + jnp.arange(blk)[None, :]
    kpos = (jnp.arange(nb)[:, None] - 1) * blk + jnp.arange(3 * blk)[None, :]
    rel = kpos[:, None, :] - qpos[:, :, None]
    valid = (jnp.abs(rel) <= radius) & (kpos[:, None, :] >= 0) & (kpos[:, None, :] < length)
    s = jnp.einsum('nbqd,nbkd->nbqk', qb, kb, preferred_element_type=jnp.float32) * (d ** -0.5)
    s = jnp.where(valid, s, -jnp.inf)
    lse = jax.nn.logsumexp(s, axis=-1)
    p = jnp.exp(s - lse[..., None])
    o = jnp.einsum('nbqk,nbkd->nbqd', p, vb.astype(jnp.float32))
    return o.reshape(n, nb * blk, d)[:, :length], lse.reshape(n, nb * blk)[:, :length]


def dilated_window_attention(q, k, v):
    bsz, seq, heads, dh = q.shape
    outs, lses = [], []
    for window, dil in DILATED_GROUPS:
        radius = window // (2 * dil)
        length = seq // dil

        def to_sub(t):
            return t.reshape(bsz, length, dil, heads, dh).transpose(0, 2, 3, 1, 4).reshape(bsz * dil * heads, length, dh)

        o, lse = banded_window_attention(to_sub(q), to_sub(k), to_sub(v), radius)
        outs.append(o.reshape(bsz, dil, heads, length, dh).transpose(0, 3, 1, 2, 4).reshape(bsz, seq, heads, dh))
        lses.append(lse.reshape(bsz, dil, heads, length).transpose(0, 3, 1, 2).reshape(bsz, seq, heads))
    weights = jax.nn.softmax(jnp.stack(lses), axis=0)
    o = jnp.sum(weights[..., None] * jnp.stack(outs), axis=0)
    return o.astype(q.dtype).reshape(bsz, seq, heads * dh)


def gated_delta_rule_chunked(q, k, v, g, beta):
    bsz, heads, seq, dk = q.shape
    dv = v.shape[-1]
    n = seq // CHUNK
    q = q.astype(jnp.float32).reshape(bsz, heads, n, CHUNK, dk)
    k = k.astype(jnp.float32).reshape(bsz, heads, n, CHUNK, dk)
    v = v.astype(jnp.float32).reshape(bsz, heads, n, CHUNK, dv)
    g = g.astype(jnp.float32).reshape(bsz, heads, n, CHUNK)
    beta = beta.astype(jnp.float32).reshape(bsz, heads, n, CHUNK)
    gc = jnp.cumsum(g, axis=-1)
    causal = jnp.tril(jnp.ones((CHUNK, CHUNK), dtype=bool))
    eye = jnp.eye(CHUNK, dtype=jnp.float32)
    decay = jnp.exp(jnp.where(causal, gc[..., :, None] - gc[..., None, :], -jnp.inf))
    kb = k * beta[..., None]
    lower = jnp.tril(jnp.einsum('bhncd,bhnsd->bhncs', kb, k) * decay, -1)
    t_mat = lax.linalg.triangular_solve(lower + eye, jnp.broadcast_to(eye, lower.shape),
                                        left_side=True, lower=True, unit_diagonal=True)
    w = jnp.einsum('bhncs,bhnsd->bhncd', t_mat, kb * jnp.exp(gc)[..., None])
    u = jnp.einsum('bhncs,bhnse->bhnce', t_mat, v * beta[..., None])
    attn = jnp.einsum('bhncd,bhnsd->bhncs', q, k) * decay
    qg = q * jnp.exp(gc)[..., None]
    kd = k * jnp.exp(gc[..., -1:] - gc)[..., None]
    g_last = jnp.exp(gc[..., -1])

    def step(state, xs):
        w_i, u_i, attn_i, qg_i, kd_i, gl_i = xs
        v_new = u_i - jnp.einsum('bhcd,bhde->bhce', w_i, state)
        o_i = jnp.einsum('bhcd,bhde->bhce', qg_i, state) + jnp.einsum('bhcs,bhse->bhce', attn_i, v_new)
        state = state * gl_i[..., None, None] + jnp.einsum('bhcd,bhce->bhde', kd_i, v_new)
        return state, o_i

    xs = tuple(jnp.moveaxis(t, 2, 0) for t in (w, u, attn, qg, kd, g_last))
    _, o = lax.scan(step, jnp.zeros((bsz, heads, dk, dv), jnp.float32), xs)
    return jnp.moveaxis(o, 0, 2).reshape(bsz, heads, seq, dv)


def retention_chunked(q, k, v, inclusive):
    bsz, heads, seq, dk = q.shape
    dv = v.shape[-1]
    n = seq // CHUNK
    log_gamma = jnp.log1p(-(2.0 ** (-RET_DECAY_BASE - jnp.arange(heads, dtype=jnp.float32))))
    q = q.astype(jnp.float32).reshape(bsz, heads, n, CHUNK, dk)
    k = k.astype(jnp.float32).reshape(bsz, heads, n, CHUNK, dk)
    v = v.astype(jnp.float32).reshape(bsz, heads, n, CHUNK, dv)
    idx = jnp.arange(CHUNK, dtype=jnp.float32)
    rel = idx[:, None] - idx[None, :]
    mask = rel >= 0 if inclusive else rel > 0
    dmat = jnp.where(mask, jnp.exp(log_gamma[:, None, None] * jnp.maximum(rel, 0.0)), 0.0)
    inner = jnp.einsum('bhncs,bhnse->bhnce', jnp.einsum('bhncd,bhnsd->bhncs', q, k) * dmat[:, None], v)
    xi = jnp.exp(log_gamma[:, None] * (idx + 1.0))
    zeta = jnp.exp(log_gamma[:, None] * (CHUNK - 1.0 - idx))
    kv_chunk = jnp.einsum('bhncd,bhnce->bhnde', k * zeta[:, None, :, None], v)
    qx = q * xi[:, None, :, None]
    g_chunk = jnp.exp(log_gamma * CHUNK)

    def step(state, xs):
        qx_i, kv_i = xs
        o_i = jnp.einsum('bhcd,bhde->bhce', qx_i, state)
        state = state * g_chunk[:, None, None] + kv_i
        return state, o_i

    _, cross = lax.scan(step, jnp.zeros((bsz, heads, dk, dv), jnp.float32),
                        (jnp.moveaxis(qx, 2, 0), jnp.moveaxis(kv_chunk, 2, 0)))
    return (inner + jnp.moveaxis(cross, 0, 2)).reshape(bsz, heads, seq, dv)


def head_group_norm(o, w, b):
    mu = jnp.mean(o, axis=-1, keepdims=True)
    var = jnp.mean(jnp.square(o - mu), axis=-1, keepdims=True)
    o = (o - mu) * lax.rsqrt(var + LN_EPS)
    bsz, heads, seq, dv = o.shape
    return o.transpose(0, 2, 1, 3).reshape(bsz, seq, heads * dv) * w.astype(jnp.float32) + b.astype(jnp.float32)


def mla_dilated_layer(x, positions, w_in, q_norm, w_uq, kv_norm, w_ukv, w_out, ln_w, ln_b):
    bsz, seq, _ = x.shape
    c_q, c_kv, k_rope, q_b, k_b, v_b, z = split_last(x @ w_in, EVEN_SPLITS)
    q_a = (rms_norm(c_q, q_norm) @ w_uq).reshape(bsz, seq, MLA_HEADS, MLA_NOPE + MLA_ROPE)
    q_nope = q_a[..., :MLA_NOPE]
    q_rope = apply_rope(q_a[..., MLA_NOPE:], positions)
    k_rope = apply_rope(k_rope[:, :, None, :], positions)[:, :, 0, :]
    kv = (rms_norm(c_kv, kv_norm) @ w_ukv).reshape(bsz, seq, MLA_HEADS, MLA_NOPE + MLA_V)
    o_a = mla_attention(q_nope, q_rope, kv[..., :MLA_NOPE], k_rope, kv[..., MLA_NOPE:])
    def heads(t):
        return t.reshape(bsz, seq, DIL_HEADS, DIL_HD)
    o_b = dilated_window_attention(apply_rope(heads(q_b), positions), apply_rope(heads(k_b), positions), heads(v_b))
    y = jnp.concatenate([o_a, o_b], axis=-1) * jax.nn.silu(z)
    return layer_norm(DEEPNORM_ALPHA * x + y @ w_out, ln_w, ln_b)


def delta_retention_layer(x, positions, w_in, conv_w, a_log, dt_bias, gdn_norm, ret_norm_w, ret_norm_b, w_out, ln_w, ln_b):
    bsz, seq, _ = x.shape
    qkv_c, beta_raw, a_raw, q_d, k_d, v_d, z = split_last(x @ w_in, ODD_SPLITS)

    def to_bhsd(t, d):
        return t.reshape(bsz, seq, -1, d).transpose(0, 2, 1, 3)

    def flip(t):
        return jnp.flip(t, axis=2)

    qkv_c = jax.nn.silu(centred_depthwise_conv(qkv_c, conv_w))
    q_c, k_c, v_c = split_last(qkv_c, (GDN_HEADS * GDN_DK, GDN_HEADS * GDN_DK, GDN_HEADS * GDN_DV))
    q_c = l2_normalize(to_bhsd(q_c, GDN_DK)) * (GDN_DK ** -0.5)
    k_c = l2_normalize(to_bhsd(k_c, GDN_DK))
    v_c = to_bhsd(v_c, GDN_DV)
    beta = jax.nn.sigmoid(beta_raw.astype(jnp.float32)).reshape(bsz, seq, 2, GDN_HEADS).transpose(2, 0, 3, 1)
    a_in = a_raw.astype(jnp.float32).reshape(bsz, seq, 2, GDN_HEADS).transpose(2, 0, 3, 1)
    g = -jnp.exp(a_log.astype(jnp.float32))[:, None, :, None] * jax.nn.softplus(a_in + dt_bias.astype(jnp.float32)[:, None, :, None])
    o_fwd = gated_delta_rule_chunked(q_c, k_c, v_c, g[0], beta[0])
    o_bwd = flip(gated_delta_rule_chunked(flip(q_c), flip(k_c), flip(v_c), flip(g[1]), flip(beta[1])))
    o_c = rms_norm(o_fwd + o_bwd, gdn_norm).transpose(0, 2, 1, 3).reshape(bsz, seq, GDN_HEADS * GDN_DV)
    q_r = apply_rope(q_d.reshape(bsz, seq, RET_HEADS, RET_DK), positions).transpose(0, 2, 1, 3)
    k_r = (apply_rope(k_d.reshape(bsz, seq, RET_HEADS, RET_DK), positions) * (RET_DK ** -0.5)).transpose(0, 2, 1, 3)
    v_r = to_bhsd(v_d, RET_DV)
    o_d = retention_chunked(q_r, k_r, v_r, True) + flip(retention_chunked(flip(q_r), flip(k_r), flip(v_r), False))
    o_d = head_group_norm(o_d, ret_norm_w, ret_norm_b)
    y = jnp.concatenate([o_c.astype(x.dtype), o_d.astype(x.dtype)], axis=-1) * jax.nn.silu(z)
    return layer_norm(DEEPNORM_ALPHA * x + y @ w_out, ln_w, ln_b)


def setup_inputs(seed: int = 0) -> dict:
    key = jax.random.key(seed)
    ks = jax.random.split(key, 24)
    f32 = jnp.float32

    def dense(k, shape, fan_in, gain=1.0):
        return jax.random.normal(k, shape, f32) * (gain * fan_in ** -0.5)

    def norm_gain(k, shape):
        return 1.0 + 0.02 * jax.random.normal(k, shape, f32)

    def small_bias(k, shape):
        return 0.02 * jax.random.normal(k, shape, f32)

    x = jax.random.normal(ks[0], (BATCH, SEQ, D_MODEL), f32)
    positions = (jnp.arange(SEQ, dtype=jnp.int32)[None, :]
                 + jax.random.randint(ks[1], (BATCH, 1), 0, POS_OFFSET_MAX, dtype=jnp.int32))
    dt = jnp.exp(jax.random.uniform(ks[12], (N_ODD, 2, GDN_HEADS), f32, math.log(DT_MIN), math.log(DT_MAX)))
    return {
        'x': x,
        'positions': positions,
        'even_w_in': dense(ks[2], (N_EVEN, D_MODEL, EVEN_IN), D_MODEL),
        'even_q_norm': norm_gain(ks[3], (N_EVEN, Q_LORA)),
        'even_w_uq': dense(ks[4], (N_EVEN, Q_LORA, MLA_HEADS * (MLA_NOPE + MLA_ROPE)), Q_LORA),
        'even_kv_norm': norm_gain(ks[5], (N_EVEN, KV_LORA)),
        'even_w_ukv': dense(ks[6], (N_EVEN, KV_LORA, MLA_HEADS * (MLA_NOPE + MLA_V)), KV_LORA),
        'even_w_out': dense(ks[7], (N_EVEN, EVEN_MIX, D_MODEL), EVEN_MIX, DEEPNORM_BETA),
        'even_ln_w': norm_gain(ks[8], (N_EVEN, D_MODEL)),
        'even_ln_b': small_bias(ks[9], (N_EVEN, D_MODEL)),
        'odd_w_in': dense(ks[10], (N_ODD, D_MODEL, ODD_IN), D_MODEL),
        'odd_conv_w': dense(ks[11], (N_ODD, CONV_K, GDN_CONV_CH), CONV_K),
        'odd_a_log': jnp.log(jax.random.uniform(ks[13], (N_ODD, 2, GDN_HEADS), f32, 1.0, 16.0)),
        'odd_dt_bias': dt + jnp.log(-jnp.expm1(-dt)),
        'odd_gdn_norm': norm_gain(ks[14], (N_ODD, GDN_DV)),
        'odd_ret_norm_w': norm_gain(ks[15], (N_ODD, RET_HEADS * RET_DV)),
        'odd_ret_norm_b': small_bias(ks[16], (N_ODD, RET_HEADS * RET_DV)),
        'odd_w_out': dense(ks[17], (N_ODD, ODD_MIX, D_MODEL), ODD_MIX, DEEPNORM_BETA),
        'odd_ln_w': norm_gain(ks[18], (N_ODD, D_MODEL)),
        'odd_ln_b': small_bias(ks[19], (N_ODD, D_MODEL)),
    }


def reference(x, positions, even_w_in, even_q_norm, even_w_uq, even_kv_norm, even_w_ukv, even_w_out,
              even_ln_w, even_ln_b, odd_w_in, odd_conv_w, odd_a_log, odd_dt_bias, odd_gdn_norm,
              odd_ret_norm_w, odd_ret_norm_b, odd_w_out, odd_ln_w, odd_ln_b):
    for layer in range(DEPTH):
        i = layer // 2
        if layer % 2 == 0:
            x = mla_dilated_layer(x, positions, even_w_in[i], even_q_norm[i], even_w_uq[i], even_kv_norm[i],
                                  even_w_ukv[i], even_w_out[i], even_ln_w[i], even_ln_b[i])
        else:
            x = delta_retention_layer(x, positions, odd_w_in[i], odd_conv_w[i], odd_a_log[i], odd_dt_bias[i],
                                      odd_gdn_norm[i], odd_ret_norm_w[i], odd_ret_norm_b[i], odd_w_out[i],
                                      odd_ln_w[i], odd_ln_b[i])
    return x
```

```python
import functools

import numpy as np
import jax
import jax.numpy as jnp
from jax import lax
from jax.experimental import pallas as pl
from jax.experimental.pallas import tpu as pltpu

F32 = jnp.float32
BF16 = jnp.bfloat16

D_MODEL = 1024
DEPTH = 2
DEEPNORM_ALPHA = (2 * DEPTH) ** 0.25
ROPE_THETA = 10000.0
RMS_EPS = 1e-6
LN_EPS = 1e-5
L2_EPS = 1e-6

MLA_HEADS = 8
MLA_NOPE = 128
MLA_ROPE = 64
MLA_V = 128
Q_LORA = 768
KV_LORA = 256

DIL_HEADS = 8
DIL_HD = 128
DILATED_GROUPS = ((128, 1), (512, 4), (2048, 16))

GDN_HEADS = 8
GDN_DK = 128
GDN_DV = 128
CONV_K = 5
CHUNK = 64

RET_HEADS = 8
RET_DK = 64
RET_DV = 128
RET_DECAY_BASE = 5.0
RET_CHUNK = 256

LANES = 128
VMEM_LIMIT = 56 * 1024 * 1024


def _cparams(sem):
    return pltpu.CompilerParams(dimension_semantics=sem, vmem_limit_bytes=VMEM_LIMIT)


def _dot(a, b):
    return jnp.dot(a, b, preferred_element_type=F32)


def _dot_nt(a, b):
    return lax.dot_general(a, b, (((1,), (1,)), ((), ())), preferred_element_type=F32)


def _dot_tn(a, b):
    return lax.dot_general(a, b, (((0,), (0,)), ((), ())), preferred_element_type=F32)


def _const_spec(shape):
    nd = len(shape)
    return pl.BlockSpec(shape, lambda *_: (0,) * nd)


def _rope_pairs64(x, cos, sin_signed):
    lane = lax.broadcasted_iota(jnp.int32, x.shape, 1)
    swapped = jnp.where((lane % 64) < 32, pltpu.roll(x, 96, 1), pltpu.roll(x, 32, 1))
    return x * cos + swapped * sin_signed


def _rope_full128(x, cos, sin_signed):
    return x * cos + pltpu.roll(x, 64, 1) * sin_signed


def _silu(x):
    return x * jax.nn.sigmoid(x)


def _rope_tables_kernel(pos_ref, inv_ref, sgn_ref, c64_ref, s64_ref, c128_ref, s128_ref):
    pos = pos_ref[0]
    a64 = pos * inv_ref[0:1, :]
    a128 = pos * inv_ref[1:2, :]
    c64_ref[0] = jnp.cos(a64)
    s64_ref[0] = jnp.sin(a64) * sgn_ref[0:1, :]
    c128_ref[0] = jnp.cos(a128)
    s128_ref[0] = jnp.sin(a128) * sgn_ref[1:2, :]


def _rope_tables(positions):
    bsz, seq = positions.shape
    ts = 512
    inv32 = ROPE_THETA ** (-jnp.arange(0, 64, 2, dtype=F32) / 64)
    inv64 = ROPE_THETA ** (-jnp.arange(0, 128, 2, dtype=F32) / 128)
    inv = jnp.stack([jnp.tile(inv32, 4), jnp.tile(inv64, 2)])
    sgn = jnp.asarray(np.stack([np.tile(np.repeat([-1.0, 1.0], 32), 2), np.repeat([-1.0, 1.0], 64)]), F32)
    pos = positions.astype(F32)[..., None]
    tab = jax.ShapeDtypeStruct((bsz, seq, LANES), F32)
    tspec = pl.BlockSpec((1, ts, LANES), lambda b, s: (b, s, 0))
    return pl.pallas_call(
        _rope_tables_kernel,
        grid=(bsz, seq // ts),
        in_specs=[pl.BlockSpec((1, ts, 1), lambda b, s: (b, s, 0)), _const_spec((2, LANES)), _const_spec((2, LANES))],
        out_specs=[tspec] * 4,
        out_shape=[tab] * 4,
        compiler_params=_cparams(("parallel", "parallel")),
        name="rope_tables",
    )(pos, inv, sgn)


def _rms(x, w):
    return x * lax.rsqrt(jnp.mean(x * x, axis=-1, keepdims=True) + RMS_EPS) * w


def _in0_kernel(x_ref, wq_ref, wkv_ref, wkr_ref, wb_ref, wz_ref, wuq_ref, wukv_ref, qnorm_ref, kvnorm_ref,
                c64_ref, s64_ref, c128_ref, s128_ref,
                qn_ref, qr_ref, kv_ref, kr_ref, qb_ref, kb_ref, vb_ref, g_ref):
    x = x_ref[0].astype(BF16)
    c64, s64, c128, s128 = c64_ref[0], s64_ref[0], c128_ref[0], s128_ref[0]
    cq = _rms(_dot(x, wq_ref[...]), qnorm_ref[...])
    qa = _dot(cq.astype(BF16), wuq_ref[...]) * ((MLA_NOPE + MLA_ROPE) ** -0.5)
    nope = MLA_HEADS * MLA_NOPE
    qn_ref[0] = qa[:, :nope].astype(BF16)
    for j in range(MLA_HEADS * MLA_ROPE // LANES):
        blk = qa[:, nope + j * LANES:nope + (j + 1) * LANES]
        qr_ref[0, :, j * LANES:(j + 1) * LANES] = _rope_pairs64(blk, c64, s64).astype(BF16)
    ckv = _rms(_dot(x, wkv_ref[...]), kvnorm_ref[...])
    kv_ref[0] = _dot(ckv.astype(BF16), wukv_ref[...]).astype(BF16)
    kr_ref[0] = _rope_pairs64(_dot(x, wkr_ref[...]), c64, s64).astype(BF16)
    hb = _dot(x, wb_ref[...])
    width = DIL_HEADS * DIL_HD
    for h in range(DIL_HEADS):
        sl = slice(h * DIL_HD, (h + 1) * DIL_HD)
        qb_ref[0, :, sl] = _rope_full128(hb[:, sl], c128, s128) * (DIL_HD ** -0.5)
        kb_ref[0, :, sl] = _rope_full128(hb[:, width + h * DIL_HD:width + (h + 1) * DIL_HD], c128, s128)
    vb_ref[0] = hb[:, 2 * width:]
    g_ref[0] = _silu(_dot(x, wz_ref[...])).astype(BF16)


def _in0(x, tabs, w_in, q_norm, w_uq, kv_norm, w_ukv, tm=256):
    bsz, seq, _ = x.shape
    c64, s64, c128, s128 = tabs
    o = np.cumsum((0, Q_LORA, KV_LORA, MLA_ROPE, 1024, 1024, 1024, 2048))
    wq = w_in[:, o[0]:o[1]].astype(BF16)
    wkv = w_in[:, o[1]:o[2]].astype(BF16)
    wkr = jnp.tile(w_in[:, o[2]:o[3]], (1, 2)).astype(BF16)
    wb = w_in[:, o[3]:o[6]].astype(BF16)
    wz = w_in[:, o[6]:o[7]].astype(BF16)
    wuq3 = w_uq.reshape(Q_LORA, MLA_HEADS, MLA_NOPE + MLA_ROPE)
    wuq = jnp.concatenate([wuq3[:, :, :MLA_NOPE].reshape(Q_LORA, -1), wuq3[:, :, MLA_NOPE:].reshape(Q_LORA, -1)],
                          axis=1).astype(BF16)
    wukv = w_ukv.astype(BF16)
    weights = (wq, wkv, wkr, wb, wz, wuq, wukv, q_norm.reshape(1, -1), kv_norm.reshape(1, -1))

    def rows(width):
        return pl.BlockSpec((1, tm, width), lambda b, s: (b, s, 0))

    out_widths = (1024, 512, 2048, LANES, 1024, 1024, 1024, 2048)
    out_dtypes = (BF16, BF16, BF16, BF16, F32, F32, F32, BF16)
    return pl.pallas_call(
        _in0_kernel,
        grid=(bsz, seq // tm),
        in_specs=[rows(D_MODEL)] + [_const_spec(w.shape) for w in weights] + [rows(LANES)] * 4,
        out_specs=[rows(w) for w in out_widths],
        out_shape=[jax.ShapeDtypeStruct((bsz, seq, w), dt) for w, dt in zip(out_widths, out_dtypes)],
        compiler_params=_cparams(("parallel", "parallel")),
        name="in_proj0",
    )(x, *weights, c64, s64, c128, s128)


def _mla_kernel(qn_ref, qr_ref, kn_ref, kr_ref, v_ref, o_ref):
    h = pl.program_id(1)
    qr = qr_ref[0]
    lane = lax.broadcasted_iota(jnp.int32, qr.shape, 1)
    qr = jnp.where((lane // MLA_ROPE) == (h % 2), qr, jnp.zeros_like(qr))
    q = jnp.concatenate([qn_ref[0], qr], axis=1)
    k = jnp.concatenate([kn_ref[0], kr_ref[0]], axis=1)
    s = _dot_nt(q, k)
    m = jnp.max(s, axis=-1, keepdims=True)
    p = jnp.exp(s - m)
    l = jnp.sum(p, axis=-1, keepdims=True)
    o_ref[0] = (_dot(p.astype(BF16), v_ref[0]) / l).astype(o_ref.dtype)


def _mla(qn, qr, kv, kr, tq=256):
    bsz, seq, _ = qn.shape
    return pl.pallas_call(
        _mla_kernel,
        grid=(bsz, MLA_HEADS, seq // tq),
        in_specs=[
            pl.BlockSpec((1, tq, LANES), lambda b, h, i: (b, i, h)),
            pl.BlockSpec((1, tq, LANES), lambda b, h, i: (b, i, h // 2)),
            pl.BlockSpec((1, seq, LANES), lambda b, h, i: (b, 0, 2 * h)),
            pl.BlockSpec((1, seq, LANES), lambda b, h, i: (b, 0, 0)),
            pl.BlockSpec((1, seq, LANES), lambda b, h, i: (b, 0, 2 * h + 1)),
        ],
        out_specs=pl.BlockSpec((1, tq, LANES), lambda b, h, i: (b, i, h)),
        out_shape=jax.ShapeDtypeStruct((bsz, seq, MLA_HEADS * MLA_V), BF16),
        compiler_params=_cparams(("parallel", "parallel", "parallel")),
        name="mla_attention",
    )(qn, qr, kv, kr, kv)


DIL_TQ = 128


def _dil_kernel(q_ref, k_ref, v_ref, o_ref, m_s, l_s, acc_s):
    seq = q_ref.shape[1]

    def rows_of(start, size, dil):
        return pl.ds(start, size) if dil == 1 else pl.ds(start, size, stride=dil)

    def tile(window, dil, r, t, first):
        radius = window // (2 * dil)
        length = seq // dil
        kw = min(length, DIL_TQ + 2 * radius)
        ws = jnp.clip(t * DIL_TQ - radius, 0, length - kw)
        qrows = rows_of(r + t * (DIL_TQ * dil), DIL_TQ, dil)
        krows = rows_of(r + ws * dil, kw, dil)
        q = q_ref[0, qrows, :].astype(BF16)
        k = k_ref[0, krows, :].astype(BF16)
        v = v_ref[0, krows, :].astype(BF16)
        s = _dot_nt(q, k)
        rel = (ws + lax.broadcasted_iota(jnp.int32, s.shape, 1)) - (t * DIL_TQ + lax.broadcasted_iota(jnp.int32, s.shape, 0))
        s = jnp.where(jnp.abs(rel) <= radius, s, -jnp.inf)
        m = jnp.max(s, axis=-1, keepdims=True)
        p = jnp.exp(s - m)
        l = jnp.sum(p, axis=-1, keepdims=True)
        o = _dot(p.astype(BF16), v)
        if first:
            m_s[qrows, :] = m
            l_s[qrows, :] = l
            acc_s[qrows, :] = o
        else:
            m_old = m_s[qrows, :]
            m_new = jnp.maximum(m_old, m)
            a_old = jnp.exp(m_old - m_new)
            a_new = jnp.exp(m - m_new)
            m_s[qrows, :] = m_new
            l_s[qrows, :] = a_old * l_s[qrows, :] + a_new * l
            acc_s[qrows, :] = a_old * acc_s[qrows, :] + a_new * o

    for g, (window, dil) in enumerate(DILATED_GROUPS):
        tiles = seq // dil // DIL_TQ

        def body(i, carry, window=window, dil=dil, tiles=tiles, first=(g == 0)):
            tile(window, dil, i // tiles, i % tiles, first)
            return carry

        lax.fori_loop(0, dil * tiles, body, 0)
    o_ref[0] = (acc_s[...] / l_s[...]).astype(o_ref.dtype)


def _dilated(qb, kb, vb):
    bsz, seq, _ = qb.shape
    spec = pl.BlockSpec((1, seq, DIL_HD), lambda b, h: (b, 0, h))
    return pl.pallas_call(
        _dil_kernel,
        grid=(bsz, DIL_HEADS),
        in_specs=[spec] * 3,
        out_specs=spec,
        out_shape=jax.ShapeDtypeStruct((bsz, seq, DIL_HEADS * DIL_HD), BF16),
        scratch_shapes=[pltpu.VMEM((seq, 1), F32), pltpu.VMEM((seq, 1), F32), pltpu.VMEM((seq, DIL_HD), F32)],
        compiler_params=_cparams(("parallel", "parallel")),
        name="dilated_attention",
    )(qb, kb, vb)


def _out_kernel(o1_ref, o2_ref, g_ref, x_ref, w_ref, lnw_ref, lnb_ref, out_ref):
    half = o1_ref.shape[-1]
    g = g_ref[0].astype(F32)
    y1 = (o1_ref[0].astype(F32) * g[:, :half]).astype(BF16)
    y2 = (o2_ref[0].astype(F32) * g[:, half:]).astype(BF16)
    r = DEEPNORM_ALPHA * x_ref[0] + _dot(y1, w_ref[:half, :]) + _dot(y2, w_ref[half:, :])
    mu = jnp.mean(r, axis=-1, keepdims=True)
    rc = r - mu
    var = jnp.mean(rc * rc, axis=-1, keepdims=True)
    out_ref[0] = rc * lax.rsqrt(var + LN_EPS) * lnw_ref[...] + lnb_ref[...]


def _out_proj(o1, o2, gate, x, w_out, ln_w, ln_b, tm=512):
    bsz, seq, _ = x.shape
    half = o1.shape[-1]

    def rows(width):
        return pl.BlockSpec((1, tm, width), lambda b, s: (b, s, 0))

    w = w_out.astype(BF16)
    return pl.pallas_call(
        _out_kernel,
        grid=(bsz, seq // tm),
        in_specs=[rows(half), rows(half), rows(2 * half), rows(D_MODEL), _const_spec(w.shape),
                  _const_spec((1, D_MODEL)), _const_spec((1, D_MODEL))],
        out_specs=rows(D_MODEL),
        out_shape=jax.ShapeDtypeStruct((bsz, seq, D_MODEL), F32),
        compiler_params=_cparams(("parallel", "parallel")),
        name="out_proj",
    )(o1, o2, gate, x, w, ln_w.reshape(1, -1), ln_b.reshape(1, -1))


def _layer0(x, tabs, w_in, q_norm, w_uq, kv_norm, w_ukv, w_out, ln_w, ln_b):
    qn, qr, kv, kr, qb, kb, vb, gate = _in0(x, tabs, w_in, q_norm, w_uq, kv_norm, w_ukv)
    o_a = _mla(qn, qr, kv, kr)
    o_b = _dilated(qb, kb, vb)
    return _out_proj(o_a, o_b, gate, x, w_out, ln_w, ln_b)


GATE_BETA = 0
GATE_GC = 2 * GDN_HEADS


def _softplus(x):
    return jnp.maximum(x, 0.0) + jnp.log1p(jnp.exp(-jnp.abs(x)))


def _in1_kernel(x_ref, wc_ref, wg_ref, wqk_ref, wv_ref, wz_ref, gp_ref, c64_ref, s64_ref,
                qkvc_ref, gate_ref, qr_ref, kr_ref, vr_ref, g_ref):
    x = x_ref[0].astype(BF16)
    tm = x.shape[0]
    c64, s64 = c64_ref[0], s64_ref[0]
    qkvc_ref[0] = _dot(x, wc_ref[...])
    hg = _dot(x, wg_ref[...])
    beta = jax.nn.sigmoid(hg)
    g = -jnp.exp(gp_ref[0:1, :]) * _softplus(hg + gp_ref[1:2, :])
    pos = lax.broadcasted_iota(jnp.int32, hg.shape, 0) % CHUNK
    pre, suf = g, g
    step = 1
    while step < CHUNK:
        pre = pre + jnp.where(pos >= step, pltpu.roll(pre, step, 0), 0.0)
        suf = suf + jnp.where(pos < CHUNK - step, pltpu.roll(suf, tm - step, 0), 0.0)
        step *= 2
    lane = lax.broadcasted_iota(jnp.int32, hg.shape, 1)
    gc = jnp.where(lane < GATE_GC + GDN_HEADS, pre, suf)
    gate_ref[0] = jnp.where(lane < GATE_GC, beta, gc)
    hqk = _dot(x, wqk_ref[...])
    width = RET_HEADS * RET_DK
    for j in range(width // LANES):
        qr_ref[0, :, j * LANES:(j + 1) * LANES] = _rope_pairs64(hqk[:, j * LANES:(j + 1) * LANES], c64, s64).astype(BF16)
        kblk = hqk[:, width + j * LANES:width + (j + 1) * LANES]
        kr_ref[0, :, j * LANES:(j + 1) * LANES] = (_rope_pairs64(kblk, c64, s64) * (RET_DK ** -0.5)).astype(BF16)
    vr_ref[0] = _dot(x, wv_ref[...]).astype(BF16)
    g_ref[0] = _silu(_dot(x, wz_ref[...])).astype(BF16)


def _in1(x, tabs, w_in, a_log, dt_bias, tm=256):
    bsz, seq, _ = x.shape
    c64, s64, _, _ = tabs
    conv_ch = 2 * GDN_HEADS * GDN_DK + GDN_HEADS * GDN_DV
    o = np.cumsum((0, conv_ch, 2 * GDN_HEADS, 2 * GDN_HEADS, 512, 512, 1024, 2048))
    wc = w_in[:, o[0]:o[1]].astype(BF16)
    wg = jnp.pad(w_in[:, o[1]:o[3]], ((0, 0), (0, LANES - 4 * GDN_HEADS))).astype(BF16)
    wqk = w_in[:, o[3]:o[5]].astype(BF16)
    wv = w_in[:, o[5]:o[6]].astype(BF16)
    wz = w_in[:, o[6]:o[7]].astype(BF16)
    pad = (GATE_GC, LANES - GATE_GC - 2 * GDN_HEADS)
    gp = jnp.stack([jnp.pad(a_log.reshape(-1), pad), jnp.pad(dt_bias.reshape(-1), pad)]).astype(F32)
    weights = (wc, wg, wqk, wv, wz, gp)

    def rows(width):
        return pl.BlockSpec((1, tm, width), lambda b, s: (b, s, 0))

    out_widths = (conv_ch, LANES, 512, 512, 1024, 2048)
    out_dtypes = (F32, F32, BF16, BF16, BF16, BF16)
    return pl.pallas_call(
        _in1_kernel,
        grid=(bsz, seq // tm),
        in_specs=[rows(D_MODEL)] + [_const_spec(w.shape) for w in weights] + [rows(LANES)] * 2,
        out_specs=[rows(w) for w in out_widths],
        out_shape=[jax.ShapeDtypeStruct((bsz, seq, w), dt) for w, dt in zip(out_widths, out_dtypes)],
        compiler_params=_cparams(("parallel", "parallel")),
        name="in_proj1",
    )(x, *weights, c64, s64)


CONV_PAD = 8


def _gdn_kernel(qraw_ref, kraw_ref, vraw_ref, wq_ref, wk_ref, wv_ref, gcol_ref, grow_ref, norm_ref, o_ref,
                xpad_s, q_s, k_s, v_s, kb_s, rhs_s, qg_s, wu_s, att_s, gcb_s, osum_s):
    seq = q_s.shape[0]
    nchunk = seq // CHUNK

    zeros = jnp.zeros((CONV_PAD, LANES), F32)
    xpad_s[0:CONV_PAD, :] = zeros
    xpad_s[CONV_PAD + seq:, :] = zeros

    def conv_silu(raw_ref, w_ref):
        xpad_s[CONV_PAD:CONV_PAD + seq, :] = raw_ref[0]
        acc = None
        for j in range(CONV_K):
            off = CONV_PAD - (CONV_K - 1) // 2 + j
            term = xpad_s[off:off + seq, :] * w_ref[j:j + 1, :]
            acc = term if acc is None else acc + term
        return _silu(acc)

    def l2norm(t):
        return t * lax.rsqrt(jnp.sum(t * t, axis=-1, keepdims=True) + L2_EPS)

    q_s[...] = l2norm(conv_silu(qraw_ref, wq_ref)) * (GDN_DK ** -0.5)
    k_s[...] = l2norm(conv_silu(kraw_ref, wk_ref))
    v_s[...] = conv_silu(vraw_ref, wv_ref)

    ii = lax.broadcasted_iota(jnp.int32, (CHUNK, CHUNK), 0)
    jj = lax.broadcasted_iota(jnp.int32, (CHUNK, CHUNK), 1)

    for d in range(2):
        causal = (ii >= jj) if d == 0 else (ii <= jj)
        strict = (ii > jj) if d == 0 else (ii < jj)
        beta_cb = jnp.broadcast_to(gcol_ref[0, 0, :, d:d + 1], (seq, LANES))
        gc_cb = jnp.broadcast_to(gcol_ref[0, 0, :, 2 + d:3 + d], (seq, LANES))
        egc = jnp.exp(gc_cb)
        gcb_s[...] = gc_cb
        kb = k_s[...] * beta_cb
        kb_s[...] = kb
        rhs_s[:, :GDN_DK] = kb * egc
        rhs_s[:, GDN_DK:] = v_s[...] * beta_cb
        qg_s[...] = q_s[...] * egc

        def phase1(c, carry, d=d, causal=causal, strict=strict):
            r = pl.ds(pl.multiple_of(c * CHUNK, CHUNK), CHUNK)
            kc = k_s[r, :].astype(BF16)
            kk = _dot_nt(kb_s[r, :].astype(BF16), kc)
            qk = _dot_nt(q_s[r, :].astype(BF16), kc)
            diff = gcb_s[r, 0:CHUNK] - grow_ref[0, 0, d, pl.ds(c, 1), :]
            decay = jnp.exp(jnp.where(causal, diff, -jnp.inf))
            a = jnp.where(strict, kk * decay, 0.0)
            n = -a
            xp = a
            for _ in range(5):
                xb = xp.astype(BF16)
                xp = _dot(xb, xb)
                n = n + xp + _dot(n.astype(BF16), xp.astype(BF16))
            rhs = rhs_s[r, :]
            wu_s[r, :] = rhs + _dot(n.astype(BF16), rhs.astype(BF16))
            att_s[r, :] = qk * decay
            return carry

        lax.fori_loop(0, nchunk, phase1, 0)

        def phase2(i, state, d=d):
            c = i if d == 0 else nchunk - 1 - i
            base = pl.multiple_of(c * CHUNK, CHUNK)
            r = pl.ds(base, CHUNK)
            wu = wu_s[r, :]
            lhs = jnp.concatenate([wu[:, :GDN_DK], qg_s[r, :]], axis=0).astype(BF16)
            m1 = _dot(lhs, state.astype(BF16))
            vnew = wu[:, GDN_DK:] - m1[:CHUNK]
            vnb = vnew.astype(BF16)
            o = m1[CHUNK:] + _dot(att_s[r, :].astype(BF16), vnb)
            gl = gcb_s[pl.ds(base + (CHUNK - 1 if d == 0 else 0), 1), :]
            kd = k_s[r, :] * jnp.exp(gl - gcb_s[r, :])
            if d == 0:
                osum_s[r, :] = o
            else:
                osum_s[r, :] = osum_s[r, :] + o
            return state * jnp.exp(gl) + _dot_tn(kd.astype(BF16), vnb)

        lax.fori_loop(0, nchunk, phase2, jnp.zeros((GDN_DK, GDN_DV), F32))

    o = osum_s[...]
    o_ref[0] = (o * lax.rsqrt(jnp.mean(o * o, axis=-1, keepdims=True) + RMS_EPS) * norm_ref[...]).astype(o_ref.dtype)


def _gdn(qkvc, conv_w, gcol, grow, gdn_norm):
    bsz, seq, _ = qkvc.shape
    nh = GDN_HEADS

    def col(off):
        return pl.BlockSpec((1, seq, LANES), lambda b, h: (b, 0, off + h))

    def wcol(off):
        return pl.BlockSpec((CONV_K, LANES), lambda b, h: (0, off + h))

    big = pltpu.VMEM((seq, LANES), F32)
    return pl.pallas_call(
        _gdn_kernel,
        grid=(bsz, nh),
        in_specs=[col(0), col(nh), col(2 * nh), wcol(0), wcol(nh), wcol(2 * nh),
                  pl.BlockSpec((1, 1, seq, 4), lambda b, h: (b, h, 0, 0)),
                  pl.BlockSpec((1, 1, 2, seq // CHUNK, CHUNK), lambda b, h: (b, h, 0, 0, 0)),
                  _const_spec((1, GDN_DV))],
        out_specs=pl.BlockSpec((1, seq, GDN_DV), lambda b, h: (b, 0, h)),
        out_shape=jax.ShapeDtypeStruct((bsz, seq, nh * GDN_DV), BF16),
        scratch_shapes=[pltpu.VMEM((seq + 2 * CONV_PAD, LANES), F32), big, big, big, big,
                        pltpu.VMEM((seq, 2 * LANES), F32), big, pltpu.VMEM((seq, 2 * LANES), F32),
                        pltpu.VMEM((seq, CHUNK), F32), big, big],
        compiler_params=_cparams(("parallel", "parallel")),
        name="gated_deltanet",
    )(qkvc, qkvc, qkvc, conv_w, conv_w, conv_w, gcol, grow, gdn_norm.reshape(1, -1))


def _ret_tables():
    heads = np.arange(RET_HEADS, dtype=np.float64)
    log_gamma = np.log1p(-(2.0 ** (-RET_DECAY_BASE - heads)))[:, None]
    idx = np.arange(RET_CHUNK, dtype=np.float64)
    dmat = np.exp(log_gamma[:, :, None] * np.abs(idx[:, None] - idx[None, :])[None])
    xi = np.exp(log_gamma * (idx + 1.0))
    zeta = np.exp(log_gamma * (RET_CHUNK - 1.0 - idx))
    gch = np.exp(log_gamma * RET_CHUNK)

    def cb(t):
        return jnp.asarray(np.broadcast_to(t[:, :, None], t.shape + (LANES,)), F32)

    return jnp.asarray(dmat, F32), cb(xi), cb(zeta), cb(gch)


def _ret_kernel(q_ref, k_ref, v_ref, d_ref, xi_ref, zeta_ref, gch_ref, nw_ref, nb_ref, o_ref, kvf_s, kvb_s):
    seq = q_ref.shape[1]
    nchunk = seq // RET_CHUNK
    h = pl.program_id(1)
    xi, zeta, gch = xi_ref[0], zeta_ref[0], gch_ref[0]
    lane = lax.broadcasted_iota(jnp.int32, (RET_CHUNK, LANES), 1)
    mine = (lane // RET_DK) == (h % 2)

    def rows(c):
        return slice(c * RET_CHUNK, (c + 1) * RET_CHUNK)

    for c in range(nchunk):
        kc = k_ref[0, rows(c), :].astype(F32)
        vc = v_ref[0, rows(c), :]
        kvf_s[c] = _dot_tn((kc * zeta).astype(BF16), vc)
        kvb_s[c] = _dot_tn((kc * xi).astype(BF16), vc)
    state = jnp.zeros((LANES, RET_DV), F32)
    for c in range(nchunk):
        nxt = state * gch + kvf_s[c]
        kvf_s[c] = state
        state = nxt
    state = jnp.zeros((LANES, RET_DV), F32)
    for c in reversed(range(nchunk)):
        nxt = state * gch + kvb_s[c]
        kvb_s[c] = state
        state = nxt
    for c in range(nchunk):
        q = jnp.where(mine, q_ref[0, rows(c), :], jnp.zeros((RET_CHUNK, LANES), BF16))
        qf = q.astype(F32)
        s = _dot_nt(q, k_ref[0, rows(c), :]) * d_ref[0]
        o = _dot(s.astype(BF16), v_ref[0, rows(c), :])
        o = o + _dot((qf * xi).astype(BF16), kvf_s[c].astype(BF16))
        o = o + _dot((qf * zeta).astype(BF16), kvb_s[c].astype(BF16))
        mu = jnp.mean(o, axis=-1, keepdims=True)
        oc = o - mu
        var = jnp.mean(oc * oc, axis=-1, keepdims=True)
        o_ref[0, rows(c), :] = (oc * lax.rsqrt(var + LN_EPS) * nw_ref[...] + nb_ref[...]).astype(o_ref.dtype)


def _retention(qr, kr, vr, norm_w, norm_b):
    bsz, seq, _ = vr.shape
    dmat, xi, zeta, gch = _ret_tables()
    pair = pl.BlockSpec((1, seq, LANES), lambda b, h: (b, 0, h // 2))
    head = pl.BlockSpec((1, seq, RET_DV), lambda b, h: (b, 0, h))

    def per_head(shape):
        return pl.BlockSpec((1,) + shape, lambda b, h: (h, 0, 0))

    vec = pl.BlockSpec((1, RET_DV), lambda b, h: (0, h))
    state = pltpu.VMEM((seq // RET_CHUNK, LANES, RET_DV), F32)
    return pl.pallas_call(
        _ret_kernel,
        grid=(bsz, RET_HEADS),
        in_specs=[pair, pair, head, per_head((RET_CHUNK, RET_CHUNK)), per_head((RET_CHUNK, LANES)),
                  per_head((RET_CHUNK, LANES)), per_head((1, LANES)), vec, vec],
        out_specs=head,
        out_shape=jax.ShapeDtypeStruct((bsz, seq, RET_HEADS * RET_DV), BF16),
        scratch_shapes=[state, state],
        compiler_params=_cparams(("parallel", "parallel")),
        name="retention",
    )(qr, kr, vr, dmat, xi, zeta, gch, norm_w.reshape(1, -1), norm_b.reshape(1, -1))


def _layer1(x, tabs, w_in, conv_w, a_log, dt_bias, gdn_norm, ret_norm_w, ret_norm_b, w_out, ln_w, ln_b):
    bsz, seq, _ = x.shape
    qkvc, gates, qr, kr, vr, gate = _in1(x, tabs, w_in, a_log, dt_bias)
    gsl = gates[:, :, :4 * GDN_HEADS].reshape(bsz, seq, 2, 2, GDN_HEADS)
    gcol = gsl.transpose(0, 4, 1, 2, 3).reshape(bsz, GDN_HEADS, seq, 4)
    grow = gsl[:, :, 1].transpose(0, 3, 2, 1).reshape(bsz, GDN_HEADS, 2, seq // CHUNK, CHUNK)
    o_c = _gdn(qkvc, conv_w, gcol, grow, gdn_norm)
    o_d = _retention(qr, kr, vr, ret_norm_w, ret_norm_b)
    return _out_proj(o_c, o_d, gate, x, w_out, ln_w, ln_b)


def kernel(x, positions, even_w_in, even_q_norm, even_w_uq, even_kv_norm, even_w_ukv, even_w_out, even_ln_w, even_ln_b, odd_w_in, odd_conv_w, odd_a_log, odd_dt_bias, odd_gdn_norm, odd_ret_norm_w, odd_ret_norm_b, odd_w_out, odd_ln_w, odd_ln_b):
    tabs = _rope_tables(positions)
    x = _layer0(x, tabs, even_w_in[0], even_q_norm[0], even_w_uq[0], even_kv_norm[0], even_w_ukv[0],
                even_w_out[0], even_ln_w[0], even_ln_b[0])
    x = _layer1(x, tabs, odd_w_in[0], odd_conv_w[0], odd_a_log[0], odd_dt_bias[0], odd_gdn_norm[0],
                odd_ret_norm_w[0], odd_ret_norm_b[0], odd_w_out[0], odd_ln_w[0], odd_ln_b[0])
    return x
```

```python
import functools

import numpy as np
import jax
import jax.numpy as jnp
from jax import lax
from jax.experimental import pallas as pl
from jax.experimental.pallas import tpu as pltpu

F32 = jnp.float32
BF16 = jnp.bfloat16

D_MODEL = 1024
DEPTH = 2
DEEPNORM_ALPHA = (2 * DEPTH) ** 0.25
ROPE_THETA = 10000.0
RMS_EPS = 1e-6
LN_EPS = 1e-5
L2_EPS = 1e-6

MLA_HEADS = 8
MLA_NOPE = 128
MLA_ROPE = 64
MLA_V = 128
Q_LORA = 768
KV_LORA = 256

DIL_HEADS = 8
DIL_HD = 128
DILATED_GROUPS = ((128, 1), (512, 4), (2048, 16))

GDN_HEADS = 8
GDN_DK = 128
GDN_DV = 128
CONV_K = 5
GDN_CHUNK = 128

RET_HEADS = 8
RET_DK = 64
RET_DV = 128
RET_DECAY_BASE = 5.0
RET_CHUNK = 256

LANES = 128
VMEM_LIMIT = 56 * 1024 * 1024


def _cparams(sem):
    return pltpu.CompilerParams(dimension_semantics=sem, vmem_limit_bytes=VMEM_LIMIT)


def _dot(a, b):
    return jnp.dot(a, b, preferred_element_type=F32)


def _dot_nt(a, b):
    return lax.dot_general(a, b, (((1,), (1,)), ((), ())), preferred_element_type=F32)


def _dot_tn(a, b):
    return lax.dot_general(a, b, (((0,), (0,)), ((), ())), preferred_element_type=F32)


def _const_spec(shape):
    nd = len(shape)
    return pl.BlockSpec(shape, lambda *_: (0,) * nd)


def _rope_pairs64(x, cos, sin_signed):
    lane = lax.broadcasted_iota(jnp.int32, x.shape, 1)
    swapped = jnp.where((lane % 64) < 32, pltpu.roll(x, 96, 1), pltpu.roll(x, 32, 1))
    return x * cos + swapped * sin_signed


def _rope_full128(x, cos, sin_signed):
    return x * cos + pltpu.roll(x, 64, 1) * sin_signed


def _silu(x):
    return x * jax.nn.sigmoid(x)


def _rope_tables_kernel(pos_ref, inv_ref, sgn_ref, c64_ref, s64_ref, c128_ref, s128_ref):
    pos = pos_ref[0]
    a64 = pos * inv_ref[0:1, :]
    a128 = pos * inv_ref[1:2, :]
    c64_ref[0] = jnp.cos(a64)
    s64_ref[0] = jnp.sin(a64) * sgn_ref[0:1, :]
    c128_ref[0] = jnp.cos(a128)
    s128_ref[0] = jnp.sin(a128) * sgn_ref[1:2, :]


def _rope_tables(positions):
    bsz, seq = positions.shape
    ts = 512
    inv32 = ROPE_THETA ** (-jnp.arange(0, 64, 2, dtype=F32) / 64)
    inv64 = ROPE_THETA ** (-jnp.arange(0, 128, 2, dtype=F32) / 128)
    inv = jnp.stack([jnp.tile(inv32, 4), jnp.tile(inv64, 2)])
    sgn = jnp.asarray(np.stack([np.tile(np.repeat([-1.0, 1.0], 32), 2), np.repeat([-1.0, 1.0], 64)]), F32)
    pos = positions.astype(F32)[..., None]
    tab = jax.ShapeDtypeStruct((bsz, seq, LANES), F32)
    tspec = pl.BlockSpec((1, ts, LANES), lambda b, s: (b, s, 0))
    return pl.pallas_call(
        _rope_tables_kernel,
        grid=(bsz, seq // ts),
        in_specs=[pl.BlockSpec((1, ts, 1), lambda b, s: (b, s, 0)), _const_spec((2, LANES)), _const_spec((2, LANES))],
        out_specs=[tspec] * 4,
        out_shape=[tab] * 4,
        compiler_params=_cparams(("parallel", "parallel")),
        name="rope_tables",
    )(pos, inv, sgn)


def _rms(x, w):
    return x * lax.rsqrt(jnp.mean(x * x, axis=-1, keepdims=True) + RMS_EPS) * w


def _in0_kernel(x_ref, wq_ref, wkv_ref, wkr_ref, wb_ref, wz_ref, wuq_ref, wukv_ref, qnorm_ref, kvnorm_ref,
                c64_ref, s64_ref, c128_ref, s128_ref,
                qn_ref, qr_ref, kv_ref, kr_ref, qb_ref, kb_ref, vb_ref, g_ref):
    x = x_ref[0].astype(BF16)
    c64, s64, c128, s128 = c64_ref[0], s64_ref[0], c128_ref[0], s128_ref[0]
    cq = _rms(_dot(x, wq_ref[...]), qnorm_ref[...])
    qa = _dot(cq.astype(BF16), wuq_ref[...]) * ((MLA_NOPE + MLA_ROPE) ** -0.5 * LOG2E)
    nope = MLA_HEADS * MLA_NOPE
    qn_ref[0] = qa[:, :nope].astype(BF16)
    for j in range(MLA_HEADS * MLA_ROPE // LANES):
        blk = qa[:, nope + j * LANES:nope + (j + 1) * LANES]
        qr_ref[0, :, j * LANES:(j + 1) * LANES] = _rope_pairs64(blk, c64, s64).astype(BF16)
    ckv = _rms(_dot(x, wkv_ref[...]), kvnorm_ref[...])
    kv_ref[0] = _dot(ckv.astype(BF16), wukv_ref[...]).astype(BF16)
    kr_ref[0] = _rope_pairs64(_dot(x, wkr_ref[...]), c64, s64).astype(BF16)
    hb = _dot(x, wb_ref[...])
    width = DIL_HEADS * DIL_HD
    for h in range(DIL_HEADS):
        sl = slice(h * DIL_HD, (h + 1) * DIL_HD)
        qb_ref[0, :, sl] = _rope_full128(hb[:, sl], c128, s128) * (DIL_HD ** -0.5)
        kb_ref[0, :, sl] = _rope_full128(hb[:, width + h * DIL_HD:width + (h + 1) * DIL_HD], c128, s128)
    vb_ref[0] = hb[:, 2 * width:]
    g_ref[0] = _silu(_dot(x, wz_ref[...])).astype(BF16)


def _in0(x, tabs, w_in, q_norm, w_uq, kv_norm, w_ukv, tm=256):
    bsz, seq, _ = x.shape
    c64, s64, c128, s128 = tabs
    o = np.cumsum((0, Q_LORA, KV_LORA, MLA_ROPE, 1024, 1024, 1024, 2048))
    wq = w_in[:, o[0]:o[1]].astype(BF16)
    wkv = w_in[:, o[1]:o[2]].astype(BF16)
    wkr = jnp.tile(w_in[:, o[2]:o[3]], (1, 2)).astype(BF16)
    wb = w_in[:, o[3]:o[6]].astype(BF16)
    wz = w_in[:, o[6]:o[7]].astype(BF16)
    wuq3 = w_uq.reshape(Q_LORA, MLA_HEADS, MLA_NOPE + MLA_ROPE)
    wuq = jnp.concatenate([wuq3[:, :, :MLA_NOPE].reshape(Q_LORA, -1), wuq3[:, :, MLA_NOPE:].reshape(Q_LORA, -1)],
                          axis=1).astype(BF16)
    wukv = w_ukv.astype(BF16)
    weights = (wq, wkv, wkr, wb, wz, wuq, wukv, q_norm.reshape(1, -1), kv_norm.reshape(1, -1))

    def rows(width):
        return pl.BlockSpec((1, tm, width), lambda b, s: (b, s, 0))

    out_widths = (1024, 512, 2048, LANES, 1024, 1024, 1024, 2048)
    out_dtypes = (BF16, BF16, BF16, BF16, F32, F32, F32, BF16)
    return pl.pallas_call(
        _in0_kernel,
        grid=(bsz, seq // tm),
        in_specs=[rows(D_MODEL)] + [_const_spec(w.shape) for w in weights] + [rows(LANES)] * 4,
        out_specs=[rows(w) for w in out_widths],
        out_shape=[jax.ShapeDtypeStruct((bsz, seq, w), dt) for w, dt in zip(out_widths, out_dtypes)],
        compiler_params=_cparams(("parallel", "parallel")),
        name="in_proj0",
    )(x, *weights, c64, s64, c128, s128)


MLA_SUB = 256
LOG2E = 1.4426950408889634


def _mla_kernel(qn_ref, qr_ref, kn_ref, kr_ref, v_ref, o_ref, k_s, vt_s):
    h = pl.program_id(1)

    @pl.when(pl.program_id(2) == 0)
    def _():
        k_s[:, :MLA_NOPE] = kn_ref[0]
        k_s[:, MLA_NOPE:] = kr_ref[0]
        vt_s[...] = v_ref[0].astype(F32).T.astype(BF16)

    lane = lax.broadcasted_iota(jnp.int32, (MLA_SUB, LANES), 1)
    mine = (lane // MLA_ROPE) == (h % 2)
    nsub = qn_ref.shape[1] // MLA_SUB

    def scores(i):
        rows = slice(i * MLA_SUB, (i + 1) * MLA_SUB)
        qr = jnp.where(mine, qr_ref[0, rows, :], jnp.zeros((MLA_SUB, LANES), BF16))
        return _dot_nt(k_s[...], jnp.concatenate([qn_ref[0, rows, :], qr], axis=1))

    s_next = scores(0)
    for i in range(nsub):
        s = s_next
        if i + 1 < nsub:
            s_next = scores(i + 1)
        m = jnp.max(s, axis=0, keepdims=True)
        p = jnp.exp2(s - m)
        l = jnp.sum(p, axis=0, keepdims=True)
        ot = _dot(vt_s[...], p.astype(BF16)) / l
        o_ref[0, i * MLA_SUB:(i + 1) * MLA_SUB, :] = ot.T.astype(o_ref.dtype)


def _mla(qn, qr, kv, kr, tq=2048):
    bsz, seq, _ = qn.shape
    return pl.pallas_call(
        _mla_kernel,
        grid=(bsz, MLA_HEADS, seq // tq),
        in_specs=[
            pl.BlockSpec((1, tq, LANES), lambda b, h, i: (b, i, h)),
            pl.BlockSpec((1, tq, LANES), lambda b, h, i: (b, i, h // 2)),
            pl.BlockSpec((1, seq, LANES), lambda b, h, i: (b, 0, 2 * h)),
            pl.BlockSpec((1, seq, LANES), lambda b, h, i: (b, 0, 0)),
            pl.BlockSpec((1, seq, LANES), lambda b, h, i: (b, 0, 2 * h + 1)),
        ],
        out_specs=pl.BlockSpec((1, tq, LANES), lambda b, h, i: (b, i, h)),
        out_shape=jax.ShapeDtypeStruct((bsz, seq, MLA_HEADS * MLA_V), BF16),
        scratch_shapes=[pltpu.VMEM((seq, MLA_NOPE + LANES), BF16), pltpu.VMEM((MLA_V, seq), BF16)],
        compiler_params=_cparams(("parallel", "parallel", "arbitrary")),
        name="mla_attention",
    )(qn, qr, kv, kr, kv)


DIL_TQ = 128
DIL_TILES_PER_STEP = 4


def _dil_kernel(q_ref, k_ref, v_ref, o_ref, m_s, l_s, acc_s):
    seq = q_ref.shape[1]

    def rows_of(start, size, dil):
        return pl.ds(start, size) if dil == 1 else pl.ds(start, size, stride=dil)

    def tiles_step(window, dil, which, first):
        radius = window // (2 * dil)
        length = seq // dil
        kw = min(length, DIL_TQ + 2 * radius)
        n = len(which)
        ws = [jnp.clip(t * DIL_TQ - radius, 0, length - kw) for _, t in which]
        qrows = [rows_of(r + t * (DIL_TQ * dil), DIL_TQ, dil) for r, t in which]
        krows = [rows_of(r + ws[i] * dil, kw, dil) for i, (r, _) in enumerate(which)]
        s = [_dot_nt(q_ref[0, qrows[i], :].astype(BF16), k_ref[0, krows[i], :].astype(BF16)) for i in range(n)]
        col = lax.broadcasted_iota(jnp.int32, (DIL_TQ, kw), 1)
        row = lax.broadcasted_iota(jnp.int32, (DIL_TQ, kw), 0)
        rel = [(ws[i] + col) - (t * DIL_TQ + row) for i, (_, t) in enumerate(which)]
        s = [jnp.where(jnp.abs(rel[i]) <= radius, s[i], -jnp.inf) for i in range(n)]
        m = [jnp.max(t, axis=-1, keepdims=True) for t in s]
        p = [jnp.exp(s[i] - m[i]) for i in range(n)]
        l = [jnp.sum(t, axis=-1, keepdims=True) for t in p]
        o = [_dot(p[i].astype(BF16), v_ref[0, krows[i], :].astype(BF16)) for i in range(n)]
        for i in range(n):
            if first:
                m_s[qrows[i], :] = m[i]
                l_s[qrows[i], :] = l[i]
                acc_s[qrows[i], :] = o[i]
            else:
                m_old = m_s[qrows[i], :]
                m_new = jnp.maximum(m_old, m[i])
                a_old = jnp.exp(m_old - m_new)
                a_new = jnp.exp(m[i] - m_new)
                m_s[qrows[i], :] = m_new
                l_s[qrows[i], :] = a_old * l_s[qrows[i], :] + a_new * l[i]
                acc_s[qrows[i], :] = a_old * acc_s[qrows[i], :] + a_new * o[i]

    for g, (window, dil) in enumerate(DILATED_GROUPS):
        tiles = seq // dil // DIL_TQ

        def body(it, carry, window=window, dil=dil, tiles=tiles, first=(g == 0)):
            idx = [it * DIL_TILES_PER_STEP + j for j in range(DIL_TILES_PER_STEP)]
            tiles_step(window, dil, [(i // tiles, i % tiles) for i in idx], first)
            return carry

        lax.fori_loop(0, dil * tiles // DIL_TILES_PER_STEP, body, 0)
    o_ref[0] = (acc_s[...] / l_s[...]).astype(o_ref.dtype)


def _dilated(qb, kb, vb):
    bsz, seq, _ = qb.shape
    spec = pl.BlockSpec((1, seq, DIL_HD), lambda b, h: (b, 0, h))
    return pl.pallas_call(
        _dil_kernel,
        grid=(bsz, DIL_HEADS),
        in_specs=[spec] * 3,
        out_specs=spec,
        out_shape=jax.ShapeDtypeStruct((bsz, seq, DIL_HEADS * DIL_HD), BF16),
        scratch_shapes=[pltpu.VMEM((seq, 1), F32), pltpu.VMEM((seq, 1), F32), pltpu.VMEM((seq, DIL_HD), F32)],
        compiler_params=_cparams(("parallel", "parallel")),
        name="dilated_attention",
    )(qb, kb, vb)


def _out_kernel(o1_ref, o2_ref, g_ref, x_ref, w_ref, lnw_ref, lnb_ref, out_ref):
    half = o1_ref.shape[-1]
    g = g_ref[0].astype(F32)
    y1 = (o1_ref[0].astype(F32) * g[:, :half]).astype(BF16)
    y2 = (o2_ref[0].astype(F32) * g[:, half:]).astype(BF16)
    r = DEEPNORM_ALPHA * x_ref[0] + _dot(y1, w_ref[:half, :]) + _dot(y2, w_ref[half:, :])
    mu = jnp.mean(r, axis=-1, keepdims=True)
    rc = r - mu
    var = jnp.mean(rc * rc, axis=-1, keepdims=True)
    out_ref[0] = rc * lax.rsqrt(var + LN_EPS) * lnw_ref[...] + lnb_ref[...]


def _out_proj(o1, o2, gate, x, w_out, ln_w, ln_b, tm=512):
    bsz, seq, _ = x.shape
    half = o1.shape[-1]

    def rows(width):
        return pl.BlockSpec((1, tm, width), lambda b, s: (b, s, 0))

    w = w_out.astype(BF16)
    return pl.pallas_call(
        _out_kernel,
        grid=(bsz, seq // tm),
        in_specs=[rows(half), rows(half), rows(2 * half), rows(D_MODEL), _const_spec(w.shape),
                  _const_spec((1, D_MODEL)), _const_spec((1, D_MODEL))],
        out_specs=rows(D_MODEL),
        out_shape=jax.ShapeDtypeStruct((bsz, seq, D_MODEL), F32),
        compiler_params=_cparams(("parallel", "parallel")),
        name="out_proj",
    )(o1, o2, gate, x, w, ln_w.reshape(1, -1), ln_b.reshape(1, -1))


def _layer0(x, tabs, w_in, q_norm, w_uq, kv_norm, w_ukv, w_out, ln_w, ln_b):
    qn, qr, kv, kr, qb, kb, vb, gate = _in0(x, tabs, w_in, q_norm, w_uq, kv_norm, w_ukv)
    o_a = _mla(qn, qr, kv, kr)
    o_b = _dilated(qb, kb, vb)
    return _out_proj(o_a, o_b, gate, x, w_out, ln_w, ln_b)


GATE_BETA = 0
GATE_GC = 2 * GDN_HEADS


def _softplus(x):
    return jnp.maximum(x, 0.0) + jnp.log1p(jnp.exp(-jnp.abs(x)))


def _in1_kernel(x_ref, wc_ref, wg_ref, wqk_ref, wv_ref, wz_ref, gp_ref, c64_ref, s64_ref,
                qkvc_ref, gate_ref, qr_ref, kr_ref, vr_ref, g_ref):
    x = x_ref[0].astype(BF16)
    tm = x.shape[0]
    c64, s64 = c64_ref[0], s64_ref[0]
    qkvc_ref[0] = _dot(x, wc_ref[...])
    hg = _dot(x, wg_ref[...])
    beta = jax.nn.sigmoid(hg)
    g = -jnp.exp(gp_ref[0:1, :]) * _softplus(hg + gp_ref[1:2, :])
    pos = lax.broadcasted_iota(jnp.int32, hg.shape, 0) % GDN_CHUNK
    pre, suf = g, g
    step = 1
    while step < GDN_CHUNK:
        pre = pre + jnp.where(pos >= step, pltpu.roll(pre, step, 0), 0.0)
        suf = suf + jnp.where(pos < GDN_CHUNK - step, pltpu.roll(suf, tm - step, 0), 0.0)
        step *= 2
    lane = lax.broadcasted_iota(jnp.int32, hg.shape, 1)
    gc = jnp.where(lane < GATE_GC + GDN_HEADS, pre, suf)
    gate_ref[0] = jnp.where(lane < GATE_GC, beta, gc)
    hqk = _dot(x, wqk_ref[...])
    width = RET_HEADS * RET_DK
    for j in range(width // LANES):
        qr_ref[0, :, j * LANES:(j + 1) * LANES] = _rope_pairs64(hqk[:, j * LANES:(j + 1) * LANES], c64, s64).astype(BF16)
        kblk = hqk[:, width + j * LANES:width + (j + 1) * LANES]
        kr_ref[0, :, j * LANES:(j + 1) * LANES] = (_rope_pairs64(kblk, c64, s64) * (RET_DK ** -0.5)).astype(BF16)
    vr_ref[0] = _dot(x, wv_ref[...]).astype(BF16)
    g_ref[0] = _silu(_dot(x, wz_ref[...])).astype(BF16)


def _in1(x, tabs, w_in, a_log, dt_bias, tm=256):
    bsz, seq, _ = x.shape
    c64, s64, _, _ = tabs
    conv_ch = 2 * GDN_HEADS * GDN_DK + GDN_HEADS * GDN_DV
    o = np.cumsum((0, conv_ch, 2 * GDN_HEADS, 2 * GDN_HEADS, 512, 512, 1024, 2048))
    wc = w_in[:, o[0]:o[1]].astype(BF16)
    wg = jnp.pad(w_in[:, o[1]:o[3]], ((0, 0), (0, LANES - 4 * GDN_HEADS))).astype(BF16)
    wqk = w_in[:, o[3]:o[5]].astype(BF16)
    wv = w_in[:, o[5]:o[6]].astype(BF16)
    wz = w_in[:, o[6]:o[7]].astype(BF16)
    pad = (GATE_GC, LANES - GATE_GC - 2 * GDN_HEADS)
    gp = jnp.stack([jnp.pad(a_log.reshape(-1), pad), jnp.pad(dt_bias.reshape(-1), pad)]).astype(F32)
    weights = (wc, wg, wqk, wv, wz, gp)

    def rows(width):
        return pl.BlockSpec((1, tm, width), lambda b, s: (b, s, 0))

    out_widths = (conv_ch, LANES, 512, 512, 1024, 2048)
    out_dtypes = (F32, F32, BF16, BF16, BF16, BF16)
    return pl.pallas_call(
        _in1_kernel,
        grid=(bsz, seq // tm),
        in_specs=[rows(D_MODEL)] + [_const_spec(w.shape) for w in weights] + [rows(LANES)] * 2,
        out_specs=[rows(w) for w in out_widths],
        out_shape=[jax.ShapeDtypeStruct((bsz, seq, w), dt) for w, dt in zip(out_widths, out_dtypes)],
        compiler_params=_cparams(("parallel", "parallel")),
        name="in_proj1",
    )(x, *weights, c64, s64)


CONV_PAD = 8
GDN_P1_CHUNKS = 4
GDN_P3_CHUNKS = 4


def _gdn_kernel(qraw_ref, kraw_ref, vraw_ref, wq_ref, wk_ref, wv_ref, gcol_ref, grow_ref, norm_ref, o_ref,
                xpad_s, q_s, k_s, v_s, bcb_s, gcb_s, w_s, u_s, att_s, mt_s, bt_s, st_s):
    seq = q_s.shape[0]
    nchunk = seq // GDN_CHUNK

    zeros = jnp.zeros((CONV_PAD, LANES), F32)
    xpad_s[0:CONV_PAD, :] = zeros
    xpad_s[CONV_PAD + seq:, :] = zeros

    def conv_silu(raw_ref, w_ref):
        xpad_s[CONV_PAD:CONV_PAD + seq, :] = raw_ref[0]
        acc = None
        for j in range(CONV_K):
            off = CONV_PAD - (CONV_K - 1) // 2 + j
            term = xpad_s[off:off + seq, :] * w_ref[j:j + 1, :]
            acc = term if acc is None else acc + term
        return _silu(acc)

    def l2norm(t):
        return t * lax.rsqrt(jnp.sum(t * t, axis=-1, keepdims=True) + L2_EPS)

    q_s[...] = l2norm(conv_silu(qraw_ref, wq_ref)) * (GDN_DK ** -0.5)
    k_s[...] = l2norm(conv_silu(kraw_ref, wk_ref))
    v_s[...] = conv_silu(vraw_ref, wv_ref)

    for d in range(2):
        bcb_s[d] = jnp.broadcast_to(gcol_ref[0, 0, :, d:d + 1], (seq, LANES))
        gcb_s[d] = jnp.broadcast_to(gcol_ref[0, 0, :, 2 + d:3 + d], (seq, LANES))

    ii = lax.broadcasted_iota(jnp.int32, (GDN_CHUNK, GDN_CHUNK), 0)
    jj = lax.broadcasted_iota(jnp.int32, (GDN_CHUNK, GDN_CHUNK), 1)
    causal = (ii >= jj, ii <= jj)
    strict = (ii > jj, ii < jj)

    def chunk_rows(c):
        base = pl.multiple_of(c * GDN_CHUNK, GDN_CHUNK)
        return base, pl.ds(base, GDN_CHUNK)

    def chunk_log_decay(d, base):
        return gcb_s[d, pl.ds(base + (GDN_CHUNK - 1 if d == 0 else 0), 1), :]

    def phase1(it, carry):
        chains = [(it * GDN_P1_CHUNKS + j, d) for j in range(GDN_P1_CHUNKS) for d in range(2)]
        rows = [chunk_rows(c) for c, _ in chains]
        kf = [k_s[r, :] for _, r in rows]
        kc = [t.astype(BF16) for t in kf]
        gcb = [gcb_s[d, r, :] for (_, d), (_, r) in zip(chains, rows)]
        bcb = [bcb_s[d, r, :] for (_, d), (_, r) in zip(chains, rows)]
        kb = [kf[i] * bcb[i] for i in range(len(chains))]
        kk = [_dot_nt(kb[i].astype(BF16), kc[i]) for i in range(len(chains))]
        decay = [jnp.exp(jnp.where(causal[d], gcb[i] - grow_ref[0, 0, d, pl.ds(c, 1), :], -jnp.inf))
                 for i, (c, d) in enumerate(chains)]
        a = [jnp.where(strict[d], kk[i] * decay[i], 0.0) for i, (_, d) in enumerate(chains)]
        n = [-t for t in a]
        xp = a
        for _ in range(6):
            xb = [t.astype(BF16) for t in xp]
            xp = [_dot(t, t) for t in xb]
            n = [n[i] + xp[i] + _dot(n[i].astype(BF16), xp[i].astype(BF16)) for i in range(len(chains))]
        nb = [t.astype(BF16) for t in n]
        wpre = [kb[i] * jnp.exp(gcb[i]) for i in range(len(chains))]
        upre = [v_s[r, :] * bcb[i] for i, (_, r) in enumerate(rows)]
        w = [(wpre[i] + _dot(nb[i], wpre[i].astype(BF16))).astype(BF16) for i in range(len(chains))]
        u = [upre[i] + _dot(nb[i], upre[i].astype(BF16)) for i in range(len(chains))]
        qkc = [_dot_nt(q_s[rows[2 * j][1], :].astype(BF16), kc[2 * j]) for j in range(GDN_P1_CHUNKS)]
        qk = [qkc[i // 2] for i in range(len(chains))]
        kd = [(kf[i] * jnp.exp(chunk_log_decay(d, rows[i][0]) - gcb[i])).astype(BF16) for i, (_, d) in enumerate(chains)]
        mt = [_dot_tn(w[i], kd[i]) for i in range(len(chains))]
        bt = [_dot_tn(u[i].astype(BF16), kd[i]) for i in range(len(chains))]
        for i, (c, d) in enumerate(chains):
            r = rows[i][1]
            w_s[d, r, :] = w[i]
            u_s[d, r, :] = u[i]
            att_s[d, r, :] = (qk[i] * decay[i]).astype(BF16)
            mt_s[d, c] = (-mt[i]).astype(BF16)
            bt_s[d, c] = bt[i]
        return carry

    lax.fori_loop(0, nchunk // GDN_P1_CHUNKS, phase1, 0)

    def phase2(i, carry):
        new = []
        for d in range(2):
            c = i if d == 0 else nchunk - 1 - i
            st = carry[d]
            st_s[d, c] = st.astype(BF16)
            gl = chunk_log_decay(d, pl.multiple_of(c * GDN_CHUNK, GDN_CHUNK))
            new.append(st * jnp.exp(gl) + _dot(st.astype(BF16), mt_s[d, c]) + bt_s[d, c])
        return tuple(new)

    zero_state = jnp.zeros((GDN_DV, GDN_DK), F32)
    lax.fori_loop(0, nchunk, phase2, (zero_state, zero_state))

    def phase3(it, carry):
        chains = [(it * GDN_P3_CHUNKS + j, d) for j in range(GDN_P3_CHUNKS) for d in range(2)]
        rows = [chunk_rows(c)[1] for c, _ in chains]
        qg = [(q_s[r, :] * jnp.exp(gcb_s[d, r, :])).astype(BF16) for (_, d), r in zip(chains, rows)]
        m = [_dot_nt(jnp.concatenate([w_s[d, rows[i], :], qg[i]], axis=0), st_s[d, c]) for i, (c, d) in enumerate(chains)]
        vnew = [(u_s[d, rows[i], :] - m[i][:GDN_CHUNK]).astype(BF16) for i, (_, d) in enumerate(chains)]
        od = [m[i][GDN_CHUNK:] + _dot(att_s[d, rows[i], :], vnew[i]) for i, (_, d) in enumerate(chains)]
        for j in range(GDN_P3_CHUNKS):
            o = od[2 * j] + od[2 * j + 1]
            o_ref[0, rows[2 * j], :] = (o * lax.rsqrt(jnp.mean(o * o, axis=-1, keepdims=True) + RMS_EPS)
                                        * norm_ref[...]).astype(o_ref.dtype)
        return carry

    lax.fori_loop(0, nchunk // GDN_P3_CHUNKS, phase3, 0)


def _gdn(qkvc, conv_w, gcol, grow, gdn_norm):
    bsz, seq, _ = qkvc.shape
    nh = GDN_HEADS

    def col(off):
        return pl.BlockSpec((1, seq, LANES), lambda b, h: (b, 0, off + h))

    def wcol(off):
        return pl.BlockSpec((CONV_K, LANES), lambda b, h: (0, off + h))

    nchunk = seq // GDN_CHUNK
    big = pltpu.VMEM((seq, LANES), F32)

    def per_dir(dtype):
        return pltpu.VMEM((2, seq, LANES), dtype)

    def per_chunk(dtype):
        return pltpu.VMEM((2, nchunk, GDN_DV, GDN_DK), dtype)

    return pl.pallas_call(
        _gdn_kernel,
        grid=(bsz, nh),
        in_specs=[col(0), col(nh), col(2 * nh), wcol(0), wcol(nh), wcol(2 * nh),
                  pl.BlockSpec((1, 1, seq, 4), lambda b, h: (b, h, 0, 0)),
                  pl.BlockSpec((1, 1, 2, nchunk, GDN_CHUNK), lambda b, h: (b, h, 0, 0, 0)),
                  _const_spec((1, GDN_DV))],
        out_specs=pl.BlockSpec((1, seq, GDN_DV), lambda b, h: (b, 0, h)),
        out_shape=jax.ShapeDtypeStruct((bsz, seq, nh * GDN_DV), BF16),
        scratch_shapes=[pltpu.VMEM((seq + 2 * CONV_PAD, LANES), F32), big, big, big,
                        per_dir(F32), per_dir(F32), per_dir(BF16), per_dir(F32), per_dir(BF16),
                        per_chunk(BF16), per_chunk(F32), per_chunk(BF16)],
        compiler_params=_cparams(("parallel", "parallel")),
        name="gated_deltanet",
    )(qkvc, qkvc, qkvc, conv_w, conv_w, conv_w, gcol, grow, gdn_norm.reshape(1, -1))


def _ret_tables():
    heads = np.arange(RET_HEADS, dtype=np.float64)
    log_gamma = np.log1p(-(2.0 ** (-RET_DECAY_BASE - heads)))[:, None]
    idx = np.arange(RET_CHUNK, dtype=np.float64)
    dmat = np.exp(log_gamma[:, :, None] * np.abs(idx[:, None] - idx[None, :])[None])
    xi = np.exp(log_gamma * (idx + 1.0))
    zeta = np.exp(log_gamma * (RET_CHUNK - 1.0 - idx))
    gch = np.exp(log_gamma * RET_CHUNK)

    def cb(t):
        return jnp.asarray(np.broadcast_to(t[:, :, None], t.shape + (LANES,)), F32)

    return jnp.asarray(dmat, F32), cb(xi), cb(zeta), cb(gch)


def _ret_kernel(q_ref, k_ref, v_ref, d_ref, xi_ref, zeta_ref, gch_ref, nw_ref, nb_ref, o_ref, kvf_s, kvb_s):
    seq = q_ref.shape[1]
    nchunk = seq // RET_CHUNK
    h = pl.program_id(1)
    xi, zeta, gch = xi_ref[0], zeta_ref[0], gch_ref[0]
    lane = lax.broadcasted_iota(jnp.int32, (RET_CHUNK, LANES), 1)
    mine = (lane // RET_DK) == (h % 2)

    def rows(c):
        return slice(c * RET_CHUNK, (c + 1) * RET_CHUNK)

    for c in range(nchunk):
        kc = k_ref[0, rows(c), :].astype(F32)
        vc = v_ref[0, rows(c), :]
        kvf_s[c] = _dot_tn((kc * zeta).astype(BF16), vc)
        kvb_s[c] = _dot_tn((kc * xi).astype(BF16), vc)
    state = jnp.zeros((LANES, RET_DV), F32)
    for c in range(nchunk):
        nxt = state * gch + kvf_s[c]
        kvf_s[c] = state
        state = nxt
    state = jnp.zeros((LANES, RET_DV), F32)
    for c in reversed(range(nchunk)):
        nxt = state * gch + kvb_s[c]
        kvb_s[c] = state
        state = nxt
    for c in range(nchunk):
        q = jnp.where(mine, q_ref[0, rows(c), :], jnp.zeros((RET_CHUNK, LANES), BF16))
        qf = q.astype(F32)
        s = _dot_nt(q, k_ref[0, rows(c), :]) * d_ref[0]
        o = _dot(s.astype(BF16), v_ref[0, rows(c), :])
        o = o + _dot((qf * xi).astype(BF16), kvf_s[c].astype(BF16))
        o = o + _dot((qf * zeta).astype(BF16), kvb_s[c].astype(BF16))
        mu = jnp.mean(o, axis=-1, keepdims=True)
        oc = o - mu
        var = jnp.mean(oc * oc, axis=-1, keepdims=True)
        o_ref[0, rows(c), :] = (oc * lax.rsqrt(var + LN_EPS) * nw_ref[...] + nb_ref[...]).astype(o_ref.dtype)


def _retention(qr, kr, vr, norm_w, norm_b):
    bsz, seq, _ = vr.shape
    dmat, xi, zeta, gch = _ret_tables()
    pair = pl.BlockSpec((1, seq, LANES), lambda b, h: (b, 0, h // 2))
    head = pl.BlockSpec((1, seq, RET_DV), lambda b, h: (b, 0, h))

    def per_head(shape):
        return pl.BlockSpec((1,) + shape, lambda b, h: (h, 0, 0))

    vec = pl.BlockSpec((1, RET_DV), lambda b, h: (0, h))
    state = pltpu.VMEM((seq // RET_CHUNK, LANES, RET_DV), F32)
    return pl.pallas_call(
        _ret_kernel,
        grid=(bsz, RET_HEADS),
        in_specs=[pair, pair, head, per_head((RET_CHUNK, RET_CHUNK)), per_head((RET_CHUNK, LANES)),
                  per_head((RET_CHUNK, LANES)), per_head((1, LANES)), vec, vec],
        out_specs=head,
        out_shape=jax.ShapeDtypeStruct((bsz, seq, RET_HEADS * RET_DV), BF16),
        scratch_shapes=[state, state],
        compiler_params=_cparams(("parallel", "parallel")),
        name="retention",
    )(qr, kr, vr, dmat, xi, zeta, gch, norm_w.reshape(1, -1), norm_b.reshape(1, -1))


def _layer1(x, tabs, w_in, conv_w, a_log, dt_bias, gdn_norm, ret_norm_w, ret_norm_b, w_out, ln_w, ln_b):
    bsz, seq, _ = x.shape
    qkvc, gates, qr, kr, vr, gate = _in1(x, tabs, w_in, a_log, dt_bias)
    gsl = gates[:, :, :4 * GDN_HEADS].reshape(bsz, seq, 2, 2, GDN_HEADS)
    gcol = gsl.transpose(0, 4, 1, 2, 3).reshape(bsz, GDN_HEADS, seq, 4)
    grow = gsl[:, :, 1].transpose(0, 3, 2, 1).reshape(bsz, GDN_HEADS, 2, seq // GDN_CHUNK, GDN_CHUNK)
    o_c = _gdn(qkvc, conv_w, gcol, grow, gdn_norm)
    o_d = _retention(qr, kr, vr, ret_norm_w, ret_norm_b)
    return _out_proj(o_c, o_d, gate, x, w_out, ln_w, ln_b)


def kernel(x, positions, even_w_in, even_q_norm, even_w_uq, even_kv_norm, even_w_ukv, even_w_out, even_ln_w, even_ln_b, odd_w_in, odd_conv_w, odd_a_log, odd_dt_bias, odd_gdn_norm, odd_ret_norm_w, odd_ret_norm_b, odd_w_out, odd_ln_w, odd_ln_b):
    tabs = _rope_tables(positions)
    x = _layer0(x, tabs, even_w_in[0], even_q_norm[0], even_w_uq[0], even_kv_norm[0], even_w_ukv[0],
                even_w_out[0], even_ln_w[0], even_ln_b[0])
    x = _layer1(x, tabs, odd_w_in[0], odd_conv_w[0], odd_a_log[0], odd_dt_bias[0], odd_gdn_norm[0],
                odd_ret_norm_w[0], odd_ret_norm_b[0], odd_w_out[0], odd_ln_w[0], odd_ln_b[0])
    return x
```

```python
import functools

import numpy as np
import jax
import jax.numpy as jnp
from jax import lax
from jax.experimental import pallas as pl
from jax.experimental.pallas import tpu as pltpu

F32 = jnp.float32
BF16 = jnp.bfloat16

D_MODEL = 1024
DEPTH = 2
DEEPNORM_ALPHA = (2 * DEPTH) ** 0.25
ROPE_THETA = 10000.0
RMS_EPS = 1e-6
LN_EPS = 1e-5
L2_EPS = 1e-6

MLA_HEADS = 8
MLA_NOPE = 128
MLA_ROPE = 64
MLA_V = 128
Q_LORA = 768
KV_LORA = 256

DIL_HEADS = 8
DIL_HD = 128
DILATED_GROUPS = ((128, 1), (512, 4), (2048, 16))

GDN_HEADS = 8
GDN_DK = 128
GDN_DV = 128
CONV_K = 5
GDN_CHUNK = 128

RET_HEADS = 8
RET_DK = 64
RET_DV = 128
RET_DECAY_BASE = 5.0
RET_CHUNK = 256

LANES = 128
VMEM_LIMIT = 56 * 1024 * 1024


def _cparams(sem):
    return pltpu.CompilerParams(dimension_semantics=sem, vmem_limit_bytes=VMEM_LIMIT)


def _dot(a, b):
    return jnp.dot(a, b, preferred_element_type=F32)


def _dot_nt(a, b):
    return lax.dot_general(a, b, (((1,), (1,)), ((), ())), preferred_element_type=F32)


def _dot_tn(a, b):
    return lax.dot_general(a, b, (((0,), (0,)), ((), ())), preferred_element_type=F32)


def _const_spec(shape):
    nd = len(shape)
    return pl.BlockSpec(shape, lambda *_: (0,) * nd, pipeline_mode=pl.Buffered(1))


def _rope_pairs64(x, cos, sin_signed):
    lane = lax.broadcasted_iota(jnp.int32, x.shape, 1)
    swapped = jnp.where((lane % 64) < 32, pltpu.roll(x, 96, 1), pltpu.roll(x, 32, 1))
    return x * cos + swapped * sin_signed


def _rope_full128(x, cos, sin_signed):
    return x * cos + pltpu.roll(x, 64, 1) * sin_signed


def _silu(x):
    return x * jax.nn.sigmoid(x)


def _rope_tables_kernel(pos_ref, inv_ref, sgn_ref, c64_ref, s64_ref, c128_ref, s128_ref):
    ang = pos_ref[0] * inv_ref[...]
    cos, sin = jnp.cos(ang), jnp.sin(ang)
    c32, s32 = cos[:, :32], sin[:, :32]
    c64, s64 = cos[:, 32:96], sin[:, 32:96]
    c64_ref[0] = jnp.concatenate([c32] * 4, axis=1)
    s64_ref[0] = jnp.concatenate([s32] * 4, axis=1) * sgn_ref[0:1, :]
    c128_ref[0] = jnp.concatenate([c64] * 2, axis=1)
    s128_ref[0] = jnp.concatenate([s64] * 2, axis=1) * sgn_ref[1:2, :]


def _rope_tables(positions):
    bsz, seq = positions.shape
    ts = 512
    inv32 = ROPE_THETA ** (-jnp.arange(0, 64, 2, dtype=F32) / 64)
    inv64 = ROPE_THETA ** (-jnp.arange(0, 128, 2, dtype=F32) / 128)
    inv = jnp.concatenate([inv32, inv64, jnp.zeros((32,), F32)])[None, :]
    sgn = jnp.asarray(np.stack([np.tile(np.repeat([-1.0, 1.0], 32), 2), np.repeat([-1.0, 1.0], 64)]), F32)
    pos = positions.astype(F32)[..., None]
    tab = jax.ShapeDtypeStruct((bsz, seq, LANES), F32)
    tspec = pl.BlockSpec((1, ts, LANES), lambda b, s: (b, s, 0))
    return pl.pallas_call(
        _rope_tables_kernel,
        grid=(bsz, seq // ts),
        in_specs=[pl.BlockSpec((1, ts, 1), lambda b, s: (b, s, 0)), _const_spec((1, LANES)), _const_spec((2, LANES))],
        out_specs=[tspec] * 4,
        out_shape=[tab] * 4,
        compiler_params=_cparams(("parallel", "parallel")),
        name="rope_tables",
    )(pos, inv, sgn)


def _rms(x, w):
    return x * lax.rsqrt(jnp.mean(x * x, axis=-1, keepdims=True) + RMS_EPS) * w


def _in0_kernel(x_ref, wq_ref, wkv_ref, wkr_ref, wb_ref, wz_ref, wuq_ref, wukv_ref, qnorm_ref, kvnorm_ref,
                c64_ref, s64_ref, c128_ref, s128_ref,
                qn_ref, qr_ref, kv_ref, kr_ref, qb_ref, kb_ref, vb_ref, g_ref):
    x = x_ref[0].astype(BF16)
    c64, s64, c128, s128 = c64_ref[0], s64_ref[0], c128_ref[0], s128_ref[0]
    cq = _rms(_dot(x, wq_ref[...]), qnorm_ref[...])
    qa = _dot(cq.astype(BF16), wuq_ref[...]) * ((MLA_NOPE + MLA_ROPE) ** -0.5 * LOG2E)
    nope = MLA_HEADS * MLA_NOPE
    qn_ref[0] = qa[:, :nope].astype(BF16)
    for j in range(MLA_HEADS * MLA_ROPE // LANES):
        blk = qa[:, nope + j * LANES:nope + (j + 1) * LANES]
        qr_ref[0, :, j * LANES:(j + 1) * LANES] = _rope_pairs64(blk, c64, s64).astype(BF16)
    ckv = _rms(_dot(x, wkv_ref[...]), kvnorm_ref[...])
    kv_ref[0] = _dot(ckv.astype(BF16), wukv_ref[...]).astype(BF16)
    kr_ref[0] = _rope_pairs64(_dot(x, wkr_ref[...]), c64, s64).astype(BF16)
    hb = _dot(x, wb_ref[...])
    width = DIL_HEADS * DIL_HD
    for h in range(DIL_HEADS):
        sl = slice(h * DIL_HD, (h + 1) * DIL_HD)
        qb_ref[0, :, sl] = _rope_full128(hb[:, sl], c128, s128) * (DIL_HD ** -0.5)
        kb_ref[0, :, sl] = _rope_full128(hb[:, width + h * DIL_HD:width + (h + 1) * DIL_HD], c128, s128)
    vb_ref[0] = hb[:, 2 * width:]
    g_ref[0] = _silu(_dot(x, wz_ref[...])).astype(BF16)


def _in0(x, tabs, w_in, q_norm, w_uq, kv_norm, w_ukv, tm=512):
    bsz, seq, _ = x.shape
    c64, s64, c128, s128 = tabs
    o = np.cumsum((0, Q_LORA, KV_LORA, MLA_ROPE, 1024, 1024, 1024, 2048))
    wq = w_in[:, o[0]:o[1]].astype(BF16)
    wkv = w_in[:, o[1]:o[2]].astype(BF16)
    wkr = jnp.tile(w_in[:, o[2]:o[3]], (1, 2)).astype(BF16)
    wb = w_in[:, o[3]:o[6]].astype(BF16)
    wz = w_in[:, o[6]:o[7]].astype(BF16)
    wuq3 = w_uq.reshape(Q_LORA, MLA_HEADS, MLA_NOPE + MLA_ROPE)
    wuq = jnp.concatenate([wuq3[:, :, :MLA_NOPE].reshape(Q_LORA, -1), wuq3[:, :, MLA_NOPE:].reshape(Q_LORA, -1)],
                          axis=1).astype(BF16)
    wukv = w_ukv.astype(BF16)
    weights = (wq, wkv, wkr, wb, wz, wuq, wukv, q_norm.reshape(1, -1), kv_norm.reshape(1, -1))

    def rows(width):
        return pl.BlockSpec((1, tm, width), lambda b, s: (b, s, 0))

    out_widths = (1024, 512, 2048, LANES, 1024, 1024, 1024, 2048)
    out_dtypes = (BF16, BF16, BF16, BF16, F32, F32, F32, BF16)
    return pl.pallas_call(
        _in0_kernel,
        grid=(bsz, seq // tm),
        in_specs=[rows(D_MODEL)] + [_const_spec(w.shape) for w in weights] + [rows(LANES)] * 4,
        out_specs=[rows(w) for w in out_widths],
        out_shape=[jax.ShapeDtypeStruct((bsz, seq, w), dt) for w, dt in zip(out_widths, out_dtypes)],
        compiler_params=_cparams(("parallel", "parallel")),
        name="in_proj0",
    )(x, *weights, c64, s64, c128, s128)


MLA_SUB = 256
MLA_ONES_ROWS = 16
MLA_AHEAD = 5
LOG2E = 1.4426950408889634


def _mla_kernel(qn_ref, qr_ref, kn_ref, kr_ref, v_ref, o_ref, k_s, vt_s):
    h = pl.program_id(1)

    @pl.when(pl.program_id(2) == 0)
    def _():
        k_s[:, :MLA_NOPE] = kn_ref[0]
        k_s[:, MLA_NOPE:] = kr_ref[0]
        vt_s[:MLA_V, :] = v_ref[0].astype(F32).T.astype(BF16)
        vt_s[MLA_V:, :] = jnp.ones((MLA_ONES_ROWS, vt_s.shape[1]), BF16)

    lane = lax.broadcasted_iota(jnp.int32, (MLA_SUB, LANES), 1)
    mine = (lane // MLA_ROPE) == (h % 2)
    nsub = qn_ref.shape[1] // MLA_SUB

    def scores(i):
        rows = slice(i * MLA_SUB, (i + 1) * MLA_SUB)
        qr = jnp.where(mine, qr_ref[0, rows, :], jnp.zeros((MLA_SUB, LANES), BF16))
        return _dot_nt(k_s[...], jnp.concatenate([qn_ref[0, rows, :], qr], axis=1))

    ahead = [scores(i) for i in range(min(MLA_AHEAD, nsub))]
    for i in range(nsub):
        s = ahead.pop(0)
        if i + MLA_AHEAD < nsub:
            ahead.append(scores(i + MLA_AHEAD))
        m = jnp.max(s, axis=0, keepdims=True)
        p = jnp.exp2(s - m).astype(BF16)
        ot = _dot(vt_s[...], p)
        o = ot[:MLA_V] / ot[MLA_V:MLA_V + 1]
        o_ref[0, i * MLA_SUB:(i + 1) * MLA_SUB, :] = o.T.astype(o_ref.dtype)


def _mla(qn, qr, kv, kr, tq=2048):
    bsz, seq, _ = qn.shape
    return pl.pallas_call(
        _mla_kernel,
        grid=(bsz, MLA_HEADS, seq // tq),
        in_specs=[
            pl.BlockSpec((1, tq, LANES), lambda b, h, i: (b, i, h)),
            pl.BlockSpec((1, tq, LANES), lambda b, h, i: (b, i, h // 2)),
            pl.BlockSpec((1, seq, LANES), lambda b, h, i: (b, 0, 2 * h)),
            pl.BlockSpec((1, seq, LANES), lambda b, h, i: (b, 0, 0)),
            pl.BlockSpec((1, seq, LANES), lambda b, h, i: (b, 0, 2 * h + 1)),
        ],
        out_specs=pl.BlockSpec((1, tq, LANES), lambda b, h, i: (b, i, h)),
        out_shape=jax.ShapeDtypeStruct((bsz, seq, MLA_HEADS * MLA_V), BF16),
        scratch_shapes=[pltpu.VMEM((seq, MLA_NOPE + LANES), BF16), pltpu.VMEM((MLA_V + MLA_ONES_ROWS, seq), BF16)],
        compiler_params=_cparams(("parallel", "parallel", "arbitrary")),
        name="mla_attention",
    )(qn, qr, kv, kr, kv)


DIL_TQ = 128
DIL_TILES_PER_STEP = 8


def _dil_kernel(q_ref, k_ref, v_ref, o_ref, m_s, l_s, acc_s):
    seq = q_ref.shape[1]

    def rows_of(start, size, dil):
        return pl.ds(start, size) if dil == 1 else pl.ds(start, size, stride=dil)

    def tiles_step(window, dil, which, first):
        radius = window // (2 * dil)
        length = seq // dil
        kw = min(length, DIL_TQ + 2 * radius)
        n = len(which)
        ws = [jnp.clip(t * DIL_TQ - radius, 0, length - kw) for _, t in which]
        qrows = [rows_of(r + t * (DIL_TQ * dil), DIL_TQ, dil) for r, t in which]
        krows = [rows_of(r + ws[i] * dil, kw, dil) for i, (r, _) in enumerate(which)]
        s = [_dot_nt(q_ref[0, qrows[i], :].astype(BF16), k_ref[0, krows[i], :].astype(BF16)) for i in range(n)]
        col = lax.broadcasted_iota(jnp.int32, (DIL_TQ, kw), 1)
        row = lax.broadcasted_iota(jnp.int32, (DIL_TQ, kw), 0)
        rel = [(ws[i] + col) - (t * DIL_TQ + row) for i, (_, t) in enumerate(which)]
        s = [jnp.where(jnp.abs(rel[i]) <= radius, s[i], -jnp.inf) for i in range(n)]
        m = [jnp.max(t, axis=-1, keepdims=True) for t in s]
        p = [jnp.exp(s[i] - m[i]) for i in range(n)]
        l = [jnp.sum(t, axis=-1, keepdims=True) for t in p]
        o = [_dot(p[i].astype(BF16), v_ref[0, krows[i], :].astype(BF16)) for i in range(n)]
        for i in range(n):
            if first:
                m_s[qrows[i], :] = m[i]
                l_s[qrows[i], :] = l[i]
                acc_s[qrows[i], :] = o[i]
            else:
                m_old = m_s[qrows[i], :]
                m_new = jnp.maximum(m_old, m[i])
                a_old = jnp.exp(m_old - m_new)
                a_new = jnp.exp(m[i] - m_new)
                m_s[qrows[i], :] = m_new
                l_s[qrows[i], :] = a_old * l_s[qrows[i], :] + a_new * l[i]
                acc_s[qrows[i], :] = a_old * acc_s[qrows[i], :] + a_new * o[i]

    for g, (window, dil) in enumerate(DILATED_GROUPS):
        tiles = seq // dil // DIL_TQ

        def body(it, carry, window=window, dil=dil, tiles=tiles, first=(g == 0)):
            idx = [it * DIL_TILES_PER_STEP + j for j in range(DIL_TILES_PER_STEP)]
            tiles_step(window, dil, [(i // tiles, i % tiles) for i in idx], first)
            return carry

        lax.fori_loop(0, dil * tiles // DIL_TILES_PER_STEP, body, 0)
    o_ref[0] = (acc_s[...] / l_s[...]).astype(o_ref.dtype)


def _dilated(qb, kb, vb):
    bsz, seq, _ = qb.shape
    spec = pl.BlockSpec((1, seq, DIL_HD), lambda b, h: (b, 0, h))
    return pl.pallas_call(
        _dil_kernel,
        grid=(bsz, DIL_HEADS),
        in_specs=[spec] * 3,
        out_specs=spec,
        out_shape=jax.ShapeDtypeStruct((bsz, seq, DIL_HEADS * DIL_HD), BF16),
        scratch_shapes=[pltpu.VMEM((seq, 1), F32), pltpu.VMEM((seq, 1), F32), pltpu.VMEM((seq, DIL_HD), F32)],
        compiler_params=_cparams(("parallel", "parallel")),
        name="dilated_attention",
    )(qb, kb, vb)


def _out_kernel(o1_ref, o2_ref, g_ref, x_ref, w_ref, lnw_ref, lnb_ref, out_ref):
    half = o1_ref.shape[-1]
    g = g_ref[0].astype(F32)
    y1 = (o1_ref[0].astype(F32) * g[:, :half]).astype(BF16)
    y2 = (o2_ref[0].astype(F32) * g[:, half:]).astype(BF16)
    r = DEEPNORM_ALPHA * x_ref[0] + _dot(y1, w_ref[:half, :]) + _dot(y2, w_ref[half:, :])
    mu = jnp.mean(r, axis=-1, keepdims=True)
    rc = r - mu
    var = jnp.mean(rc * rc, axis=-1, keepdims=True)
    out_ref[0] = rc * lax.rsqrt(var + LN_EPS) * lnw_ref[...] + lnb_ref[...]


def _out_proj(o1, o2, gate, x, w_out, ln_w, ln_b, tm=512):
    bsz, seq, _ = x.shape
    half = o1.shape[-1]

    def rows(width):
        return pl.BlockSpec((1, tm, width), lambda b, s: (b, s, 0))

    w = w_out.astype(BF16)
    return pl.pallas_call(
        _out_kernel,
        grid=(bsz, seq // tm),
        in_specs=[rows(half), rows(half), rows(2 * half), rows(D_MODEL), _const_spec(w.shape),
                  _const_spec((1, D_MODEL)), _const_spec((1, D_MODEL))],
        out_specs=rows(D_MODEL),
        out_shape=jax.ShapeDtypeStruct((bsz, seq, D_MODEL), F32),
        compiler_params=_cparams(("parallel", "parallel")),
        name="out_proj",
    )(o1, o2, gate, x, w, ln_w.reshape(1, -1), ln_b.reshape(1, -1))


def _layer0(x, tabs, w_in, q_norm, w_uq, kv_norm, w_ukv, w_out, ln_w, ln_b):
    qn, qr, kv, kr, qb, kb, vb, gate = _in0(x, tabs, w_in, q_norm, w_uq, kv_norm, w_ukv)
    o_a = _mla(qn, qr, kv, kr)
    o_b = _dilated(qb, kb, vb)
    return _out_proj(o_a, o_b, gate, x, w_out, ln_w, ln_b)


GATE_BETA = 0
GATE_GC = 2 * GDN_HEADS


def _softplus(x):
    return jnp.maximum(x, 0.0) + jnp.log1p(jnp.exp(-jnp.abs(x)))


def _in1_kernel(x_ref, wc_ref, wg_ref, wqk_ref, wv_ref, wz_ref, gp_ref, c64_ref, s64_ref,
                qkvc_ref, gate_ref, qr_ref, kr_ref, vr_ref, g_ref):
    x = x_ref[0].astype(BF16)
    tm = x.shape[0]
    c64, s64 = c64_ref[0], s64_ref[0]
    qkvc_ref[0] = _dot(x, wc_ref[...])
    hg = _dot(x, wg_ref[...])
    beta = jax.nn.sigmoid(hg)
    g = -jnp.exp(gp_ref[0:1, :]) * _softplus(hg + gp_ref[1:2, :])
    pos = lax.broadcasted_iota(jnp.int32, hg.shape, 0) % GDN_CHUNK
    pre, suf = g, g
    step = 1
    while step < GDN_CHUNK:
        pre = pre + jnp.where(pos >= step, pltpu.roll(pre, step, 0), 0.0)
        suf = suf + jnp.where(pos < GDN_CHUNK - step, pltpu.roll(suf, tm - step, 0), 0.0)
        step *= 2
    lane = lax.broadcasted_iota(jnp.int32, hg.shape, 1)
    gc = jnp.where(lane < GATE_GC + GDN_HEADS, pre, suf)
    gate_ref[0] = jnp.where(lane < GATE_GC, beta, gc)
    hqk = _dot(x, wqk_ref[...])
    width = RET_HEADS * RET_DK
    for j in range(width // LANES):
        qr_ref[0, :, j * LANES:(j + 1) * LANES] = _rope_pairs64(hqk[:, j * LANES:(j + 1) * LANES], c64, s64).astype(BF16)
        kblk = hqk[:, width + j * LANES:width + (j + 1) * LANES]
        kr_ref[0, :, j * LANES:(j + 1) * LANES] = (_rope_pairs64(kblk, c64, s64) * (RET_DK ** -0.5)).astype(BF16)
    vr_ref[0] = _dot(x, wv_ref[...]).astype(BF16)
    g_ref[0] = _silu(_dot(x, wz_ref[...])).astype(BF16)


def _in1(x, tabs, w_in, a_log, dt_bias, tm=512):
    bsz, seq, _ = x.shape
    c64, s64, _, _ = tabs
    conv_ch = 2 * GDN_HEADS * GDN_DK + GDN_HEADS * GDN_DV
    o = np.cumsum((0, conv_ch, 2 * GDN_HEADS, 2 * GDN_HEADS, 512, 512, 1024, 2048))
    wc = w_in[:, o[0]:o[1]].astype(BF16)
    wg = jnp.pad(w_in[:, o[1]:o[3]], ((0, 0), (0, LANES - 4 * GDN_HEADS))).astype(BF16)
    wqk = w_in[:, o[3]:o[5]].astype(BF16)
    wv = w_in[:, o[5]:o[6]].astype(BF16)
    wz = w_in[:, o[6]:o[7]].astype(BF16)
    pad = (GATE_GC, LANES - GATE_GC - 2 * GDN_HEADS)
    gp = jnp.stack([jnp.pad(a_log.reshape(-1), pad), jnp.pad(dt_bias.reshape(-1), pad)]).astype(F32)
    weights = (wc, wg, wqk, wv, wz, gp)

    def rows(width):
        return pl.BlockSpec((1, tm, width), lambda b, s: (b, s, 0))

    out_widths = (conv_ch, LANES, 512, 512, 1024, 2048)
    out_dtypes = (F32, F32, BF16, BF16, BF16, BF16)
    return pl.pallas_call(
        _in1_kernel,
        grid=(bsz, seq // tm),
        in_specs=[rows(D_MODEL)] + [_const_spec(w.shape) for w in weights] + [rows(LANES)] * 2,
        out_specs=[rows(w) for w in out_widths],
        out_shape=[jax.ShapeDtypeStruct((bsz, seq, w), dt) for w, dt in zip(out_widths, out_dtypes)],
        compiler_params=_cparams(("parallel", "parallel")),
        name="in_proj1",
    )(x, *weights, c64, s64)


CONV_PAD = 8
GDN_P1_CHUNKS = 8
GDN_P3_CHUNKS = 8


def _gdn_kernel(qraw_ref, kraw_ref, vraw_ref, wq_ref, wk_ref, wv_ref, gcol_ref, grow_ref, norm_ref, o_ref,
                xpad_s, q_s, k_s, v_s, bcb_s, gcb_s, w_s, u_s, att_s, mt_s, bt_s, st_s):
    seq = q_s.shape[0]
    nchunk = seq // GDN_CHUNK

    zeros = jnp.zeros((CONV_PAD, LANES), F32)
    xpad_s[0:CONV_PAD, :] = zeros
    xpad_s[CONV_PAD + seq:, :] = zeros

    def conv_silu(raw_ref, w_ref):
        xpad_s[CONV_PAD:CONV_PAD + seq, :] = raw_ref[0]
        acc = None
        for j in range(CONV_K):
            off = CONV_PAD - (CONV_K - 1) // 2 + j
            term = xpad_s[off:off + seq, :] * w_ref[j:j + 1, :]
            acc = term if acc is None else acc + term
        return _silu(acc)

    def l2norm(t):
        return t * lax.rsqrt(jnp.sum(t * t, axis=-1, keepdims=True) + L2_EPS)

    q_s[...] = l2norm(conv_silu(qraw_ref, wq_ref)) * (GDN_DK ** -0.5)
    k_s[...] = l2norm(conv_silu(kraw_ref, wk_ref))
    v_s[...] = conv_silu(vraw_ref, wv_ref)

    for d in range(2):
        bcb_s[d] = jnp.broadcast_to(gcol_ref[0, 0, :, d:d + 1], (seq, LANES))
        gcb_s[d] = jnp.broadcast_to(gcol_ref[0, 0, :, 2 + d:3 + d], (seq, LANES))

    ii = lax.broadcasted_iota(jnp.int32, (GDN_CHUNK, GDN_CHUNK), 0)
    jj = lax.broadcasted_iota(jnp.int32, (GDN_CHUNK, GDN_CHUNK), 1)
    causal = (ii >= jj, ii <= jj)
    strict = (ii > jj, ii < jj)

    def chunk_rows(c):
        base = pl.multiple_of(c * GDN_CHUNK, GDN_CHUNK)
        return base, pl.ds(base, GDN_CHUNK)

    def chunk_log_decay(d, base):
        return gcb_s[d, pl.ds(base + (GDN_CHUNK - 1 if d == 0 else 0), 1), :]

    def phase1(it, carry):
        chains = [(it * GDN_P1_CHUNKS + j, d) for j in range(GDN_P1_CHUNKS) for d in range(2)]
        rows = [chunk_rows(c) for c, _ in chains]
        kf = [k_s[r, :] for _, r in rows]
        kc = [t.astype(BF16) for t in kf]
        gcb = [gcb_s[d, r, :] for (_, d), (_, r) in zip(chains, rows)]
        bcb = [bcb_s[d, r, :] for (_, d), (_, r) in zip(chains, rows)]
        kb = [kf[i] * bcb[i] for i in range(len(chains))]
        kk = [_dot_nt(kb[i].astype(BF16), kc[i]) for i in range(len(chains))]
        decay = [jnp.exp(jnp.where(causal[d], gcb[i] - grow_ref[0, 0, d, pl.ds(c, 1), :], -jnp.inf))
                 for i, (c, d) in enumerate(chains)]
        a = [jnp.where(strict[d], kk[i] * decay[i], 0.0) for i, (_, d) in enumerate(chains)]
        n = [-t for t in a]
        xp = a
        for _ in range(6):
            xb = [t.astype(BF16) for t in xp]
            xp = [_dot(t, t) for t in xb]
            n = [n[i] + xp[i] + _dot(n[i].astype(BF16), xp[i].astype(BF16)) for i in range(len(chains))]
        nb = [t.astype(BF16) for t in n]
        wpre = [kb[i] * jnp.exp(gcb[i]) for i in range(len(chains))]
        upre = [v_s[r, :] * bcb[i] for i, (_, r) in enumerate(rows)]
        w = [(wpre[i] + _dot(nb[i], wpre[i].astype(BF16))).astype(BF16) for i in range(len(chains))]
        u = [upre[i] + _dot(nb[i], upre[i].astype(BF16)) for i in range(len(chains))]
        qkc = [_dot_nt(q_s[rows[2 * j][1], :].astype(BF16), kc[2 * j]) for j in range(GDN_P1_CHUNKS)]
        qk = [qkc[i // 2] for i in range(len(chains))]
        kd = [(kf[i] * jnp.exp(chunk_log_decay(d, rows[i][0]) - gcb[i])).astype(BF16) for i, (_, d) in enumerate(chains)]
        mt = [_dot_tn(w[i], kd[i]) for i in range(len(chains))]
        bt = [_dot_tn(u[i].astype(BF16), kd[i]) for i in range(len(chains))]
        for i, (c, d) in enumerate(chains):
            r = rows[i][1]
            w_s[d, r, :] = w[i]
            u_s[d, r, :] = u[i]
            att_s[d, r, :] = (qk[i] * decay[i]).astype(BF16)
            mt_s[d, c] = (-mt[i]).astype(BF16)
            bt_s[d, c] = bt[i]
        return carry

    lax.fori_loop(0, nchunk // GDN_P1_CHUNKS, phase1, 0)

    def phase2(i, carry):
        new = []
        for d in range(2):
            c = i if d == 0 else nchunk - 1 - i
            st = carry[d]
            st_s[d, c] = st.astype(BF16)
            gl = chunk_log_decay(d, pl.multiple_of(c * GDN_CHUNK, GDN_CHUNK))
            new.append(st * jnp.exp(gl) + _dot(st.astype(BF16), mt_s[d, c]) + bt_s[d, c])
        return tuple(new)

    zero_state = jnp.zeros((GDN_DV, GDN_DK), F32)
    lax.fori_loop(0, nchunk, phase2, (zero_state, zero_state))

    def phase3(it, carry):
        chains = [(it * GDN_P3_CHUNKS + j, d) for j in range(GDN_P3_CHUNKS) for d in range(2)]
        rows = [chunk_rows(c)[1] for c, _ in chains]
        qg = [(q_s[r, :] * jnp.exp(gcb_s[d, r, :])).astype(BF16) for (_, d), r in zip(chains, rows)]
        m = [_dot_nt(jnp.concatenate([w_s[d, rows[i], :], qg[i]], axis=0), st_s[d, c]) for i, (c, d) in enumerate(chains)]
        vnew = [(u_s[d, rows[i], :] - m[i][:GDN_CHUNK]).astype(BF16) for i, (_, d) in enumerate(chains)]
        od = [m[i][GDN_CHUNK:] + _dot(att_s[d, rows[i], :], vnew[i]) for i, (_, d) in enumerate(chains)]
        for j in range(GDN_P3_CHUNKS):
            o = od[2 * j] + od[2 * j + 1]
            o_ref[0, rows[2 * j], :] = (o * lax.rsqrt(jnp.mean(o * o, axis=-1, keepdims=True) + RMS_EPS)
                                        * norm_ref[...]).astype(o_ref.dtype)
        return carry

    lax.fori_loop(0, nchunk // GDN_P3_CHUNKS, phase3, 0)


def _gdn(qkvc, conv_w, gcol, grow, gdn_norm):
    bsz, seq, _ = qkvc.shape
    nh = GDN_HEADS

    def col(off):
        return pl.BlockSpec((1, seq, LANES), lambda b, h: (b, 0, off + h))

    def wcol(off):
        return pl.BlockSpec((CONV_K, LANES), lambda b, h: (0, off + h))

    nchunk = seq // GDN_CHUNK
    big = pltpu.VMEM((seq, LANES), F32)

    def per_dir(dtype):
        return pltpu.VMEM((2, seq, LANES), dtype)

    def per_chunk(dtype):
        return pltpu.VMEM((2, nchunk, GDN_DV, GDN_DK), dtype)

    return pl.pallas_call(
        _gdn_kernel,
        grid=(bsz, nh),
        in_specs=[col(0), col(nh), col(2 * nh), wcol(0), wcol(nh), wcol(2 * nh),
                  pl.BlockSpec((1, 1, seq, 4), lambda b, h: (b, h, 0, 0)),
                  pl.BlockSpec((1, 1, 2, nchunk, GDN_CHUNK), lambda b, h: (b, h, 0, 0, 0)),
                  _const_spec((1, GDN_DV))],
        out_specs=pl.BlockSpec((1, seq, GDN_DV), lambda b, h: (b, 0, h)),
        out_shape=jax.ShapeDtypeStruct((bsz, seq, nh * GDN_DV), BF16),
        scratch_shapes=[pltpu.VMEM((seq + 2 * CONV_PAD, LANES), F32), big, big, big,
                        per_dir(F32), per_dir(F32), per_dir(BF16), per_dir(F32), per_dir(BF16),
                        per_chunk(BF16), per_chunk(F32), per_chunk(BF16)],
        compiler_params=_cparams(("parallel", "parallel")),
        name="gated_deltanet",
    )(qkvc, qkvc, qkvc, conv_w, conv_w, conv_w, gcol, grow, gdn_norm.reshape(1, -1))


def _ret_tables():
    heads = np.arange(RET_HEADS, dtype=np.float64)
    log_gamma = np.log1p(-(2.0 ** (-RET_DECAY_BASE - heads)))[:, None]
    idx = np.arange(RET_CHUNK, dtype=np.float64)
    dmat = np.exp(log_gamma[:, :, None] * np.abs(idx[:, None] - idx[None, :])[None])
    xi = np.exp(log_gamma * (idx + 1.0))
    zeta = np.exp(log_gamma * (RET_CHUNK - 1.0 - idx))
    gch = np.exp(log_gamma * RET_CHUNK)

    def cb(t):
        return jnp.asarray(np.broadcast_to(t[:, :, None], t.shape + (LANES,)), F32)

    return jnp.asarray(dmat, F32), cb(xi), cb(zeta), cb(gch)


def _ret_kernel(q_ref, k_ref, v_ref, d_ref, xi_ref, zeta_ref, gch_ref, nw_ref, nb_ref, o_ref, kvf_s, kvb_s):
    seq = q_ref.shape[1]
    nchunk = seq // RET_CHUNK
    h = pl.program_id(1)
    xi, zeta, gch = xi_ref[0], zeta_ref[0], gch_ref[0]
    lane = lax.broadcasted_iota(jnp.int32, (RET_CHUNK, LANES), 1)
    mine = (lane // RET_DK) == (h % 2)

    def rows(c):
        return slice(c * RET_CHUNK, (c + 1) * RET_CHUNK)

    for c in range(nchunk):
        kc = k_ref[0, rows(c), :].astype(F32)
        vc = v_ref[0, rows(c), :]
        kvf_s[c] = _dot_tn((kc * zeta).astype(BF16), vc)
        kvb_s[c] = _dot_tn((kc * xi).astype(BF16), vc)
    state = jnp.zeros((LANES, RET_DV), F32)
    for c in range(nchunk):
        nxt = state * gch + kvf_s[c]
        kvf_s[c] = state
        state = nxt
    state = jnp.zeros((LANES, RET_DV), F32)
    for c in reversed(range(nchunk)):
        nxt = state * gch + kvb_s[c]
        kvb_s[c] = state
        state = nxt
    for c in range(nchunk):
        q = jnp.where(mine, q_ref[0, rows(c), :], jnp.zeros((RET_CHUNK, LANES), BF16))
        qf = q.astype(F32)
        s = _dot_nt(q, k_ref[0, rows(c), :]) * d_ref[0]
        o = _dot(s.astype(BF16), v_ref[0, rows(c), :])
        o = o + _dot((qf * xi).astype(BF16), kvf_s[c].astype(BF16))
        o = o + _dot((qf * zeta).astype(BF16), kvb_s[c].astype(BF16))
        mu = jnp.mean(o, axis=-1, keepdims=True)
        oc = o - mu
        var = jnp.mean(oc * oc, axis=-1, keepdims=True)
        o_ref[0, rows(c), :] = (oc * lax.rsqrt(var + LN_EPS) * nw_ref[...] + nb_ref[...]).astype(o_ref.dtype)


def _retention(qr, kr, vr, norm_w, norm_b):
    bsz, seq, _ = vr.shape
    dmat, xi, zeta, gch = _ret_tables()
    pair = pl.BlockSpec((1, seq, LANES), lambda b, h: (b, 0, h // 2))
    head = pl.BlockSpec((1, seq, RET_DV), lambda b, h: (b, 0, h))

    def per_head(shape):
        return pl.BlockSpec((1,) + shape, lambda b, h: (h, 0, 0))

    vec = pl.BlockSpec((1, RET_DV), lambda b, h: (0, h))
    state = pltpu.VMEM((seq // RET_CHUNK, LANES, RET_DV), F32)
    return pl.pallas_call(
        _ret_kernel,
        grid=(bsz, RET_HEADS),
        in_specs=[pair, pair, head, per_head((RET_CHUNK, RET_CHUNK)), per_head((RET_CHUNK, LANES)),
                  per_head((RET_CHUNK, LANES)), per_head((1, LANES)), vec, vec],
        out_specs=head,
        out_shape=jax.ShapeDtypeStruct((bsz, seq, RET_HEADS * RET_DV), BF16),
        scratch_shapes=[state, state],
        compiler_params=_cparams(("parallel", "parallel")),
        name="retention",
    )(qr, kr, vr, dmat, xi, zeta, gch, norm_w.reshape(1, -1), norm_b.reshape(1, -1))


def _layer1(x, tabs, w_in, conv_w, a_log, dt_bias, gdn_norm, ret_norm_w, ret_norm_b, w_out, ln_w, ln_b):
    bsz, seq, _ = x.shape
    qkvc, gates, qr, kr, vr, gate = _in1(x, tabs, w_in, a_log, dt_bias)
    gsl = gates[:, :, :4 * GDN_HEADS].reshape(bsz, seq, 2, 2, GDN_HEADS)
    gcol = gsl.transpose(0, 4, 1, 2, 3).reshape(bsz, GDN_HEADS, seq, 4)
    grow = gsl[:, :, 1].transpose(0, 3, 2, 1).reshape(bsz, GDN_HEADS, 2, seq // GDN_CHUNK, GDN_CHUNK)
    o_c = _gdn(qkvc, conv_w, gcol, grow, gdn_norm)
    o_d = _retention(qr, kr, vr, ret_norm_w, ret_norm_b)
    return _out_proj(o_c, o_d, gate, x, w_out, ln_w, ln_b)


def kernel(x, positions, even_w_in, even_q_norm, even_w_uq, even_kv_norm, even_w_ukv, even_w_out, even_ln_w, even_ln_b, odd_w_in, odd_conv_w, odd_a_log, odd_dt_bias, odd_gdn_norm, odd_ret_norm_w, odd_ret_norm_b, odd_w_out, odd_ln_w, odd_ln_b):
    tabs = _rope_tables(positions)
    x = _layer0(x, tabs, even_w_in[0], even_q_norm[0], even_w_uq[0], even_kv_norm[0], even_w_ukv[0],
                even_w_out[0], even_ln_w[0], even_ln_b[0])
    x = _layer1(x, tabs, odd_w_in[0], odd_conv_w[0], odd_a_log[0], odd_dt_bias[0], odd_gdn_norm[0],
                odd_ret_norm_w[0], odd_ret_norm_b[0], odd_w_out[0], odd_ln_w[0], odd_ln_b[0])
    return x
```

```python
import functools

import numpy as np
import jax
import jax.numpy as jnp
from jax import lax
from jax.experimental import pallas as pl
from jax.experimental.pallas import tpu as pltpu

F32 = jnp.float32
BF16 = jnp.bfloat16

D_MODEL = 1024
DEPTH = 2
DEEPNORM_ALPHA = (2 * DEPTH) ** 0.25
ROPE_THETA = 10000.0
RMS_EPS = 1e-6
LN_EPS = 1e-5
L2_EPS = 1e-6

MLA_HEADS = 8
MLA_NOPE = 128
MLA_ROPE = 64
MLA_V = 128
Q_LORA = 768
KV_LORA = 256

DIL_HEADS = 8
DIL_HD = 128
DILATED_GROUPS = ((128, 1), (512, 4), (2048, 16))

GDN_HEADS = 8
GDN_DK = 128
GDN_DV = 128
CONV_K = 5
GDN_CHUNK = 128

RET_HEADS = 8
RET_DK = 64
RET_DV = 128
RET_DECAY_BASE = 5.0
RET_CHUNK = 256

LANES = 128
VMEM_LIMIT = 56 * 1024 * 1024


def _cparams(sem):
    return pltpu.CompilerParams(dimension_semantics=sem, vmem_limit_bytes=VMEM_LIMIT)


def _dot(a, b):
    return jnp.dot(a, b, preferred_element_type=F32)


def _dot_nt(a, b):
    return lax.dot_general(a, b, (((1,), (1,)), ((), ())), preferred_element_type=F32)


def _dot_tn(a, b):
    return lax.dot_general(a, b, (((0,), (0,)), ((), ())), preferred_element_type=F32)


def _const_spec(shape):
    nd = len(shape)
    return pl.BlockSpec(shape, lambda *_: (0,) * nd, pipeline_mode=pl.Buffered(1))


def _rope_pairs64(x, cos, sin_signed):
    lane = lax.broadcasted_iota(jnp.int32, x.shape, 1)
    swapped = jnp.where((lane % 64) < 32, pltpu.roll(x, 96, 1), pltpu.roll(x, 32, 1))
    return x * cos + swapped * sin_signed


def _rope_full128(x, cos, sin_signed):
    return x * cos + pltpu.roll(x, 64, 1) * sin_signed


def _silu(x):
    return x * jax.nn.sigmoid(x)


def _rope_tables_kernel(pos_ref, inv_ref, sgn_ref, c64_ref, s64_ref, c128_ref, s128_ref):
    ang = pos_ref[0] * inv_ref[...]
    cos, sin = jnp.cos(ang), jnp.sin(ang)
    c32, s32 = cos[:, :32], sin[:, :32]
    c64, s64 = cos[:, 32:96], sin[:, 32:96]
    c64_ref[0] = jnp.concatenate([c32] * 4, axis=1)
    s64_ref[0] = jnp.concatenate([s32] * 4, axis=1) * sgn_ref[0:1, :]
    c128_ref[0] = jnp.concatenate([c64] * 2, axis=1)
    s128_ref[0] = jnp.concatenate([s64] * 2, axis=1) * sgn_ref[1:2, :]


def _rope_tables(positions):
    bsz, seq = positions.shape
    ts = 512
    inv32 = ROPE_THETA ** (-jnp.arange(0, 64, 2, dtype=F32) / 64)
    inv64 = ROPE_THETA ** (-jnp.arange(0, 128, 2, dtype=F32) / 128)
    inv = jnp.concatenate([inv32, inv64, jnp.zeros((32,), F32)])[None, :]
    sgn = jnp.asarray(np.stack([np.tile(np.repeat([-1.0, 1.0], 32), 2), np.repeat([-1.0, 1.0], 64)]), F32)
    pos = positions.astype(F32)[..., None]
    tab = jax.ShapeDtypeStruct((bsz, seq, LANES), F32)
    tspec = pl.BlockSpec((1, ts, LANES), lambda b, s: (b, s, 0))
    return pl.pallas_call(
        _rope_tables_kernel,
        grid=(bsz, seq // ts),
        in_specs=[pl.BlockSpec((1, ts, 1), lambda b, s: (b, s, 0)), _const_spec((1, LANES)), _const_spec((2, LANES))],
        out_specs=[tspec] * 4,
        out_shape=[tab] * 4,
        compiler_params=_cparams(("parallel", "parallel")),
        name="rope_tables",
    )(pos, inv, sgn)


def _rms(x, w):
    return x * lax.rsqrt(jnp.mean(x * x, axis=-1, keepdims=True) + RMS_EPS) * w


def _in0_kernel(x_ref, wq_ref, wkv_ref, wkr_ref, wb_ref, wz_ref, wuq_ref, wukv_ref, qnorm_ref, kvnorm_ref,
                c64_ref, s64_ref, c128_ref, s128_ref,
                qn_ref, qr_ref, kv_ref, kr_ref, qb_ref, kb_ref, vb_ref, g_ref):
    x = x_ref[0].astype(BF16)
    c64, s64, c128, s128 = c64_ref[0], s64_ref[0], c128_ref[0], s128_ref[0]
    cq = _rms(_dot(x, wq_ref[...]), qnorm_ref[...])
    qa = _dot(cq.astype(BF16), wuq_ref[...]) * ((MLA_NOPE + MLA_ROPE) ** -0.5 * LOG2E)
    nope = MLA_HEADS * MLA_NOPE
    qn_ref[0] = qa[:, :nope].astype(BF16)
    for j in range(MLA_HEADS * MLA_ROPE // LANES):
        blk = qa[:, nope + j * LANES:nope + (j + 1) * LANES]
        qr_ref[0, :, j * LANES:(j + 1) * LANES] = _rope_pairs64(blk, c64, s64).astype(BF16)
    ckv = _rms(_dot(x, wkv_ref[...]), kvnorm_ref[...])
    kv_ref[0] = _dot(ckv.astype(BF16), wukv_ref[...]).astype(BF16)
    kr_ref[0] = _rope_pairs64(_dot(x, wkr_ref[...]), c64, s64).astype(BF16)
    hb = _dot(x, wb_ref[...])
    width = DIL_HEADS * DIL_HD
    for h in range(DIL_HEADS):
        sl = slice(h * DIL_HD, (h + 1) * DIL_HD)
        qb_ref[0, :, sl] = _rope_full128(hb[:, sl], c128, s128) * (DIL_HD ** -0.5 * LOG2E)
        kb_ref[0, :, sl] = _rope_full128(hb[:, width + h * DIL_HD:width + (h + 1) * DIL_HD], c128, s128)
    vb_ref[0] = hb[:, 2 * width:]
    g_ref[0] = _silu(_dot(x, wz_ref[...])).astype(BF16)


def _in0(x, tabs, w_in, q_norm, w_uq, kv_norm, w_ukv, tm=512):
    bsz, seq, _ = x.shape
    c64, s64, c128, s128 = tabs
    o = np.cumsum((0, Q_LORA, KV_LORA, MLA_ROPE, 1024, 1024, 1024, 2048))
    wq = w_in[:, o[0]:o[1]].astype(BF16)
    wkv = w_in[:, o[1]:o[2]].astype(BF16)
    wkr = jnp.tile(w_in[:, o[2]:o[3]], (1, 2)).astype(BF16)
    wb = w_in[:, o[3]:o[6]].astype(BF16)
    wz = w_in[:, o[6]:o[7]].astype(BF16)
    wuq3 = w_uq.reshape(Q_LORA, MLA_HEADS, MLA_NOPE + MLA_ROPE)
    wuq = jnp.concatenate([wuq3[:, :, :MLA_NOPE].reshape(Q_LORA, -1), wuq3[:, :, MLA_NOPE:].reshape(Q_LORA, -1)],
                          axis=1).astype(BF16)
    wukv = w_ukv.astype(BF16)
    weights = (wq, wkv, wkr, wb, wz, wuq, wukv, q_norm.reshape(1, -1), kv_norm.reshape(1, -1))

    def rows(width):
        return pl.BlockSpec((1, tm, width), lambda b, s: (b, s, 0))

    out_widths = (1024, 512, 2048, LANES, 1024, 1024, 1024, 2048)
    out_dtypes = (BF16, BF16, BF16, BF16, F32, F32, F32, BF16)
    return pl.pallas_call(
        _in0_kernel,
        grid=(bsz, seq // tm),
        in_specs=[rows(D_MODEL)] + [_const_spec(w.shape) for w in weights] + [rows(LANES)] * 4,
        out_specs=[rows(w) for w in out_widths],
        out_shape=[jax.ShapeDtypeStruct((bsz, seq, w), dt) for w, dt in zip(out_widths, out_dtypes)],
        compiler_params=_cparams(("parallel", "parallel")),
        name="in_proj0",
    )(x, *weights, c64, s64, c128, s128)


MLA_SUB = 256
MLA_ONES_ROWS = 16
MLA_AHEAD = 5
LOG2E = 1.4426950408889634


def _mla_kernel(qn_ref, qr_ref, kn_ref, kr_ref, v_ref, o_ref, k_s, vt_s):
    h = pl.program_id(1)

    @pl.when(pl.program_id(2) == 0)
    def _():
        k_s[:, :MLA_NOPE] = kn_ref[0]
        k_s[:, MLA_NOPE:] = kr_ref[0]
        vt_s[:MLA_V, :] = v_ref[0].astype(F32).T.astype(BF16)
        vt_s[MLA_V:, :] = jnp.ones((MLA_ONES_ROWS, vt_s.shape[1]), BF16)

    lane = lax.broadcasted_iota(jnp.int32, (MLA_SUB, LANES), 1)
    mine = (lane // MLA_ROPE) == (h % 2)
    nsub = qn_ref.shape[1] // MLA_SUB

    def scores(i):
        rows = slice(i * MLA_SUB, (i + 1) * MLA_SUB)
        qr = jnp.where(mine, qr_ref[0, rows, :], jnp.zeros((MLA_SUB, LANES), BF16))
        return _dot_nt(k_s[...], jnp.concatenate([qn_ref[0, rows, :], qr], axis=1))

    ahead = [scores(i) for i in range(min(MLA_AHEAD, nsub))]
    for i in range(nsub):
        s = ahead.pop(0)
        if i + MLA_AHEAD < nsub:
            ahead.append(scores(i + MLA_AHEAD))
        m = jnp.max(s, axis=0, keepdims=True)
        p = jnp.exp2(s - m).astype(BF16)
        ot = _dot(vt_s[...], p)
        o = ot[:MLA_V] / ot[MLA_V:MLA_V + 1]
        o_ref[0, i * MLA_SUB:(i + 1) * MLA_SUB, :] = o.T.astype(o_ref.dtype)


def _mla(qn, qr, kv, kr, tq=2048):
    bsz, seq, _ = qn.shape
    return pl.pallas_call(
        _mla_kernel,
        grid=(bsz, MLA_HEADS, seq // tq),
        in_specs=[
            pl.BlockSpec((1, tq, LANES), lambda b, h, i: (b, i, h)),
            pl.BlockSpec((1, tq, LANES), lambda b, h, i: (b, i, h // 2)),
            pl.BlockSpec((1, seq, LANES), lambda b, h, i: (b, 0, 2 * h)),
            pl.BlockSpec((1, seq, LANES), lambda b, h, i: (b, 0, 0)),
            pl.BlockSpec((1, seq, LANES), lambda b, h, i: (b, 0, 2 * h + 1)),
        ],
        out_specs=pl.BlockSpec((1, tq, LANES), lambda b, h, i: (b, i, h)),
        out_shape=jax.ShapeDtypeStruct((bsz, seq, MLA_HEADS * MLA_V), BF16),
        scratch_shapes=[pltpu.VMEM((seq, MLA_NOPE + LANES), BF16), pltpu.VMEM((MLA_V + MLA_ONES_ROWS, seq), BF16)],
        compiler_params=_cparams(("parallel", "parallel", "arbitrary")),
        name="mla_attention",
    )(qn, qr, kv, kr, kv)


DIL_TQ = 128
DIL_RADIUS = 64
DIL_TILES_PER_STEP = 8


def _dil_bias_table():
    i = np.arange(DIL_TQ)[:, None]
    j = np.arange(2 * DIL_TQ)[None, :]
    tabs = [np.where(np.abs(j - off - i) <= DIL_RADIUS, 0.0, -np.inf) for off in (0, DIL_RADIUS, 2 * DIL_RADIUS)]
    return jnp.asarray(np.stack(tabs), F32)


def _dil_kernel(q_ref, k_ref, v_ref, bias_ref, o_ref, qd_s, kd_s, vd_s, m_s, l_s, acc_s):
    seq = q_ref.shape[1]
    vd_s[:, :, DIL_HD:] = jnp.ones((len(DILATED_GROUPS), seq, DIL_HD), BF16)

    def rows_of(start, size, dil):
        return pl.ds(start, size) if dil == 1 else pl.ds(start, size, stride=dil)

    for g, (_, dil) in enumerate(DILATED_GROUPS):
        length = seq // dil
        for src, dst in ((q_ref, qd_s), (k_ref, kd_s), (v_ref, vd_s)):
            for r in range(dil):
                dst[g, r * length:(r + 1) * length, :DIL_HD] = src[0, rows_of(r, length, dil), :].astype(BF16)

    def tiles_step(g, dil, which, first):
        length = seq // dil
        tiles = length // DIL_TQ
        kw = min(length, DIL_TQ + 2 * DIL_RADIUS)
        n = len(which)
        ws = [jnp.clip(t * DIL_TQ - DIL_RADIUS, 0, length - kw) for _, t in which]
        qrows = [pl.ds(pl.multiple_of(r * length + t * DIL_TQ, DIL_TQ), DIL_TQ) for r, t in which]
        krows = [pl.ds(pl.multiple_of(r * length + ws[i], DIL_RADIUS), kw) for i, (r, _) in enumerate(which)]
        out_rows = [rows_of(r + t * (DIL_TQ * dil), DIL_TQ, dil) for r, t in which]
        if tiles == 1:
            bias = [bias_ref[0, :, :kw] for _ in which]
        else:
            bias = [bias_ref[jnp.where(t == 0, 0, jnp.where(t == tiles - 1, 2, 1))] for _, t in which]
        s = [_dot_nt(qd_s[g, qrows[i], :], kd_s[g, krows[i], :]) + bias[i] for i in range(n)]
        mk = [jnp.max(t, axis=-1, keepdims=True) for t in s]
        m = [jnp.broadcast_to(t, (DIL_TQ, DIL_HD)) for t in mk]
        p = [jnp.exp2(s[i] - mk[i]).astype(BF16) for i in range(n)]
        ol = [_dot(p[i], vd_s[g, krows[i], :]) for i in range(n)]
        o = [t[:, :DIL_HD] for t in ol]
        l = [t[:, DIL_HD:] for t in ol]
        for i in range(n):
            if first:
                m_s[out_rows[i], :] = m[i]
                l_s[out_rows[i], :] = l[i]
                acc_s[out_rows[i], :] = o[i]
            else:
                m_old = m_s[out_rows[i], :]
                m_new = jnp.maximum(m_old, m[i])
                a_old = jnp.exp2(m_old - m_new)
                a_new = jnp.exp2(m[i] - m_new)
                m_s[out_rows[i], :] = m_new
                l_s[out_rows[i], :] = a_old * l_s[out_rows[i], :] + a_new * l[i]
                acc_s[out_rows[i], :] = a_old * acc_s[out_rows[i], :] + a_new * o[i]

    for g, (window, dil) in enumerate(DILATED_GROUPS):
        assert window // (2 * dil) == DIL_RADIUS
        tiles = seq // dil // DIL_TQ

        def body(it, carry, g=g, dil=dil, tiles=tiles):
            idx = [it * DIL_TILES_PER_STEP + j for j in range(DIL_TILES_PER_STEP)]
            tiles_step(g, dil, [(i // tiles, i % tiles) for i in idx], g == 0)
            return carry

        lax.fori_loop(0, dil * tiles // DIL_TILES_PER_STEP, body, 0)
    o_ref[0] = (acc_s[...] / l_s[...]).astype(o_ref.dtype)


def _dilated(qb, kb, vb):
    bsz, seq, _ = qb.shape
    spec = pl.BlockSpec((1, seq, DIL_HD), lambda b, h: (b, 0, h))
    bias = _dil_bias_table()
    grouped = pltpu.VMEM((len(DILATED_GROUPS), seq, DIL_HD), BF16)
    stat = pltpu.VMEM((seq, DIL_HD), F32)
    return pl.pallas_call(
        _dil_kernel,
        grid=(bsz, DIL_HEADS),
        in_specs=[spec] * 3 + [_const_spec(bias.shape)],
        out_specs=spec,
        out_shape=jax.ShapeDtypeStruct((bsz, seq, DIL_HEADS * DIL_HD), BF16),
        scratch_shapes=[grouped, grouped, pltpu.VMEM((len(DILATED_GROUPS), seq, 2 * DIL_HD), BF16), stat, stat, stat],
        compiler_params=_cparams(("parallel", "parallel")),
        name="dilated_attention",
    )(qb, kb, vb, bias)


def _out_kernel(o1_ref, o2_ref, g_ref, x_ref, w_ref, lnw_ref, lnb_ref, out_ref):
    half = o1_ref.shape[-1]
    g = g_ref[0].astype(F32)
    y1 = (o1_ref[0].astype(F32) * g[:, :half]).astype(BF16)
    y2 = (o2_ref[0].astype(F32) * g[:, half:]).astype(BF16)
    r = DEEPNORM_ALPHA * x_ref[0] + _dot(y1, w_ref[:half, :]) + _dot(y2, w_ref[half:, :])
    mu = jnp.mean(r, axis=-1, keepdims=True)
    rc = r - mu
    var = jnp.mean(rc * rc, axis=-1, keepdims=True)
    out_ref[0] = rc * lax.rsqrt(var + LN_EPS) * lnw_ref[...] + lnb_ref[...]


def _out_proj(o1, o2, gate, x, w_out, ln_w, ln_b, tm=512):
    bsz, seq, _ = x.shape
    half = o1.shape[-1]

    def rows(width):
        return pl.BlockSpec((1, tm, width), lambda b, s: (b, s, 0))

    w = w_out.astype(BF16)
    return pl.pallas_call(
        _out_kernel,
        grid=(bsz, seq // tm),
        in_specs=[rows(half), rows(half), rows(2 * half), rows(D_MODEL), _const_spec(w.shape),
                  _const_spec((1, D_MODEL)), _const_spec((1, D_MODEL))],
        out_specs=rows(D_MODEL),
        out_shape=jax.ShapeDtypeStruct((bsz, seq, D_MODEL), F32),
        compiler_params=_cparams(("parallel", "parallel")),
        name="out_proj",
    )(o1, o2, gate, x, w, ln_w.reshape(1, -1), ln_b.reshape(1, -1))


def _layer0(x, tabs, w_in, q_norm, w_uq, kv_norm, w_ukv, w_out, ln_w, ln_b):
    qn, qr, kv, kr, qb, kb, vb, gate = _in0(x, tabs, w_in, q_norm, w_uq, kv_norm, w_ukv)
    o_a = _mla(qn, qr, kv, kr)
    o_b = _dilated(qb, kb, vb)
    return _out_proj(o_a, o_b, gate, x, w_out, ln_w, ln_b)


GATE_BETA = 0
GATE_GC = 2 * GDN_HEADS


def _softplus(x):
    return jnp.maximum(x, 0.0) + jnp.log1p(jnp.exp(-jnp.abs(x)))


def _in1_kernel(x_ref, wc_ref, wg_ref, wqk_ref, wv_ref, wz_ref, gp_ref, c64_ref, s64_ref,
                qkvc_ref, gate_ref, qr_ref, kr_ref, vr_ref, g_ref):
    x = x_ref[0].astype(BF16)
    tm = x.shape[0]
    c64, s64 = c64_ref[0], s64_ref[0]
    qkvc_ref[0] = _dot(x, wc_ref[...])
    hg = _dot(x, wg_ref[...])
    beta = jax.nn.sigmoid(hg)
    g = -jnp.exp(gp_ref[0:1, :]) * _softplus(hg + gp_ref[1:2, :])
    pos = lax.broadcasted_iota(jnp.int32, hg.shape, 0) % GDN_CHUNK
    pre, suf = g, g
    step = 1
    while step < GDN_CHUNK:
        pre = pre + jnp.where(pos >= step, pltpu.roll(pre, step, 0), 0.0)
        suf = suf + jnp.where(pos < GDN_CHUNK - step, pltpu.roll(suf, tm - step, 0), 0.0)
        step *= 2
    lane = lax.broadcasted_iota(jnp.int32, hg.shape, 1)
    gc = jnp.where(lane < GATE_GC + GDN_HEADS, pre, suf)
    gate_ref[0] = jnp.where(lane < GATE_GC, beta, gc)
    hqk = _dot(x, wqk_ref[...])
    width = RET_HEADS * RET_DK
    for j in range(width // LANES):
        qr_ref[0, :, j * LANES:(j + 1) * LANES] = _rope_pairs64(hqk[:, j * LANES:(j + 1) * LANES], c64, s64).astype(BF16)
        kblk = hqk[:, width + j * LANES:width + (j + 1) * LANES]
        kr_ref[0, :, j * LANES:(j + 1) * LANES] = (_rope_pairs64(kblk, c64, s64) * (RET_DK ** -0.5)).astype(BF16)
    vr_ref[0] = _dot(x, wv_ref[...]).astype(BF16)
    g_ref[0] = _silu(_dot(x, wz_ref[...])).astype(BF16)


def _in1(x, tabs, w_in, a_log, dt_bias, tm=512):
    bsz, seq, _ = x.shape
    c64, s64, _, _ = tabs
    conv_ch = 2 * GDN_HEADS * GDN_DK + GDN_HEADS * GDN_DV
    o = np.cumsum((0, conv_ch, 2 * GDN_HEADS, 2 * GDN_HEADS, 512, 512, 1024, 2048))
    wc = w_in[:, o[0]:o[1]].astype(BF16)
    wg = jnp.pad(w_in[:, o[1]:o[3]], ((0, 0), (0, LANES - 4 * GDN_HEADS))).astype(BF16)
    wqk = w_in[:, o[3]:o[5]].astype(BF16)
    wv = w_in[:, o[5]:o[6]].astype(BF16)
    wz = w_in[:, o[6]:o[7]].astype(BF16)
    pad = (GATE_GC, LANES - GATE_GC - 2 * GDN_HEADS)
    gp = jnp.stack([jnp.pad(a_log.reshape(-1), pad), jnp.pad(dt_bias.reshape(-1), pad)]).astype(F32)
    weights = (wc, wg, wqk, wv, wz, gp)

    def rows(width):
        return pl.BlockSpec((1, tm, width), lambda b, s: (b, s, 0))

    out_widths = (conv_ch, LANES, 512, 512, 1024, 2048)
    out_dtypes = (F32, F32, BF16, BF16, BF16, BF16)
    return pl.pallas_call(
        _in1_kernel,
        grid=(bsz, seq // tm),
        in_specs=[rows(D_MODEL)] + [_const_spec(w.shape) for w in weights] + [rows(LANES)] * 2,
        out_specs=[rows(w) for w in out_widths],
        out_shape=[jax.ShapeDtypeStruct((bsz, seq, w), dt) for w, dt in zip(out_widths, out_dtypes)],
        compiler_params=_cparams(("parallel", "parallel")),
        name="in_proj1",
    )(x, *weights, c64, s64)


CONV_PAD = 8
GDN_P1_CHUNKS = 8
GDN_P3_CHUNKS = 8
GDN_LEVELS = 7


def _gdn_level_masks():
    i = np.arange(GDN_CHUNK)[:, None]
    j = np.arange(GDN_CHUNK)[None, :]
    masks = [((i >> (k + 1)) == (j >> (k + 1))) & ((i >> k) != (j >> k)) for k in range(GDN_LEVELS)]
    return jnp.asarray(np.stack(masks), F32)


def _gdn_kernel(qraw_ref, kraw_ref, vraw_ref, wq_ref, wk_ref, wv_ref, gcol_ref, grow_ref, norm_ref, lvl_ref, o_ref,
                xpad_s, q_s, k_s, v_s, bcb_s, gcb_s, w_s, u_s, att_s, mt_s, bt_s, st_s):
    seq = q_s.shape[0]
    nchunk = seq // GDN_CHUNK

    zeros = jnp.zeros((CONV_PAD, LANES), F32)
    xpad_s[0:CONV_PAD, :] = zeros
    xpad_s[CONV_PAD + seq:, :] = zeros

    def conv_silu(raw_ref, w_ref):
        xpad_s[CONV_PAD:CONV_PAD + seq, :] = raw_ref[0]
        acc = None
        for j in range(CONV_K):
            off = CONV_PAD - (CONV_K - 1) // 2 + j
            term = xpad_s[off:off + seq, :] * w_ref[j:j + 1, :]
            acc = term if acc is None else acc + term
        return _silu(acc)

    def l2norm(t):
        return t * lax.rsqrt(jnp.sum(t * t, axis=-1, keepdims=True) + L2_EPS)

    q_s[...] = l2norm(conv_silu(qraw_ref, wq_ref)) * (GDN_DK ** -0.5)
    k_s[...] = l2norm(conv_silu(kraw_ref, wk_ref))
    v_s[...] = conv_silu(vraw_ref, wv_ref)

    for d in range(2):
        bcb_s[d] = jnp.broadcast_to(gcol_ref[0, 0, :, d:d + 1], (seq, LANES))
        gcb_s[d] = jnp.broadcast_to(gcol_ref[0, 0, :, 2 + d:3 + d], (seq, LANES))

    ii = lax.broadcasted_iota(jnp.int32, (GDN_CHUNK, GDN_CHUNK), 0)
    jj = lax.broadcasted_iota(jnp.int32, (GDN_CHUNK, GDN_CHUNK), 1)
    causal = (ii >= jj, ii <= jj)
    strict = (ii > jj, ii < jj)

    def chunk_rows(c):
        base = pl.multiple_of(c * GDN_CHUNK, GDN_CHUNK)
        return base, pl.ds(base, GDN_CHUNK)

    def chunk_log_decay(d, base):
        return gcb_s[d, pl.ds(base + (GDN_CHUNK - 1 if d == 0 else 0), 1), :]

    def phase1(it, carry):
        chains = [(it * GDN_P1_CHUNKS + j, d) for j in range(GDN_P1_CHUNKS) for d in range(2)]
        rows = [chunk_rows(c) for c, _ in chains]
        kf = [k_s[r, :] for _, r in rows]
        kc = [t.astype(BF16) for t in kf]
        gcb = [gcb_s[d, r, :] for (_, d), (_, r) in zip(chains, rows)]
        bcb = [bcb_s[d, r, :] for (_, d), (_, r) in zip(chains, rows)]
        kb = [kf[i] * bcb[i] for i in range(len(chains))]
        kk = [_dot_nt(kb[i].astype(BF16), kc[i]) for i in range(len(chains))]
        decay = [jnp.exp(jnp.where(causal[d], gcb[i] - grow_ref[0, 0, d, pl.ds(c, 1), :], -jnp.inf))
                 for i, (c, d) in enumerate(chains)]
        a = [jnp.where(strict[d], kk[i] * decay[i], 0.0) for i, (_, d) in enumerate(chains)]
        n = [-(t * lvl_ref[0]) for t in a]
        for lvl in range(1, GDN_LEVELS):
            ak = [t * lvl_ref[lvl] for t in a]
            y = [ak[i] + _dot(ak[i].astype(BF16), n[i].astype(BF16)) for i in range(len(chains))]
            n = [n[i] - y[i] - _dot(n[i].astype(BF16), y[i].astype(BF16)) for i in range(len(chains))]
        nb = [t.astype(BF16) for t in n]
        wpre = [kb[i] * jnp.exp(gcb[i]) for i in range(len(chains))]
        upre = [v_s[r, :] * bcb[i] for i, (_, r) in enumerate(rows)]
        w = [(wpre[i] + _dot(nb[i], wpre[i].astype(BF16))).astype(BF16) for i in range(len(chains))]
        u = [upre[i] + _dot(nb[i], upre[i].astype(BF16)) for i in range(len(chains))]
        qkc = [_dot_nt(q_s[rows[2 * j][1], :].astype(BF16), kc[2 * j]) for j in range(GDN_P1_CHUNKS)]
        qk = [qkc[i // 2] for i in range(len(chains))]
        kd = [(kf[i] * jnp.exp(chunk_log_decay(d, rows[i][0]) - gcb[i])).astype(BF16) for i, (_, d) in enumerate(chains)]
        mt = [_dot_tn(w[i], kd[i]) for i in range(len(chains))]
        bt = [_dot_tn(u[i].astype(BF16), kd[i]) for i in range(len(chains))]
        for i, (c, d) in enumerate(chains):
            r = rows[i][1]
            w_s[d, r, :] = w[i]
            u_s[d, r, :] = u[i]
            att_s[d, r, :] = (qk[i] * decay[i]).astype(BF16)
            mt_s[d, c] = (-mt[i]).astype(BF16)
            bt_s[d, c] = bt[i]
        return carry

    lax.fori_loop(0, nchunk // GDN_P1_CHUNKS, phase1, 0)

    def phase2(i, carry):
        new = []
        for d in range(2):
            c = i if d == 0 else nchunk - 1 - i
            st = carry[d]
            st_s[d, c] = st.astype(BF16)
            gl = chunk_log_decay(d, pl.multiple_of(c * GDN_CHUNK, GDN_CHUNK))
            new.append(st * jnp.exp(gl) + _dot(st.astype(BF16), mt_s[d, c]) + bt_s[d, c])
        return tuple(new)

    zero_state = jnp.zeros((GDN_DV, GDN_DK), F32)
    lax.fori_loop(0, nchunk, phase2, (zero_state, zero_state))

    def phase3(it, carry):
        chains = [(it * GDN_P3_CHUNKS + j, d) for j in range(GDN_P3_CHUNKS) for d in range(2)]
        rows = [chunk_rows(c)[1] for c, _ in chains]
        qg = [(q_s[r, :] * jnp.exp(gcb_s[d, r, :])).astype(BF16) for (_, d), r in zip(chains, rows)]
        m = [_dot_nt(jnp.concatenate([w_s[d, rows[i], :], qg[i]], axis=0), st_s[d, c]) for i, (c, d) in enumerate(chains)]
        vnew = [(u_s[d, rows[i], :] - m[i][:GDN_CHUNK]).astype(BF16) for i, (_, d) in enumerate(chains)]
        od = [m[i][GDN_CHUNK:] + _dot(att_s[d, rows[i], :], vnew[i]) for i, (_, d) in enumerate(chains)]
        for j in range(GDN_P3_CHUNKS):
            o = od[2 * j] + od[2 * j + 1]
            o_ref[0, rows[2 * j], :] = (o * lax.rsqrt(jnp.mean(o * o, axis=-1, keepdims=True) + RMS_EPS)
                                        * norm_ref[...]).astype(o_ref.dtype)
        return carry

    lax.fori_loop(0, nchunk // GDN_P3_CHUNKS, phase3, 0)


def _gdn(qkvc, conv_w, gcol, grow, gdn_norm):
    bsz, seq, _ = qkvc.shape
    nh = GDN_HEADS

    def col(off):
        return pl.BlockSpec((1, seq, LANES), lambda b, h: (b, 0, off + h))

    def wcol(off):
        return pl.BlockSpec((CONV_K, LANES), lambda b, h: (0, off + h))

    nchunk = seq // GDN_CHUNK
    big = pltpu.VMEM((seq, LANES), F32)

    def per_dir(dtype):
        return pltpu.VMEM((2, seq, LANES), dtype)

    def per_chunk(dtype):
        return pltpu.VMEM((2, nchunk, GDN_DV, GDN_DK), dtype)

    return pl.pallas_call(
        _gdn_kernel,
        grid=(bsz, nh),
        in_specs=[col(0), col(nh), col(2 * nh), wcol(0), wcol(nh), wcol(2 * nh),
                  pl.BlockSpec((1, 1, seq, 4), lambda b, h: (b, h, 0, 0)),
                  pl.BlockSpec((1, 1, 2, nchunk, GDN_CHUNK), lambda b, h: (b, h, 0, 0, 0)),
                  _const_spec((1, GDN_DV)), _const_spec((GDN_LEVELS, GDN_CHUNK, GDN_CHUNK))],
        out_specs=pl.BlockSpec((1, seq, GDN_DV), lambda b, h: (b, 0, h)),
        out_shape=jax.ShapeDtypeStruct((bsz, seq, nh * GDN_DV), BF16),
        scratch_shapes=[pltpu.VMEM((seq + 2 * CONV_PAD, LANES), F32), big, big, big,
                        per_dir(F32), per_dir(F32), per_dir(BF16), per_dir(F32), per_dir(BF16),
                        per_chunk(BF16), per_chunk(F32), per_chunk(BF16)],
        compiler_params=_cparams(("parallel", "parallel")),
        name="gated_deltanet",
    )(qkvc, qkvc, qkvc, conv_w, conv_w, conv_w, gcol, grow, gdn_norm.reshape(1, -1), _gdn_level_masks())


def _ret_tables():
    heads = np.arange(RET_HEADS, dtype=np.float64)
    log_gamma = np.log1p(-(2.0 ** (-RET_DECAY_BASE - heads)))[:, None]
    idx = np.arange(RET_CHUNK, dtype=np.float64)
    dmat = np.exp(log_gamma[:, :, None] * np.abs(idx[:, None] - idx[None, :])[None])
    xi = np.exp(log_gamma * (idx + 1.0))
    zeta = np.exp(log_gamma * (RET_CHUNK - 1.0 - idx))
    gch = np.exp(log_gamma * RET_CHUNK)

    def cb(t):
        return jnp.asarray(np.broadcast_to(t[:, :, None], t.shape + (LANES,)), F32)

    return jnp.asarray(dmat, F32), cb(xi), cb(zeta), cb(gch)


def _ret_kernel(q_ref, k_ref, v_ref, d_ref, xi_ref, zeta_ref, gch_ref, nw_ref, nb_ref, o_ref, kvf_s, kvb_s):
    seq = q_ref.shape[1]
    nchunk = seq // RET_CHUNK
    h = pl.program_id(1)
    xi, zeta, gch = xi_ref[0], zeta_ref[0], gch_ref[0]
    lane = lax.broadcasted_iota(jnp.int32, (RET_CHUNK, LANES), 1)
    mine = (lane // RET_DK) == (h % 2)

    def rows(c):
        return slice(c * RET_CHUNK, (c + 1) * RET_CHUNK)

    for c in range(nchunk):
        kc = k_ref[0, rows(c), :].astype(F32)
        vc = v_ref[0, rows(c), :]
        kvf_s[c] = _dot_tn((kc * zeta).astype(BF16), vc)
        kvb_s[c] = _dot_tn((kc * xi).astype(BF16), vc)
    state = jnp.zeros((LANES, RET_DV), F32)
    for c in range(nchunk):
        nxt = state * gch + kvf_s[c]
        kvf_s[c] = state
        state = nxt
    state = jnp.zeros((LANES, RET_DV), F32)
    for c in reversed(range(nchunk)):
        nxt = state * gch + kvb_s[c]
        kvb_s[c] = state
        state = nxt
    for c in range(nchunk):
        q = jnp.where(mine, q_ref[0, rows(c), :], jnp.zeros((RET_CHUNK, LANES), BF16))
        qf = q.astype(F32)
        s = _dot_nt(q, k_ref[0, rows(c), :]) * d_ref[0]
        o = _dot(s.astype(BF16), v_ref[0, rows(c), :])
        o = o + _dot((qf * xi).astype(BF16), kvf_s[c].astype(BF16))
        o = o + _dot((qf * zeta).astype(BF16), kvb_s[c].astype(BF16))
        mu = jnp.mean(o, axis=-1, keepdims=True)
        oc = o - mu
        var = jnp.mean(oc * oc, axis=-1, keepdims=True)
        o_ref[0, rows(c), :] = (oc * lax.rsqrt(var + LN_EPS) * nw_ref[...] + nb_ref[...]).astype(o_ref.dtype)


def _retention(qr, kr, vr, norm_w, norm_b):
    bsz, seq, _ = vr.shape
    dmat, xi, zeta, gch = _ret_tables()
    pair = pl.BlockSpec((1, seq, LANES), lambda b, h: (b, 0, h // 2))
    head = pl.BlockSpec((1, seq, RET_DV), lambda b, h: (b, 0, h))

    def per_head(shape):
        return pl.BlockSpec((1,) + shape, lambda b, h: (h, 0, 0))

    vec = pl.BlockSpec((1, RET_DV), lambda b, h: (0, h))
    state = pltpu.VMEM((seq // RET_CHUNK, LANES, RET_DV), F32)
    return pl.pallas_call(
        _ret_kernel,
        grid=(bsz, RET_HEADS),
        in_specs=[pair, pair, head, per_head((RET_CHUNK, RET_CHUNK)), per_head((RET_CHUNK, LANES)),
                  per_head((RET_CHUNK, LANES)), per_head((1, LANES)), vec, vec],
        out_specs=head,
        out_shape=jax.ShapeDtypeStruct((bsz, seq, RET_HEADS * RET_DV), BF16),
        scratch_shapes=[state, state],
        compiler_params=_cparams(("parallel", "parallel")),
        name="retention",
    )(qr, kr, vr, dmat, xi, zeta, gch, norm_w.reshape(1, -1), norm_b.reshape(1, -1))


def _layer1(x, tabs, w_in, conv_w, a_log, dt_bias, gdn_norm, ret_norm_w, ret_norm_b, w_out, ln_w, ln_b):
    bsz, seq, _ = x.shape
    qkvc, gates, qr, kr, vr, gate = _in1(x, tabs, w_in, a_log, dt_bias)
    gsl = gates[:, :, :4 * GDN_HEADS].reshape(bsz, seq, 2, 2, GDN_HEADS)
    gcol = gsl.transpose(0, 4, 1, 2, 3).reshape(bsz, GDN_HEADS, seq, 4)
    grow = gsl[:, :, 1].transpose(0, 3, 2, 1).reshape(bsz, GDN_HEADS, 2, seq // GDN_CHUNK, GDN_CHUNK)
    o_c = _gdn(qkvc, conv_w, gcol, grow, gdn_norm)
    o_d = _retention(qr, kr, vr, ret_norm_w, ret_norm_b)
    return _out_proj(o_c, o_d, gate, x, w_out, ln_w, ln_b)


def kernel(x, positions, even_w_in, even_q_norm, even_w_uq, even_kv_norm, even_w_ukv, even_w_out, even_ln_w, even_ln_b, odd_w_in, odd_conv_w, odd_a_log, odd_dt_bias, odd_gdn_norm, odd_ret_norm_w, odd_ret_norm_b, odd_w_out, odd_ln_w, odd_ln_b):
    tabs = _rope_tables(positions)
    x = _layer0(x, tabs, even_w_in[0], even_q_norm[0], even_w_uq[0], even_kv_norm[0], even_w_ukv[0],
                even_w_out[0], even_ln_w[0], even_ln_b[0])
    x = _layer1(x, tabs, odd_w_in[0], odd_conv_w[0], odd_a_log[0], odd_dt_bias[0], odd_gdn_norm[0],
                odd_ret_norm_w[0], odd_ret_norm_b[0], odd_w_out[0], odd_ln_w[0], odd_ln_b[0])
    return x
```

```python
import functools

import numpy as np
import jax
import jax.numpy as jnp
from jax import lax
from jax.experimental import pallas as pl
from jax.experimental.pallas import tpu as pltpu

F32 = jnp.float32
BF16 = jnp.bfloat16

D_MODEL = 1024
DEPTH = 2
DEEPNORM_ALPHA = (2 * DEPTH) ** 0.25
ROPE_THETA = 10000.0
RMS_EPS = 1e-6
LN_EPS = 1e-5
L2_EPS = 1e-6

MLA_HEADS = 8
MLA_NOPE = 128
MLA_ROPE = 64
MLA_V = 128
Q_LORA = 768
KV_LORA = 256

DIL_HEADS = 8
DIL_HD = 128
DILATED_GROUPS = ((128, 1), (512, 4), (2048, 16))

GDN_HEADS = 8
GDN_DK = 128
GDN_DV = 128
CONV_K = 5
GDN_CHUNK = 128

RET_HEADS = 8
RET_DK = 64
RET_DV = 128
RET_DECAY_BASE = 5.0
RET_CHUNK = 256

LANES = 128
VMEM_LIMIT = 56 * 1024 * 1024


def _cparams(sem):
    return pltpu.CompilerParams(dimension_semantics=sem, vmem_limit_bytes=VMEM_LIMIT)


def _dot(a, b):
    return jnp.dot(a, b, preferred_element_type=F32)


def _dot_nt(a, b):
    return lax.dot_general(a, b, (((1,), (1,)), ((), ())), preferred_element_type=F32)


def _dot_tn(a, b):
    return lax.dot_general(a, b, (((0,), (0,)), ((), ())), preferred_element_type=F32)


def _const_spec(shape):
    nd = len(shape)
    return pl.BlockSpec(shape, lambda *_: (0,) * nd, pipeline_mode=pl.Buffered(1))


def _rope_pairs64(x, cos, sin_signed):
    lane = lax.broadcasted_iota(jnp.int32, x.shape, 1)
    swapped = jnp.where((lane % 64) < 32, pltpu.roll(x, 96, 1), pltpu.roll(x, 32, 1))
    return x * cos + swapped * sin_signed


def _rope_full128(x, cos, sin_signed):
    return x * cos + pltpu.roll(x, 64, 1) * sin_signed


def _splice_columns(w, head_end, tail_start, gap):
    wb = w.astype(BF16)
    parts = [wb[:, :head_end], wb[:, tail_start:]]
    if gap:
        parts.insert(1, jnp.zeros((w.shape[0], gap), BF16))
    return jnp.concatenate(parts, axis=1)


def _silu(x):
    return x * jax.nn.sigmoid(x)


def _rope_tables_kernel(pos_ref, inv_ref, sgn_ref, c64_ref, s64_ref, c128_ref, s128_ref):
    ang = pos_ref[0] * inv_ref[...]
    cos, sin = jnp.cos(ang), jnp.sin(ang)
    c32, s32 = cos[:, :32], sin[:, :32]
    c64, s64 = cos[:, 32:96], sin[:, 32:96]
    c64_ref[0] = jnp.concatenate([c32] * 4, axis=1)
    s64_ref[0] = jnp.concatenate([s32] * 4, axis=1) * sgn_ref[0:1, :]
    c128_ref[0] = jnp.concatenate([c64] * 2, axis=1)
    s128_ref[0] = jnp.concatenate([s64] * 2, axis=1) * sgn_ref[1:2, :]


def _rope_tables(positions):
    bsz, seq = positions.shape
    ts = 512
    inv32 = ROPE_THETA ** (-jnp.arange(0, 64, 2, dtype=F32) / 64)
    inv64 = ROPE_THETA ** (-jnp.arange(0, 128, 2, dtype=F32) / 128)
    inv = jnp.concatenate([inv32, inv64, jnp.zeros((32,), F32)])[None, :]
    sgn = jnp.asarray(np.stack([np.tile(np.repeat([-1.0, 1.0], 32), 2), np.repeat([-1.0, 1.0], 64)]), F32)
    pos = positions.astype(F32)[..., None]
    tab = jax.ShapeDtypeStruct((bsz, seq, LANES), F32)
    tspec = pl.BlockSpec((1, ts, LANES), lambda b, s: (b, s, 0))
    return pl.pallas_call(
        _rope_tables_kernel,
        grid=(bsz, seq // ts),
        in_specs=[pl.BlockSpec((1, ts, 1), lambda b, s: (b, s, 0)), _const_spec((1, LANES)), _const_spec((2, LANES))],
        out_specs=[tspec] * 4,
        out_shape=[tab] * 4,
        compiler_params=_cparams(("parallel", "parallel")),
        name="rope_tables",
    )(pos, inv, sgn)


def _rms(x, w):
    return x * lax.rsqrt(jnp.mean(x * x, axis=-1, keepdims=True) + RMS_EPS) * w


IN0_COLS = np.cumsum((0, Q_LORA, KV_LORA, 2 * MLA_ROPE, 3 * DIL_HEADS * DIL_HD, MLA_HEADS * MLA_V + DIL_HEADS * DIL_HD))


def _in0_kernel(x_ref, w_ref, wuq_ref, wukv_ref, qnorm_ref, kvnorm_ref,
                c64_ref, s64_ref, c128_ref, s128_ref,
                qn_ref, qr_ref, kv_ref, kr_ref, qb_ref, kb_ref, vb_ref, g_ref):
    x = x_ref[0].astype(BF16)
    c64, s64, c128, s128 = c64_ref[0], s64_ref[0], c128_ref[0], s128_ref[0]

    def proj(i):
        return _dot(x, w_ref[:, IN0_COLS[i]:IN0_COLS[i + 1]])

    cq = _rms(proj(0), qnorm_ref[...])
    qa = _dot(cq.astype(BF16), wuq_ref[...]) * ((MLA_NOPE + MLA_ROPE) ** -0.5 * LOG2E)
    nope = MLA_HEADS * MLA_NOPE
    qn_ref[0] = qa[:, :nope].astype(BF16)
    for j in range(MLA_HEADS * MLA_ROPE // LANES):
        blk = qa[:, nope + j * LANES:nope + (j + 1) * LANES]
        qr_ref[0, :, j * LANES:(j + 1) * LANES] = _rope_pairs64(blk, c64, s64).astype(BF16)
    ckv = _rms(proj(1), kvnorm_ref[...])
    kv_ref[0] = _dot(ckv.astype(BF16), wukv_ref[...]).astype(BF16)
    kr_ref[0] = _rope_pairs64(proj(2), c64, s64).astype(BF16)
    hb = proj(3)
    width = DIL_HEADS * DIL_HD
    for h in range(DIL_HEADS):
        sl = slice(h * DIL_HD, (h + 1) * DIL_HD)
        qb_ref[0, :, sl] = _rope_full128(hb[:, sl], c128, s128) * (DIL_HD ** -0.5 * LOG2E)
        kb_ref[0, :, sl] = _rope_full128(hb[:, width + h * DIL_HD:width + (h + 1) * DIL_HD], c128, s128)
    vb_ref[0] = hb[:, 2 * width:]
    g_ref[0] = _silu(proj(4)).astype(BF16)


def _in0(x, tabs, w_in, q_norm, w_uq, kv_norm, w_ukv, tm=512):
    bsz, seq, _ = x.shape
    c64, s64, c128, s128 = tabs
    kr0 = Q_LORA + KV_LORA
    w_all = _splice_columns(w_in, kr0 + MLA_ROPE, kr0, 0)
    wuq3 = w_uq.reshape(Q_LORA, MLA_HEADS, MLA_NOPE + MLA_ROPE)
    wuq = jnp.concatenate([wuq3[:, :, :MLA_NOPE].reshape(Q_LORA, -1), wuq3[:, :, MLA_NOPE:].reshape(Q_LORA, -1)],
                          axis=1).astype(BF16)
    wukv = w_ukv.astype(BF16)
    weights = (w_all, wuq, wukv, q_norm.reshape(1, -1), kv_norm.reshape(1, -1))

    def rows(width):
        return pl.BlockSpec((1, tm, width), lambda b, s: (b, s, 0))

    out_widths = (1024, 512, 2048, LANES, 1024, 1024, 1024, 2048)
    out_dtypes = (BF16, BF16, BF16, BF16, F32, F32, F32, BF16)
    return pl.pallas_call(
        _in0_kernel,
        grid=(bsz, seq // tm),
        in_specs=[rows(D_MODEL)] + [_const_spec(w.shape) for w in weights] + [rows(LANES)] * 4,
        out_specs=[rows(w) for w in out_widths],
        out_shape=[jax.ShapeDtypeStruct((bsz, seq, w), dt) for w, dt in zip(out_widths, out_dtypes)],
        compiler_params=_cparams(("parallel", "parallel")),
        name="in_proj0",
    )(x, *weights, c64, s64, c128, s128)


MLA_SUB = 256
MLA_ONES_ROWS = 16
MLA_AHEAD = 5
LOG2E = 1.4426950408889634


def _mla_kernel(qn_ref, qr_ref, kn_ref, kr_ref, v_ref, o_ref, k_s, vt_s):
    h = pl.program_id(1)

    @pl.when(pl.program_id(2) == 0)
    def _():
        k_s[:, :MLA_NOPE] = kn_ref[0]
        k_s[:, MLA_NOPE:] = kr_ref[0]
        vt_s[:MLA_V, :] = v_ref[0].astype(F32).T.astype(BF16)
        vt_s[MLA_V:, :] = jnp.ones((MLA_ONES_ROWS, vt_s.shape[1]), BF16)

    lane = lax.broadcasted_iota(jnp.int32, (MLA_SUB, LANES), 1)
    mine = (lane // MLA_ROPE) == (h % 2)
    nsub = qn_ref.shape[1] // MLA_SUB

    def scores(i):
        rows = slice(i * MLA_SUB, (i + 1) * MLA_SUB)
        qr = jnp.where(mine, qr_ref[0, rows, :], jnp.zeros((MLA_SUB, LANES), BF16))
        return _dot_nt(k_s[...], jnp.concatenate([qn_ref[0, rows, :], qr], axis=1))

    ahead = [scores(i) for i in range(min(MLA_AHEAD, nsub))]
    for i in range(nsub):
        s = ahead.pop(0)
        if i + MLA_AHEAD < nsub:
            ahead.append(scores(i + MLA_AHEAD))
        m = jnp.max(s, axis=0, keepdims=True)
        p = jnp.exp2(s - m).astype(BF16)
        ot = _dot(vt_s[...], p)
        o = ot[:MLA_V] / ot[MLA_V:MLA_V + 1]
        o_ref[0, i * MLA_SUB:(i + 1) * MLA_SUB, :] = o.T.astype(o_ref.dtype)


def _mla(qn, qr, kv, kr, tq=2048):
    bsz, seq, _ = qn.shape
    return pl.pallas_call(
        _mla_kernel,
        grid=(bsz, MLA_HEADS, seq // tq),
        in_specs=[
            pl.BlockSpec((1, tq, LANES), lambda b, h, i: (b, i, h)),
            pl.BlockSpec((1, tq, LANES), lambda b, h, i: (b, i, h // 2)),
            pl.BlockSpec((1, seq, LANES), lambda b, h, i: (b, 0, 2 * h)),
            pl.BlockSpec((1, seq, LANES), lambda b, h, i: (b, 0, 0)),
            pl.BlockSpec((1, seq, LANES), lambda b, h, i: (b, 0, 2 * h + 1)),
        ],
        out_specs=pl.BlockSpec((1, tq, LANES), lambda b, h, i: (b, i, h)),
        out_shape=jax.ShapeDtypeStruct((bsz, seq, MLA_HEADS * MLA_V), BF16),
        scratch_shapes=[pltpu.VMEM((seq, MLA_NOPE + LANES), BF16), pltpu.VMEM((MLA_V + MLA_ONES_ROWS, seq), BF16)],
        compiler_params=_cparams(("parallel", "parallel", "arbitrary")),
        name="mla_attention",
    )(qn, qr, kv, kr, kv)


DIL_TQ = 128
DIL_RADIUS = 64
DIL_TILES_PER_STEP = 8


def _dil_bias_table():
    i = np.arange(DIL_TQ)[:, None]
    j = np.arange(2 * DIL_TQ)[None, :]
    tabs = [np.where(np.abs(j - off - i) <= DIL_RADIUS, 0.0, -np.inf) for off in (0, DIL_RADIUS, 2 * DIL_RADIUS)]
    return jnp.asarray(np.stack(tabs), F32)


def _dil_kernel(q_ref, k_ref, v_ref, bias_ref, o_ref, qd_s, kd_s, vd_s, m_s, l_s, acc_s):
    seq = q_ref.shape[1]
    vd_s[:, :, DIL_HD:] = jnp.ones((len(DILATED_GROUPS), seq, DIL_HD), BF16)

    def rows_of(start, size, dil):
        return pl.ds(start, size) if dil == 1 else pl.ds(start, size, stride=dil)

    for g, (_, dil) in enumerate(DILATED_GROUPS):
        length = seq // dil
        for src, dst in ((q_ref, qd_s), (k_ref, kd_s), (v_ref, vd_s)):
            for r in range(dil):
                dst[g, r * length:(r + 1) * length, :DIL_HD] = src[0, rows_of(r, length, dil), :].astype(BF16)

    def tiles_step(g, dil, which, first):
        length = seq // dil
        tiles = length // DIL_TQ
        kw = min(length, DIL_TQ + 2 * DIL_RADIUS)
        n = len(which)
        ws = [jnp.clip(t * DIL_TQ - DIL_RADIUS, 0, length - kw) for _, t in which]
        qrows = [pl.ds(pl.multiple_of(r * length + t * DIL_TQ, DIL_TQ), DIL_TQ) for r, t in which]
        krows = [pl.ds(pl.multiple_of(r * length + ws[i], DIL_RADIUS), kw) for i, (r, _) in enumerate(which)]
        out_rows = [rows_of(r + t * (DIL_TQ * dil), DIL_TQ, dil) for r, t in which]
        if tiles == 1:
            bias = [bias_ref[0, :, :kw] for _ in which]
        else:
            bias = [bias_ref[jnp.where(t == 0, 0, jnp.where(t == tiles - 1, 2, 1))] for _, t in which]
        s = [_dot_nt(qd_s[g, qrows[i], :], kd_s[g, krows[i], :]) + bias[i] for i in range(n)]
        mk = [jnp.max(t, axis=-1, keepdims=True) for t in s]
        m = [jnp.broadcast_to(t, (DIL_TQ, DIL_HD)) for t in mk]
        p = [jnp.exp2(s[i] - mk[i]).astype(BF16) for i in range(n)]
        ol = [_dot(p[i], vd_s[g, krows[i], :]) for i in range(n)]
        o = [t[:, :DIL_HD] for t in ol]
        l = [t[:, DIL_HD:] for t in ol]
        for i in range(n):
            if first:
                m_s[out_rows[i], :] = m[i]
                l_s[out_rows[i], :] = l[i]
                acc_s[out_rows[i], :] = o[i]
            else:
                m_old = m_s[out_rows[i], :]
                m_new = jnp.maximum(m_old, m[i])
                a_old = jnp.exp2(m_old - m_new)
                a_new = jnp.exp2(m[i] - m_new)
                m_s[out_rows[i], :] = m_new
                l_s[out_rows[i], :] = a_old * l_s[out_rows[i], :] + a_new * l[i]
                acc_s[out_rows[i], :] = a_old * acc_s[out_rows[i], :] + a_new * o[i]

    for g, (window, dil) in enumerate(DILATED_GROUPS):
        assert window // (2 * dil) == DIL_RADIUS
        tiles = seq // dil // DIL_TQ

        def body(it, carry, g=g, dil=dil, tiles=tiles):
            idx = [it * DIL_TILES_PER_STEP + j for j in range(DIL_TILES_PER_STEP)]
            tiles_step(g, dil, [(i // tiles, i % tiles) for i in idx], g == 0)
            return carry

        lax.fori_loop(0, dil * tiles // DIL_TILES_PER_STEP, body, 0)
    o_ref[0] = (acc_s[...] / l_s[...]).astype(o_ref.dtype)


def _dilated(qb, kb, vb):
    bsz, seq, _ = qb.shape
    spec = pl.BlockSpec((1, seq, DIL_HD), lambda b, h: (b, 0, h))
    bias = _dil_bias_table()
    grouped = pltpu.VMEM((len(DILATED_GROUPS), seq, DIL_HD), BF16)
    stat = pltpu.VMEM((seq, DIL_HD), F32)
    return pl.pallas_call(
        _dil_kernel,
        grid=(bsz, DIL_HEADS),
        in_specs=[spec] * 3 + [_const_spec(bias.shape)],
        out_specs=spec,
        out_shape=jax.ShapeDtypeStruct((bsz, seq, DIL_HEADS * DIL_HD), BF16),
        scratch_shapes=[grouped, grouped, pltpu.VMEM((len(DILATED_GROUPS), seq, 2 * DIL_HD), BF16), stat, stat, stat],
        compiler_params=_cparams(("parallel", "parallel")),
        name="dilated_attention",
    )(qb, kb, vb, bias)


OUT_SUB = 128


def _out_kernel(o1_ref, o2_ref, g_ref, x_ref, w_ref, lnw_ref, lnb_ref, out_ref):
    half = o1_ref.shape[-1]
    for i in range(o1_ref.shape[1] // OUT_SUB):
        rows = slice(i * OUT_SUB, (i + 1) * OUT_SUB)
        y1 = o1_ref[0, rows, :] * g_ref[0, rows, :half]
        y2 = o2_ref[0, rows, :] * g_ref[0, rows, half:]
        r = DEEPNORM_ALPHA * x_ref[0, rows, :] + _dot(y1, w_ref[:half, :]) + _dot(y2, w_ref[half:, :])
        mu = jnp.mean(r, axis=-1, keepdims=True)
        rc = r - mu
        var = jnp.mean(rc * rc, axis=-1, keepdims=True)
        out_ref[0, rows, :] = rc * lax.rsqrt(var + LN_EPS) * lnw_ref[...] + lnb_ref[...]


def _out_proj(o1, o2, gate, x, w_out, ln_w, ln_b, tm=512):
    bsz, seq, _ = x.shape
    half = o1.shape[-1]

    def rows(width):
        return pl.BlockSpec((1, tm, width), lambda b, s: (b, s, 0))

    w = w_out.astype(BF16)
    return pl.pallas_call(
        _out_kernel,
        grid=(bsz, seq // tm),
        in_specs=[rows(half), rows(half), rows(2 * half), rows(D_MODEL), _const_spec(w.shape),
                  _const_spec((1, D_MODEL)), _const_spec((1, D_MODEL))],
        out_specs=rows(D_MODEL),
        out_shape=jax.ShapeDtypeStruct((bsz, seq, D_MODEL), F32),
        compiler_params=_cparams(("parallel", "parallel")),
        name="out_proj",
    )(o1, o2, gate, x, w, ln_w.reshape(1, -1), ln_b.reshape(1, -1))


def _layer0(x, tabs, w_in, q_norm, w_uq, kv_norm, w_ukv, w_out, ln_w, ln_b):
    qn, qr, kv, kr, qb, kb, vb, gate = _in0(x, tabs, w_in, q_norm, w_uq, kv_norm, w_ukv)
    o_a = _mla(qn, qr, kv, kr)
    o_b = _dilated(qb, kb, vb)
    return _out_proj(o_a, o_b, gate, x, w_out, ln_w, ln_b)


GATE_BETA = 0
GATE_GC = 2 * GDN_HEADS


def _softplus(x):
    return jnp.maximum(x, 0.0) + jnp.log1p(jnp.exp(-jnp.abs(x)))


IN1_COLS = np.cumsum((0, 2 * GDN_HEADS * GDN_DK + GDN_HEADS * GDN_DV, LANES, 2 * RET_HEADS * RET_DK,
                      RET_HEADS * RET_DV, GDN_HEADS * GDN_DV + RET_HEADS * RET_DV))


def _in1_kernel(x_ref, w_ref, gp_ref, c64_ref, s64_ref,
                qkvc_ref, gate_ref, qr_ref, kr_ref, vr_ref, g_ref):
    x = x_ref[0].astype(BF16)
    tm = x.shape[0]
    c64, s64 = c64_ref[0], s64_ref[0]

    def proj(i):
        return _dot(x, w_ref[:, IN1_COLS[i]:IN1_COLS[i + 1]])

    qkvc_ref[0] = proj(0).astype(qkvc_ref.dtype)
    hg = proj(1)
    beta = jax.nn.sigmoid(hg)
    g = -jnp.exp(gp_ref[0:1, :]) * _softplus(hg + gp_ref[1:2, :])
    pos = lax.broadcasted_iota(jnp.int32, hg.shape, 0) % GDN_CHUNK
    pre, suf = g, g
    step = 1
    while step < GDN_CHUNK:
        pre = pre + jnp.where(pos >= step, pltpu.roll(pre, step, 0), 0.0)
        suf = suf + jnp.where(pos < GDN_CHUNK - step, pltpu.roll(suf, tm - step, 0), 0.0)
        step *= 2
    lane = lax.broadcasted_iota(jnp.int32, hg.shape, 1)
    gc = jnp.where(lane < GATE_GC + GDN_HEADS, pre, suf)
    gate_ref[0] = jnp.where(lane < GATE_GC, beta, gc)
    hqk = proj(2)
    width = RET_HEADS * RET_DK
    for j in range(width // LANES):
        qr_ref[0, :, j * LANES:(j + 1) * LANES] = _rope_pairs64(hqk[:, j * LANES:(j + 1) * LANES], c64, s64).astype(BF16)
        kblk = hqk[:, width + j * LANES:width + (j + 1) * LANES]
        kr_ref[0, :, j * LANES:(j + 1) * LANES] = (_rope_pairs64(kblk, c64, s64) * (RET_DK ** -0.5)).astype(BF16)
    vr_ref[0] = proj(3).astype(BF16)
    g_ref[0] = _silu(proj(4)).astype(BF16)


def _in1(x, tabs, w_in, a_log, dt_bias, tm=512):
    bsz, seq, _ = x.shape
    c64, s64, _, _ = tabs
    conv_ch = 2 * GDN_HEADS * GDN_DK + GDN_HEADS * GDN_DV
    gate_end = conv_ch + 4 * GDN_HEADS
    w_all = _splice_columns(w_in, gate_end, gate_end, LANES - 4 * GDN_HEADS)
    pad = (GATE_GC, LANES - GATE_GC - 2 * GDN_HEADS)
    gp = jnp.stack([jnp.pad(a_log.reshape(-1), pad), jnp.pad(dt_bias.reshape(-1), pad)]).astype(F32)
    weights = (w_all, gp)

    def rows(width):
        return pl.BlockSpec((1, tm, width), lambda b, s: (b, s, 0))

    out_widths = (conv_ch, LANES, 512, 512, 1024, 2048)
    out_dtypes = (BF16, F32, BF16, BF16, BF16, BF16)
    return pl.pallas_call(
        _in1_kernel,
        grid=(bsz, seq // tm),
        in_specs=[rows(D_MODEL)] + [_const_spec(w.shape) for w in weights] + [rows(LANES)] * 2,
        out_specs=[rows(w) for w in out_widths],
        out_shape=[jax.ShapeDtypeStruct((bsz, seq, w), dt) for w, dt in zip(out_widths, out_dtypes)],
        compiler_params=_cparams(("parallel", "parallel")),
        name="in_proj1",
    )(x, *weights, c64, s64)


CONV_PAD = 8
GDN_P1_CHUNKS = 8
GDN_P3_CHUNKS = 8
GDN_LEVELS = 7


def _gdn_level_masks():
    i = np.arange(GDN_CHUNK)[:, None]
    j = np.arange(GDN_CHUNK)[None, :]
    masks = [((i >> (k + 1)) == (j >> (k + 1))) & ((i >> k) != (j >> k)) for k in range(GDN_LEVELS)]
    return jnp.asarray(np.stack(masks), F32)


def _gdn_kernel(qraw_ref, kraw_ref, vraw_ref, wq_ref, wk_ref, wv_ref, gcol_ref, grow_ref, norm_ref, lvl_ref, o_ref,
                xpad_s, q_s, k_s, v_s, bcb_s, gcb_s, w_s, u_s, att_s, mt_s, bt_s, st_s):
    seq = q_s.shape[0]
    nchunk = seq // GDN_CHUNK

    zeros = jnp.zeros((CONV_PAD, LANES), F32)
    xpad_s[0:CONV_PAD, :] = zeros
    xpad_s[CONV_PAD + seq:, :] = zeros

    def conv_silu(raw_ref, w_ref):
        xpad_s[CONV_PAD:CONV_PAD + seq, :] = raw_ref[0]
        acc = None
        for j in range(CONV_K):
            off = CONV_PAD - (CONV_K - 1) // 2 + j
            term = xpad_s[off:off + seq, :] * w_ref[j:j + 1, :]
            acc = term if acc is None else acc + term
        return _silu(acc)

    def l2norm(t):
        return t * lax.rsqrt(jnp.sum(t * t, axis=-1, keepdims=True) + L2_EPS)

    q_s[...] = l2norm(conv_silu(qraw_ref, wq_ref)) * (GDN_DK ** -0.5)
    k_s[...] = l2norm(conv_silu(kraw_ref, wk_ref))
    v_s[...] = conv_silu(vraw_ref, wv_ref)

    for d in range(2):
        bcb_s[d] = jnp.broadcast_to(gcol_ref[0, 0, d:d + 1, :], (LANES, seq)).T
        gcb_s[d] = jnp.broadcast_to(gcol_ref[0, 0, 2 + d:3 + d, :], (LANES, seq)).T

    ii = lax.broadcasted_iota(jnp.int32, (GDN_CHUNK, GDN_CHUNK), 0)
    jj = lax.broadcasted_iota(jnp.int32, (GDN_CHUNK, GDN_CHUNK), 1)
    causal = (ii >= jj, ii <= jj)
    strict = (ii > jj, ii < jj)

    def chunk_rows(c):
        base = pl.multiple_of(c * GDN_CHUNK, GDN_CHUNK)
        return base, pl.ds(base, GDN_CHUNK)

    def chunk_log_decay(d, base):
        return gcb_s[d, pl.ds(base + (GDN_CHUNK - 1 if d == 0 else 0), 1), :]

    def phase1(it, carry):
        chains = [(it * GDN_P1_CHUNKS + j, d) for j in range(GDN_P1_CHUNKS) for d in range(2)]
        rows = [chunk_rows(c) for c, _ in chains]
        kf = [k_s[r, :] for _, r in rows]
        kc = [t.astype(BF16) for t in kf]
        gcb = [gcb_s[d, r, :] for (_, d), (_, r) in zip(chains, rows)]
        bcb = [bcb_s[d, r, :] for (_, d), (_, r) in zip(chains, rows)]
        kb = [kf[i] * bcb[i] for i in range(len(chains))]
        kk = [_dot_nt(kb[i].astype(BF16), kc[i]) for i in range(len(chains))]
        decay = [jnp.exp(jnp.where(causal[d], gcb[i] - grow_ref[0, 0, d, pl.ds(c, 1), :], -jnp.inf))
                 for i, (c, d) in enumerate(chains)]
        a = [jnp.where(strict[d], kk[i] * decay[i], 0.0) for i, (_, d) in enumerate(chains)]
        n = [-(t * lvl_ref[0]) for t in a]
        for lvl in range(1, GDN_LEVELS):
            ak = [t * lvl_ref[lvl] for t in a]
            y = [ak[i] + _dot(ak[i].astype(BF16), n[i].astype(BF16)) for i in range(len(chains))]
            n = [n[i] - y[i] - _dot(n[i].astype(BF16), y[i].astype(BF16)) for i in range(len(chains))]
        nb = [t.astype(BF16) for t in n]
        wpre = [kb[i] * jnp.exp(gcb[i]) for i in range(len(chains))]
        upre = [v_s[r, :] * bcb[i] for i, (_, r) in enumerate(rows)]
        w = [(wpre[i] + _dot(nb[i], wpre[i].astype(BF16))).astype(BF16) for i in range(len(chains))]
        u = [upre[i] + _dot(nb[i], upre[i].astype(BF16)) for i in range(len(chains))]
        qkc = [_dot_nt(q_s[rows[2 * j][1], :].astype(BF16), kc[2 * j]) for j in range(GDN_P1_CHUNKS)]
        qk = [qkc[i // 2] for i in range(len(chains))]
        kd = [(kf[i] * jnp.exp(chunk_log_decay(d, rows[i][0]) - gcb[i])).astype(BF16) for i, (_, d) in enumerate(chains)]
        mt = [_dot_tn(w[i], kd[i]) for i in range(len(chains))]
        bt = [_dot_tn(u[i].astype(BF16), kd[i]) for i in range(len(chains))]
        for i, (c, d) in enumerate(chains):
            r = rows[i][1]
            w_s[d, r, :] = w[i]
            u_s[d, r, :] = u[i]
            att_s[d, r, :] = (qk[i] * decay[i]).astype(BF16)
            mt_s[d, c] = (-mt[i]).astype(BF16)
            bt_s[d, c] = bt[i]
        return carry

    lax.fori_loop(0, nchunk // GDN_P1_CHUNKS, phase1, 0)

    def phase2(i, carry):
        new = []
        for d in range(2):
            c = i if d == 0 else nchunk - 1 - i
            st = carry[d]
            st_s[d, c] = st.astype(BF16)
            gl = chunk_log_decay(d, pl.multiple_of(c * GDN_CHUNK, GDN_CHUNK))
            new.append(st * jnp.exp(gl) + _dot(st.astype(BF16), mt_s[d, c]) + bt_s[d, c])
        return tuple(new)

    zero_state = jnp.zeros((GDN_DV, GDN_DK), F32)
    lax.fori_loop(0, nchunk, phase2, (zero_state, zero_state))

    def phase3(it, carry):
        chains = [(it * GDN_P3_CHUNKS + j, d) for j in range(GDN_P3_CHUNKS) for d in range(2)]
        rows = [chunk_rows(c)[1] for c, _ in chains]
        qg = [(q_s[r, :] * jnp.exp(gcb_s[d, r, :])).astype(BF16) for (_, d), r in zip(chains, rows)]
        m = [_dot_nt(jnp.concatenate([w_s[d, rows[i], :], qg[i]], axis=0), st_s[d, c]) for i, (c, d) in enumerate(chains)]
        vnew = [(u_s[d, rows[i], :] - m[i][:GDN_CHUNK]).astype(BF16) for i, (_, d) in enumerate(chains)]
        od = [m[i][GDN_CHUNK:] + _dot(att_s[d, rows[i], :], vnew[i]) for i, (_, d) in enumerate(chains)]
        for j in range(GDN_P3_CHUNKS):
            o = od[2 * j] + od[2 * j + 1]
            o_ref[0, rows[2 * j], :] = (o * lax.rsqrt(jnp.mean(o * o, axis=-1, keepdims=True) + RMS_EPS)
                                        * norm_ref[...]).astype(o_ref.dtype)
        return carry

    lax.fori_loop(0, nchunk // GDN_P3_CHUNKS, phase3, 0)


def _gdn(qkvc, conv_w, gcol, grow, gdn_norm):
    bsz, seq, _ = qkvc.shape
    nh = GDN_HEADS

    def col(off):
        return pl.BlockSpec((1, seq, LANES), lambda b, h: (b, 0, off + h))

    def wcol(off):
        return pl.BlockSpec((CONV_K, LANES), lambda b, h: (0, off + h))

    nchunk = seq // GDN_CHUNK
    big = pltpu.VMEM((seq, LANES), F32)

    def per_dir(dtype):
        return pltpu.VMEM((2, seq, LANES), dtype)

    def per_chunk(dtype):
        return pltpu.VMEM((2, nchunk, GDN_DV, GDN_DK), dtype)

    return pl.pallas_call(
        _gdn_kernel,
        grid=(bsz, nh),
        in_specs=[col(0), col(nh), col(2 * nh), wcol(0), wcol(nh), wcol(2 * nh),
                  pl.BlockSpec((1, 1, 4, seq), lambda b, h: (b, h, 0, 0)),
                  pl.BlockSpec((1, 1, 2, nchunk, GDN_CHUNK), lambda b, h: (b, h, 0, 0, 0)),
                  _const_spec((1, GDN_DV)), _const_spec((GDN_LEVELS, GDN_CHUNK, GDN_CHUNK))],
        out_specs=pl.BlockSpec((1, seq, GDN_DV), lambda b, h: (b, 0, h)),
        out_shape=jax.ShapeDtypeStruct((bsz, seq, nh * GDN_DV), BF16),
        scratch_shapes=[pltpu.VMEM((seq + 2 * CONV_PAD, LANES), F32), big, big, big,
                        per_dir(F32), per_dir(F32), per_dir(BF16), per_dir(F32), per_dir(BF16),
                        per_chunk(BF16), per_chunk(F32), per_chunk(BF16)],
        compiler_params=_cparams(("parallel", "parallel")),
        name="gated_deltanet",
    )(qkvc, qkvc, qkvc, conv_w, conv_w, conv_w, gcol, grow, gdn_norm.reshape(1, -1), _gdn_level_masks())


GDN_HPS = 1
CONV_ROWS = 256
G_BETA, G_GC = 0, 2


def _gdn2_kernel(qraw_ref, kraw_ref, vraw_ref, wq_ref, wk_ref, wv_ref, grow_ref, norm_ref, lvl_ref, o_ref,
                 xpad_s, q_s, k_s, v_s, w_s, u_s, att_s, qg_s, mt_s, bt_s, st_s, gl_s, o_s):
    seq = q_s.shape[1]
    nchunk = seq // GDN_CHUNK

    zeros = jnp.zeros((CONV_PAD, LANES), F32)
    xpad_s[0:CONV_PAD, :] = zeros
    xpad_s[CONV_PAD + seq:, :] = zeros

    def conv_silu(raw_ref, w_ref, lanes, dst, finish):
        xpad_s[CONV_PAD:CONV_PAD + seq, :] = raw_ref[0, :, lanes].astype(F32)
        for blk in range(seq // CONV_ROWS):
            acc = None
            for j in range(CONV_K):
                off = blk * CONV_ROWS + CONV_PAD - (CONV_K - 1) // 2 + j
                term = xpad_s[off:off + CONV_ROWS, :] * w_ref[j:j + 1, lanes]
                acc = term if acc is None else acc + term
            dst[blk * CONV_ROWS:(blk + 1) * CONV_ROWS, :] = finish(_silu(acc))

    def l2norm(t):
        return t * lax.rsqrt(jnp.sum(t * t, axis=-1, keepdims=True) + L2_EPS)

    for hh in range(GDN_HPS):
        lanes = slice(hh * LANES, (hh + 1) * LANES)
        conv_silu(qraw_ref, wq_ref, lanes, q_s.at[hh], lambda t: l2norm(t) * (GDN_DK ** -0.5))
        conv_silu(kraw_ref, wk_ref, lanes, k_s.at[hh], l2norm)
        conv_silu(vraw_ref, wv_ref, lanes, v_s.at[hh], lambda t: t)

    ii = lax.broadcasted_iota(jnp.int32, (GDN_CHUNK, GDN_CHUNK), 0)
    jj = lax.broadcasted_iota(jnp.int32, (GDN_CHUNK, GDN_CHUNK), 1)
    causal = (ii >= jj, ii <= jj)
    strict = (ii > jj, ii < jj)

    def chunk_rows(c):
        return pl.ds(pl.multiple_of(c * GDN_CHUNK, GDN_CHUNK), GDN_CHUNK)

    def token_major(row):
        return jnp.broadcast_to(row, (LANES, GDN_CHUNK)).T

    def phase1(idx, carry):
        hh = idx // (nchunk // GDN_P1_CHUNKS)
        it = idx % (nchunk // GDN_P1_CHUNKS)
        chains = [(it * GDN_P1_CHUNKS + j, d) for j in range(GDN_P1_CHUNKS) for d in range(2)]
        n_ch = len(chains)
        rows = [chunk_rows(c) for c, _ in chains]
        kf = [k_s[hh, r, :] for r in rows]
        kc = [t.astype(BF16) for t in kf]
        grow = [grow_ref[0, hh, G_GC + d, pl.ds(c, 1), :] for c, d in chains]
        gcb = [token_major(t) for t in grow]
        bcb = [token_major(grow_ref[0, hh, G_BETA + d, pl.ds(c, 1), :]) for c, d in chains]
        kb = [kf[i] * bcb[i] for i in range(n_ch)]
        kk = [_dot_nt(kb[i].astype(BF16), kc[i]) for i in range(n_ch)]
        decay = [jnp.exp(jnp.where(causal[d], gcb[i] - grow[i], -jnp.inf)) for i, (_, d) in enumerate(chains)]
        a = [jnp.where(strict[d], kk[i] * decay[i], 0.0) for i, (_, d) in enumerate(chains)]
        n = [-(t * lvl_ref[0]) for t in a]
        for lvl in range(1, GDN_LEVELS):
            ak = [t * lvl_ref[lvl] for t in a]
            y = [ak[i] + _dot(ak[i].astype(BF16), n[i].astype(BF16)) for i in range(n_ch)]
            n = [n[i] - y[i] - _dot(n[i].astype(BF16), y[i].astype(BF16)) for i in range(n_ch)]
        nb = [t.astype(BF16) for t in n]
        egc = [jnp.exp(t) for t in gcb]
        wpre = [kb[i] * egc[i] for i in range(n_ch)]
        upre = [v_s[hh, rows[i], :] * bcb[i] for i in range(n_ch)]
        w = [(wpre[i] + _dot(nb[i], wpre[i].astype(BF16))).astype(BF16) for i in range(n_ch)]
        u = [upre[i] + _dot(nb[i], upre[i].astype(BF16)) for i in range(n_ch)]
        qf = [q_s[hh, rows[2 * j], :] for j in range(GDN_P1_CHUNKS)]
        qkc = [_dot_nt(qf[j].astype(BF16), kc[2 * j]) for j in range(GDN_P1_CHUNKS)]
        gl = [gcb[i][GDN_CHUNK - 1:GDN_CHUNK, :] if d == 0 else gcb[i][0:1, :] for i, (_, d) in enumerate(chains)]
        kd = [(kf[i] * jnp.exp(gl[i] - gcb[i])).astype(BF16) for i in range(n_ch)]
        mt = [_dot_tn(w[i], kd[i]) for i in range(n_ch)]
        bt = [_dot_tn(u[i].astype(BF16), kd[i]) for i in range(n_ch)]
        for i, (c, d) in enumerate(chains):
            r = rows[i]
            w_s[hh, d, r, :] = w[i]
            u_s[hh, d, r, :] = u[i]
            att_s[hh, d, r, :] = (qkc[i // 2] * decay[i]).astype(BF16)
            qg_s[hh, d, r, :] = (qf[i // 2] * egc[i]).astype(BF16)
            mt_s[hh, d, c] = (-mt[i]).astype(BF16)
            bt_s[hh, d, c] = bt[i]
            gl_s[hh, d, c] = jnp.broadcast_to(jnp.exp(gl[i]), (8, LANES))
        return carry

    lax.fori_loop(0, GDN_HPS * (nchunk // GDN_P1_CHUNKS), phase1, 0)

    def phase2(i, carry):
        new = []
        for hh in range(GDN_HPS):
            for d in range(2):
                c = i if d == 0 else nchunk - 1 - i
                st = carry[2 * hh + d]
                st_s[hh, d, c] = st.astype(BF16)
                new.append(st * gl_s[hh, d, c, 0:1, :] + _dot(st.astype(BF16), mt_s[hh, d, c]) + bt_s[hh, d, c])
        return tuple(new)

    zero_state = jnp.zeros((GDN_DV, GDN_DK), F32)
    lax.fori_loop(0, nchunk, phase2, (zero_state,) * (2 * GDN_HPS))

    def phase3(idx, carry):
        hh = idx // (nchunk // GDN_P3_CHUNKS)
        it = idx % (nchunk // GDN_P3_CHUNKS)
        chains = [(it * GDN_P3_CHUNKS + j, d) for j in range(GDN_P3_CHUNKS) for d in range(2)]
        rows = [chunk_rows(c) for c, _ in chains]
        m = [_dot_nt(jnp.concatenate([w_s[hh, d, rows[i], :], qg_s[hh, d, rows[i], :]], axis=0), st_s[hh, d, c])
             for i, (c, d) in enumerate(chains)]
        vnew = [(u_s[hh, d, rows[i], :] - m[i][:GDN_CHUNK]).astype(BF16) for i, (_, d) in enumerate(chains)]
        od = [m[i][GDN_CHUNK:] + _dot(att_s[hh, d, rows[i], :], vnew[i]) for i, (_, d) in enumerate(chains)]
        for j in range(GDN_P3_CHUNKS):
            o = od[2 * j] + od[2 * j + 1]
            o_s[hh, rows[2 * j], :] = (o * lax.rsqrt(jnp.mean(o * o, axis=-1, keepdims=True) + RMS_EPS)
                                       * norm_ref[...]).astype(o_s.dtype)
        return carry

    lax.fori_loop(0, GDN_HPS * (nchunk // GDN_P3_CHUNKS), phase3, 0)
    for hh in range(GDN_HPS):
        o_ref[0, :, hh * LANES:(hh + 1) * LANES] = o_s[hh]


def _gdn2(qkvc, conv_w, grow, gdn_norm):
    bsz, seq, _ = qkvc.shape
    groups = GDN_HEADS // GDN_HPS
    width = GDN_HPS * LANES
    nchunk = seq // GDN_CHUNK

    def col(off):
        return pl.BlockSpec((1, seq, width), lambda b, g: (b, 0, off + g))

    def wcol(off):
        return pl.BlockSpec((CONV_K, width), lambda b, g: (0, off + g))

    def per_head(shape, dtype):
        return pltpu.VMEM((GDN_HPS,) + shape, dtype)

    return pl.pallas_call(
        _gdn2_kernel,
        grid=(bsz, groups),
        in_specs=[col(0), col(groups), col(2 * groups), wcol(0), wcol(groups), wcol(2 * groups),
                  pl.BlockSpec((1, GDN_HPS, 4, nchunk, GDN_CHUNK), lambda b, g: (b, g, 0, 0, 0)),
                  _const_spec((1, GDN_DV)), _const_spec((GDN_LEVELS, GDN_CHUNK, GDN_CHUNK))],
        out_specs=pl.BlockSpec((1, seq, width), lambda b, g: (b, 0, g)),
        out_shape=jax.ShapeDtypeStruct((bsz, seq, GDN_HEADS * GDN_DV), BF16),
        scratch_shapes=[pltpu.VMEM((seq + 2 * CONV_PAD, LANES), F32),
                        per_head((seq, LANES), F32), per_head((seq, LANES), F32), per_head((seq, LANES), F32),
                        per_head((2, seq, LANES), BF16), per_head((2, seq, LANES), F32),
                        per_head((2, seq, LANES), BF16), per_head((2, seq, LANES), BF16),
                        per_head((2, nchunk, GDN_DV, GDN_DK), BF16), per_head((2, nchunk, GDN_DV, GDN_DK), F32),
                        per_head((2, nchunk, GDN_DV, GDN_DK), BF16), per_head((2, nchunk, 8, LANES), F32),
                        per_head((seq, LANES), BF16)],
        compiler_params=_cparams(("parallel", "parallel")),
        name="gated_deltanet",
    )(qkvc, qkvc, qkvc, conv_w, conv_w, conv_w, grow, gdn_norm.reshape(1, -1), _gdn_level_masks())


def _ret_tables():
    heads = np.arange(RET_HEADS, dtype=np.float64)
    log_gamma = np.log1p(-(2.0 ** (-RET_DECAY_BASE - heads)))[:, None]
    idx = np.arange(RET_CHUNK, dtype=np.float64)
    dmat = np.exp(log_gamma[:, :, None] * np.abs(idx[:, None] - idx[None, :])[None])
    xi = np.exp(log_gamma * (idx + 1.0))
    zeta = np.exp(log_gamma * (RET_CHUNK - 1.0 - idx))
    gch = np.exp(log_gamma * RET_CHUNK)

    def cb(t):
        return jnp.asarray(np.broadcast_to(t[:, :, None], t.shape + (LANES,)), F32)

    return jnp.asarray(dmat, F32), cb(xi), cb(zeta), cb(gch)


def _ret_kernel(q_ref, k_ref, v_ref, d_ref, xi_ref, zeta_ref, gch_ref, nw_ref, nb_ref, o_ref, kvf_s, kvb_s):
    seq = q_ref.shape[1]
    nchunk = seq // RET_CHUNK
    h = pl.program_id(1)
    xi, zeta, gch = xi_ref[0], zeta_ref[0], gch_ref[0]
    lane = lax.broadcasted_iota(jnp.int32, (RET_CHUNK, LANES), 1)
    mine = (lane // RET_DK) == (h % 2)

    def rows(c):
        return slice(c * RET_CHUNK, (c + 1) * RET_CHUNK)

    for c in range(nchunk):
        kc = k_ref[0, rows(c), :].astype(F32)
        vc = v_ref[0, rows(c), :]
        kvf_s[c] = _dot_tn((kc * zeta).astype(BF16), vc)
        kvb_s[c] = _dot_tn((kc * xi).astype(BF16), vc)
    state = jnp.zeros((LANES, RET_DV), F32)
    for c in range(nchunk):
        nxt = state * gch + kvf_s[c]
        kvf_s[c] = state
        state = nxt
    state = jnp.zeros((LANES, RET_DV), F32)
    for c in reversed(range(nchunk)):
        nxt = state * gch + kvb_s[c]
        kvb_s[c] = state
        state = nxt
    for c in range(nchunk):
        q = jnp.where(mine, q_ref[0, rows(c), :], jnp.zeros((RET_CHUNK, LANES), BF16))
        qf = q.astype(F32)
        s = _dot_nt(q, k_ref[0, rows(c), :]) * d_ref[0]
        o = _dot(s.astype(BF16), v_ref[0, rows(c), :])
        o = o + _dot((qf * xi).astype(BF16), kvf_s[c].astype(BF16))
        o = o + _dot((qf * zeta).astype(BF16), kvb_s[c].astype(BF16))
        mu = jnp.mean(o, axis=-1, keepdims=True)
        oc = o - mu
        var = jnp.mean(oc * oc, axis=-1, keepdims=True)
        o_ref[0, rows(c), :] = (oc * lax.rsqrt(var + LN_EPS) * nw_ref[...] + nb_ref[...]).astype(o_ref.dtype)


def _retention(qr, kr, vr, norm_w, norm_b):
    bsz, seq, _ = vr.shape
    dmat, xi, zeta, gch = _ret_tables()
    pair = pl.BlockSpec((1, seq, LANES), lambda b, h: (b, 0, h // 2))
    head = pl.BlockSpec((1, seq, RET_DV), lambda b, h: (b, 0, h))

    def per_head(shape):
        return pl.BlockSpec((1,) + shape, lambda b, h: (h, 0, 0))

    vec = pl.BlockSpec((1, RET_DV), lambda b, h: (0, h))
    state = pltpu.VMEM((seq // RET_CHUNK, LANES, RET_DV), F32)
    return pl.pallas_call(
        _ret_kernel,
        grid=(bsz, RET_HEADS),
        in_specs=[pair, pair, head, per_head((RET_CHUNK, RET_CHUNK)), per_head((RET_CHUNK, LANES)),
                  per_head((RET_CHUNK, LANES)), per_head((1, LANES)), vec, vec],
        out_specs=head,
        out_shape=jax.ShapeDtypeStruct((bsz, seq, RET_HEADS * RET_DV), BF16),
        scratch_shapes=[state, state],
        compiler_params=_cparams(("parallel", "parallel")),
        name="retention",
    )(qr, kr, vr, dmat, xi, zeta, gch, norm_w.reshape(1, -1), norm_b.reshape(1, -1))


def _layer1(x, tabs, w_in, conv_w, a_log, dt_bias, gdn_norm, ret_norm_w, ret_norm_b, w_out, ln_w, ln_b):
    bsz, seq, _ = x.shape
    qkvc, gates, qr, kr, vr, gate = _in1(x, tabs, w_in, a_log, dt_bias)
    gsl = gates[:, :, :4 * GDN_HEADS].reshape(bsz, seq, 4, GDN_HEADS)
    grow = gsl.transpose(0, 3, 2, 1).reshape(bsz, GDN_HEADS, 4, seq // GDN_CHUNK, GDN_CHUNK)
    o_c = _gdn2(qkvc, conv_w, grow, gdn_norm)
    o_d = _retention(qr, kr, vr, ret_norm_w, ret_norm_b)
    return _out_proj(o_c, o_d, gate, x, w_out, ln_w, ln_b)


def kernel(x, positions, even_w_in, even_q_norm, even_w_uq, even_kv_norm, even_w_ukv, even_w_out, even_ln_w, even_ln_b, odd_w_in, odd_conv_w, odd_a_log, odd_dt_bias, odd_gdn_norm, odd_ret_norm_w, odd_ret_norm_b, odd_w_out, odd_ln_w, odd_ln_b):
    tabs = _rope_tables(positions)
    x = _layer0(x, tabs, even_w_in[0], even_q_norm[0], even_w_uq[0], even_kv_norm[0], even_w_ukv[0],
                even_w_out[0], even_ln_w[0], even_ln_b[0])
    x = _layer1(x, tabs, odd_w_in[0], odd_conv_w[0], odd_a_log[0], odd_dt_bias[0], odd_gdn_norm[0],
                odd_ret_norm_w[0], odd_ret_norm_b[0], odd_w_out[0], odd_ln_w[0], odd_ln_b[0])
    return x
```

```python
import functools

import numpy as np
import jax
import jax.numpy as jnp
from jax import lax
from jax.experimental import pallas as pl
from jax.experimental.pallas import tpu as pltpu

F32 = jnp.float32
BF16 = jnp.bfloat16

D_MODEL = 1024
DEPTH = 2
DEEPNORM_ALPHA = (2 * DEPTH) ** 0.25
ROPE_THETA = 10000.0
RMS_EPS = 1e-6
LN_EPS = 1e-5
L2_EPS = 1e-6

MLA_HEADS = 8
MLA_NOPE = 128
MLA_ROPE = 64
MLA_V = 128
Q_LORA = 768
KV_LORA = 256

DIL_HEADS = 8
DIL_HD = 128
DILATED_GROUPS = ((128, 1), (512, 4), (2048, 16))

GDN_HEADS = 8
GDN_DK = 128
GDN_DV = 128
CONV_K = 5
GDN_CHUNK = 128

RET_HEADS = 8
RET_DK = 64
RET_DV = 128
RET_DECAY_BASE = 5.0
RET_CHUNK = 256

LANES = 128
VMEM_LIMIT = 56 * 1024 * 1024


def _cparams(sem):
    return pltpu.CompilerParams(dimension_semantics=sem, vmem_limit_bytes=VMEM_LIMIT)


def _dot(a, b):
    return jnp.dot(a, b, preferred_element_type=F32)


def _dot_nt(a, b):
    return lax.dot_general(a, b, (((1,), (1,)), ((), ())), preferred_element_type=F32)


def _dot_tn(a, b):
    return lax.dot_general(a, b, (((0,), (0,)), ((), ())), preferred_element_type=F32)


def _const_spec(shape):
    nd = len(shape)
    return pl.BlockSpec(shape, lambda *_: (0,) * nd, pipeline_mode=pl.Buffered(1))


def _rope_pairs64(x, cos, sin_signed):
    lane = lax.broadcasted_iota(jnp.int32, x.shape, 1)
    swapped = jnp.where((lane % 64) < 32, pltpu.roll(x, 96, 1), pltpu.roll(x, 32, 1))
    return x * cos + swapped * sin_signed


def _rope_full128(x, cos, sin_signed):
    return x * cos + pltpu.roll(x, 64, 1) * sin_signed


def _splice_kernel(w_ref, o_ref, *, head_end, tail_start, gap):
    w = w_ref[...]
    parts = [w[:, :head_end], w[:, tail_start:]]
    if gap:
        parts.insert(1, jnp.zeros((w.shape[0], gap), w.dtype))
    o_ref[...] = jnp.concatenate(parts, axis=1).astype(o_ref.dtype)


def _splice_columns(w, head_end, tail_start, gap, tr=128):
    rows, cols = w.shape
    total = head_end + gap + cols - tail_start
    return pl.pallas_call(
        functools.partial(_splice_kernel, head_end=head_end, tail_start=tail_start, gap=gap),
        grid=(rows // tr,),
        in_specs=[pl.BlockSpec((tr, cols), lambda i: (i, 0))],
        out_specs=pl.BlockSpec((tr, total), lambda i: (i, 0)),
        out_shape=jax.ShapeDtypeStruct((rows, total), BF16),
        compiler_params=_cparams(("parallel",)),
        name="regroup_weight",
    )(w)


def _silu(x):
    return x * jax.nn.sigmoid(x)


def _rope_tables_kernel(pos_ref, inv_ref, sgn_ref, c64_ref, s64_ref, c128_ref, s128_ref):
    ang = pos_ref[0] * inv_ref[...]
    cos, sin = jnp.cos(ang), jnp.sin(ang)
    c32, s32 = cos[:, :32], sin[:, :32]
    c64, s64 = cos[:, 32:96], sin[:, 32:96]
    c64_ref[0] = jnp.concatenate([c32] * 4, axis=1)
    s64_ref[0] = jnp.concatenate([s32] * 4, axis=1) * sgn_ref[0:1, :]
    c128_ref[0] = jnp.concatenate([c64] * 2, axis=1)
    s128_ref[0] = jnp.concatenate([s64] * 2, axis=1) * sgn_ref[1:2, :]


def _rope_tables(positions):
    bsz, seq = positions.shape
    ts = 512
    inv32 = ROPE_THETA ** (-jnp.arange(0, 64, 2, dtype=F32) / 64)
    inv64 = ROPE_THETA ** (-jnp.arange(0, 128, 2, dtype=F32) / 128)
    inv = jnp.concatenate([inv32, inv64, jnp.zeros((32,), F32)])[None, :]
    sgn = jnp.asarray(np.stack([np.tile(np.repeat([-1.0, 1.0], 32), 2), np.repeat([-1.0, 1.0], 64)]), F32)
    pos = positions.astype(F32)[..., None]
    tab = jax.ShapeDtypeStruct((bsz, seq, LANES), F32)
    tspec = pl.BlockSpec((1, ts, LANES), lambda b, s: (b, s, 0))
    return pl.pallas_call(
        _rope_tables_kernel,
        grid=(bsz, seq // ts),
        in_specs=[pl.BlockSpec((1, ts, 1), lambda b, s: (b, s, 0)), _const_spec((1, LANES)), _const_spec((2, LANES))],
        out_specs=[tspec] * 4,
        out_shape=[tab] * 4,
        compiler_params=_cparams(("parallel", "parallel")),
        name="rope_tables",
    )(pos, inv, sgn)


def _rms(x, w):
    return x * lax.rsqrt(jnp.mean(x * x, axis=-1, keepdims=True) + RMS_EPS) * w


IN0_COLS = np.cumsum((0, Q_LORA, KV_LORA, 2 * MLA_ROPE, 3 * DIL_HEADS * DIL_HD, MLA_HEADS * MLA_V + DIL_HEADS * DIL_HD))


def _in0_kernel(x_ref, w_ref, wuq_ref, wukv_ref, qnorm_ref, kvnorm_ref,
                c64_ref, s64_ref, c128_ref, s128_ref,
                qn_ref, qr_ref, kv_ref, kr_ref, qb_ref, kb_ref, vb_ref, g_ref):
    x = x_ref[0].astype(BF16)
    c64, s64, c128, s128 = c64_ref[0], s64_ref[0], c128_ref[0], s128_ref[0]

    def proj(i):
        return _dot(x, w_ref[:, IN0_COLS[i]:IN0_COLS[i + 1]])

    cq = _rms(proj(0), qnorm_ref[...])
    qa = _dot(cq.astype(BF16), wuq_ref[...]) * ((MLA_NOPE + MLA_ROPE) ** -0.5 * LOG2E)
    nope = MLA_HEADS * MLA_NOPE
    qn_ref[0] = qa[:, :nope].astype(BF16)
    for j in range(MLA_HEADS * MLA_ROPE // LANES):
        blk = qa[:, nope + j * LANES:nope + (j + 1) * LANES]
        qr_ref[0, :, j * LANES:(j + 1) * LANES] = _rope_pairs64(blk, c64, s64).astype(BF16)
    ckv = _rms(proj(1), kvnorm_ref[...])
    kv_ref[0] = _dot(ckv.astype(BF16), wukv_ref[...]).astype(BF16)
    kr_ref[0] = _rope_pairs64(proj(2), c64, s64).astype(BF16)
    hb = proj(3)
    width = DIL_HEADS * DIL_HD
    for h in range(DIL_HEADS):
        sl = slice(h * DIL_HD, (h + 1) * DIL_HD)
        qb_ref[0, :, sl] = _rope_full128(hb[:, sl], c128, s128) * (DIL_HD ** -0.5 * LOG2E)
        kb_ref[0, :, sl] = _rope_full128(hb[:, width + h * DIL_HD:width + (h + 1) * DIL_HD], c128, s128)
    vb_ref[0] = hb[:, 2 * width:]
    g_ref[0] = _silu(proj(4)).astype(BF16)


def _in0(x, tabs, w_in, q_norm, w_uq, kv_norm, w_ukv, tm=512):
    bsz, seq, _ = x.shape
    c64, s64, c128, s128 = tabs
    kr0 = Q_LORA + KV_LORA
    w_all = _splice_columns(w_in, kr0 + MLA_ROPE, kr0, 0)
    wuq3 = w_uq.reshape(Q_LORA, MLA_HEADS, MLA_NOPE + MLA_ROPE)
    wuq = jnp.concatenate([wuq3[:, :, :MLA_NOPE].reshape(Q_LORA, -1), wuq3[:, :, MLA_NOPE:].reshape(Q_LORA, -1)],
                          axis=1).astype(BF16)
    wukv = w_ukv.astype(BF16)
    weights = (w_all, wuq, wukv, q_norm.reshape(1, -1), kv_norm.reshape(1, -1))

    def rows(width):
        return pl.BlockSpec((1, tm, width), lambda b, s: (b, s, 0))

    out_widths = (1024, 512, 2048, LANES, 1024, 1024, 1024, 2048)
    out_dtypes = (BF16, BF16, BF16, BF16, F32, F32, F32, BF16)
    return pl.pallas_call(
        _in0_kernel,
        grid=(bsz, seq // tm),
        in_specs=[rows(D_MODEL)] + [_const_spec(w.shape) for w in weights] + [rows(LANES)] * 4,
        out_specs=[rows(w) for w in out_widths],
        out_shape=[jax.ShapeDtypeStruct((bsz, seq, w), dt) for w, dt in zip(out_widths, out_dtypes)],
        compiler_params=_cparams(("parallel", "parallel")),
        name="in_proj0",
    )(x, *weights, c64, s64, c128, s128)


MLA_SUB = 256
MLA_ONES_ROWS = 16
MLA_AHEAD = 5
LOG2E = 1.4426950408889634


def _mla_kernel(qn_ref, qr_ref, kn_ref, kr_ref, v_ref, o_ref, k_s, vt_s):
    h = pl.program_id(1)

    @pl.when(pl.program_id(2) == 0)
    def _():
        k_s[:, :MLA_NOPE] = kn_ref[0]
        k_s[:, MLA_NOPE:] = kr_ref[0]
        vt_s[:MLA_V, :] = v_ref[0].astype(F32).T.astype(BF16)
        vt_s[MLA_V:, :] = jnp.ones((MLA_ONES_ROWS, vt_s.shape[1]), BF16)

    lane = lax.broadcasted_iota(jnp.int32, (MLA_SUB, LANES), 1)
    mine = (lane // MLA_ROPE) == (h % 2)
    nsub = qn_ref.shape[1] // MLA_SUB

    def scores(i):
        rows = slice(i * MLA_SUB, (i + 1) * MLA_SUB)
        qr = jnp.where(mine, qr_ref[0, rows, :], jnp.zeros((MLA_SUB, LANES), BF16))
        return _dot_nt(k_s[...], jnp.concatenate([qn_ref[0, rows, :], qr], axis=1))

    ahead = [scores(i) for i in range(min(MLA_AHEAD, nsub))]
    for i in range(nsub):
        s = ahead.pop(0)
        if i + MLA_AHEAD < nsub:
            ahead.append(scores(i + MLA_AHEAD))
        m = jnp.max(s, axis=0, keepdims=True)
        p = jnp.exp2(s - m).astype(BF16)
        ot = _dot(vt_s[...], p)
        o = ot[:MLA_V] / ot[MLA_V:MLA_V + 1]
        o_ref[0, i * MLA_SUB:(i + 1) * MLA_SUB, :] = o.T.astype(o_ref.dtype)


def _mla(qn, qr, kv, kr, tq=2048):
    bsz, seq, _ = qn.shape
    return pl.pallas_call(
        _mla_kernel,
        grid=(bsz, MLA_HEADS, seq // tq),
        in_specs=[
            pl.BlockSpec((1, tq, LANES), lambda b, h, i: (b, i, h)),
            pl.BlockSpec((1, tq, LANES), lambda b, h, i: (b, i, h // 2)),
            pl.BlockSpec((1, seq, LANES), lambda b, h, i: (b, 0, 2 * h)),
            pl.BlockSpec((1, seq, LANES), lambda b, h, i: (b, 0, 0)),
            pl.BlockSpec((1, seq, LANES), lambda b, h, i: (b, 0, 2 * h + 1)),
        ],
        out_specs=pl.BlockSpec((1, tq, LANES), lambda b, h, i: (b, i, h)),
        out_shape=jax.ShapeDtypeStruct((bsz, seq, MLA_HEADS * MLA_V), BF16),
        scratch_shapes=[pltpu.VMEM((seq, MLA_NOPE + LANES), BF16), pltpu.VMEM((MLA_V + MLA_ONES_ROWS, seq), BF16)],
        compiler_params=_cparams(("parallel", "parallel", "arbitrary")),
        name="mla_attention",
    )(qn, qr, kv, kr, kv)


DIL_TQ = 128
DIL_RADIUS = 64
DIL_TILES_PER_STEP = 8


def _dil_bias_table():
    i = np.arange(DIL_TQ)[:, None]
    j = np.arange(2 * DIL_TQ)[None, :]
    tabs = [np.where(np.abs(j - off - i) <= DIL_RADIUS, 0.0, -np.inf) for off in (0, DIL_RADIUS, 2 * DIL_RADIUS)]
    return jnp.asarray(np.stack(tabs), F32)


def _dil_kernel(q_ref, k_ref, v_ref, bias_ref, o_ref, qd_s, kd_s, vd_s, m_s, l_s, acc_s):
    seq = q_ref.shape[1]
    vd_s[:, :, DIL_HD:] = jnp.ones((len(DILATED_GROUPS), seq, DIL_HD), BF16)

    def rows_of(start, size, dil):
        return pl.ds(start, size) if dil == 1 else pl.ds(start, size, stride=dil)

    for g, (_, dil) in enumerate(DILATED_GROUPS):
        length = seq // dil
        for src, dst in ((q_ref, qd_s), (k_ref, kd_s), (v_ref, vd_s)):
            for r in range(dil):
                dst[g, r * length:(r + 1) * length, :DIL_HD] = src[0, rows_of(r, length, dil), :].astype(BF16)

    def tiles_step(g, dil, which, order):
        length = seq // dil
        tiles = length // DIL_TQ
        kw = min(length, DIL_TQ + 2 * DIL_RADIUS)
        n = len(which)
        ws = [jnp.clip(t * DIL_TQ - DIL_RADIUS, 0, length - kw) for _, t in which]
        qrows = [pl.ds(pl.multiple_of(r * length + t * DIL_TQ, DIL_TQ), DIL_TQ) for r, t in which]
        krows = [pl.ds(pl.multiple_of(r * length + ws[i], DIL_RADIUS), kw) for i, (r, _) in enumerate(which)]
        if dil == 1:
            out_rows = qrows
        else:
            out_rows = [rows_of(r + t * (DIL_TQ * dil), DIL_TQ, dil) for r, t in which]
        if tiles == 1:
            bias = [bias_ref[0, :, :kw] for _ in which]
        else:
            bias = [bias_ref[jnp.where(t == 0, 0, jnp.where(t == tiles - 1, 2, 1))] for _, t in which]
        s = [_dot_nt(qd_s[g, qrows[i], :], kd_s[g, krows[i], :]) + bias[i] for i in range(n)]
        mk = [jnp.max(t, axis=-1, keepdims=True) for t in s]
        m = [jnp.broadcast_to(t, (DIL_TQ, DIL_HD)) for t in mk]
        p = [jnp.exp2(s[i] - mk[i]).astype(BF16) for i in range(n)]
        ol = [_dot(p[i], vd_s[g, krows[i], :]) for i in range(n)]
        o = [t[:, :DIL_HD] for t in ol]
        l = [t[:, DIL_HD:] for t in ol]
        for i in range(n):
            if order == 0:
                m_s[out_rows[i], :] = m[i]
                l_s[out_rows[i], :] = l[i]
                acc_s[out_rows[i], :] = o[i]
                continue
            m_old = m_s[out_rows[i], :]
            m_new = jnp.maximum(m_old, m[i])
            a_old = jnp.exp2(m_old - m_new)
            a_new = jnp.exp2(m[i] - m_new)
            l_new = a_old * l_s[out_rows[i], :] + a_new * l[i]
            acc_new = a_old * acc_s[out_rows[i], :] + a_new * o[i]
            if order == len(DILATED_GROUPS) - 1:
                o_ref[0, out_rows[i], :] = (acc_new / l_new).astype(o_ref.dtype)
            else:
                m_s[out_rows[i], :] = m_new
                l_s[out_rows[i], :] = l_new
                acc_s[out_rows[i], :] = acc_new

    for order, g in enumerate(reversed(range(len(DILATED_GROUPS)))):
        window, dil = DILATED_GROUPS[g]
        assert window // (2 * dil) == DIL_RADIUS
        assert order < len(DILATED_GROUPS) - 1 or dil == 1
        tiles = seq // dil // DIL_TQ

        def body(it, carry, g=g, dil=dil, tiles=tiles, order=order):
            idx = [it * DIL_TILES_PER_STEP + j for j in range(DIL_TILES_PER_STEP)]
            tiles_step(g, dil, [(i // tiles, i % tiles) for i in idx], order)
            return carry

        lax.fori_loop(0, dil * tiles // DIL_TILES_PER_STEP, body, 0)


def _dilated(qb, kb, vb):
    bsz, seq, _ = qb.shape
    spec = pl.BlockSpec((1, seq, DIL_HD), lambda b, h: (b, 0, h))
    bias = _dil_bias_table()
    grouped = pltpu.VMEM((len(DILATED_GROUPS), seq, DIL_HD), BF16)
    stat = pltpu.VMEM((seq, DIL_HD), F32)
    return pl.pallas_call(
        _dil_kernel,
        grid=(bsz, DIL_HEADS),
        in_specs=[spec] * 3 + [_const_spec(bias.shape)],
        out_specs=spec,
        out_shape=jax.ShapeDtypeStruct((bsz, seq, DIL_HEADS * DIL_HD), BF16),
        scratch_shapes=[grouped, grouped, pltpu.VMEM((len(DILATED_GROUPS), seq, 2 * DIL_HD), BF16), stat, stat, stat],
        compiler_params=_cparams(("parallel", "parallel")),
        name="dilated_attention",
    )(qb, kb, vb, bias)


OUT_SUB = 128


def _out_kernel(o1_ref, o2_ref, g_ref, x_ref, w_ref, lnw_ref, lnb_ref, out_ref):
    half = o1_ref.shape[-1]
    for i in range(o1_ref.shape[1] // OUT_SUB):
        rows = slice(i * OUT_SUB, (i + 1) * OUT_SUB)
        y1 = o1_ref[0, rows, :] * g_ref[0, rows, :half]
        y2 = o2_ref[0, rows, :] * g_ref[0, rows, half:]
        r = DEEPNORM_ALPHA * x_ref[0, rows, :] + _dot(y1, w_ref[:half, :]) + _dot(y2, w_ref[half:, :])
        mu = jnp.mean(r, axis=-1, keepdims=True)
        rc = r - mu
        var = jnp.mean(rc * rc, axis=-1, keepdims=True)
        out_ref[0, rows, :] = rc * lax.rsqrt(var + LN_EPS) * lnw_ref[...] + lnb_ref[...]


def _out_proj(o1, o2, gate, x, w_out, ln_w, ln_b, tm=512):
    bsz, seq, _ = x.shape
    half = o1.shape[-1]

    def rows(width):
        return pl.BlockSpec((1, tm, width), lambda b, s: (b, s, 0))

    w = w_out.astype(BF16)
    return pl.pallas_call(
        _out_kernel,
        grid=(bsz, seq // tm),
        in_specs=[rows(half), rows(half), rows(2 * half), rows(D_MODEL), _const_spec(w.shape),
                  _const_spec((1, D_MODEL)), _const_spec((1, D_MODEL))],
        out_specs=rows(D_MODEL),
        out_shape=jax.ShapeDtypeStruct((bsz, seq, D_MODEL), F32),
        compiler_params=_cparams(("parallel", "parallel")),
        name="out_proj",
    )(o1, o2, gate, x, w, ln_w.reshape(1, -1), ln_b.reshape(1, -1))


def _layer0(x, tabs, w_in, q_norm, w_uq, kv_norm, w_ukv, w_out, ln_w, ln_b):
    qn, qr, kv, kr, qb, kb, vb, gate = _in0(x, tabs, w_in, q_norm, w_uq, kv_norm, w_ukv)
    o_a = _mla(qn, qr, kv, kr)
    o_b = _dilated(qb, kb, vb)
    return _out_proj(o_a, o_b, gate, x, w_out, ln_w, ln_b)


GATE_BETA = 0
GATE_GC = 2 * GDN_HEADS


def _softplus(x):
    return jnp.maximum(x, 0.0) + jnp.log1p(jnp.exp(-jnp.abs(x)))


IN1_COLS = np.cumsum((0, 2 * GDN_HEADS * GDN_DK + GDN_HEADS * GDN_DV, LANES, 2 * RET_HEADS * RET_DK,
                      RET_HEADS * RET_DV, GDN_HEADS * GDN_DV + RET_HEADS * RET_DV))


def _in1_kernel(x_ref, w_ref, gp_ref, c64_ref, s64_ref,
                qkvc_ref, gate_ref, qr_ref, kr_ref, vr_ref, g_ref):
    x = x_ref[0].astype(BF16)
    tm = x.shape[0]
    c64, s64 = c64_ref[0], s64_ref[0]

    def proj(i):
        return _dot(x, w_ref[:, IN1_COLS[i]:IN1_COLS[i + 1]])

    qkvc_ref[0] = proj(0).astype(qkvc_ref.dtype)
    hg = proj(1)
    beta = jax.nn.sigmoid(hg)
    g = -jnp.exp(gp_ref[0:1, :]) * _softplus(hg + gp_ref[1:2, :])
    pos = lax.broadcasted_iota(jnp.int32, hg.shape, 0) % GDN_CHUNK
    pre, suf = g, g
    step = 1
    while step < GDN_CHUNK:
        pre = pre + jnp.where(pos >= step, pltpu.roll(pre, step, 0), 0.0)
        suf = suf + jnp.where(pos < GDN_CHUNK - step, pltpu.roll(suf, tm - step, 0), 0.0)
        step *= 2
    lane = lax.broadcasted_iota(jnp.int32, hg.shape, 1)
    gc = jnp.where(lane < GATE_GC + GDN_HEADS, pre, suf)
    gate_ref[0] = jnp.where(lane < GATE_GC, beta, gc)
    hqk = proj(2)
    width = RET_HEADS * RET_DK
    for j in range(width // LANES):
        qr_ref[0, :, j * LANES:(j + 1) * LANES] = _rope_pairs64(hqk[:, j * LANES:(j + 1) * LANES], c64, s64).astype(BF16)
        kblk = hqk[:, width + j * LANES:width + (j + 1) * LANES]
        kr_ref[0, :, j * LANES:(j + 1) * LANES] = (_rope_pairs64(kblk, c64, s64) * (RET_DK ** -0.5)).astype(BF16)
    vr_ref[0] = proj(3).astype(BF16)
    g_ref[0] = _silu(proj(4)).astype(BF16)


def _in1(x, tabs, w_in, a_log, dt_bias, tm=512):
    bsz, seq, _ = x.shape
    c64, s64, _, _ = tabs
    conv_ch = 2 * GDN_HEADS * GDN_DK + GDN_HEADS * GDN_DV
    gate_end = conv_ch + 4 * GDN_HEADS
    w_all = _splice_columns(w_in, gate_end, gate_end, LANES - 4 * GDN_HEADS)
    pad = (GATE_GC, LANES - GATE_GC - 2 * GDN_HEADS)
    gp = jnp.stack([jnp.pad(a_log.reshape(-1), pad), jnp.pad(dt_bias.reshape(-1), pad)]).astype(F32)
    weights = (w_all, gp)

    def rows(width):
        return pl.BlockSpec((1, tm, width), lambda b, s: (b, s, 0))

    out_widths = (conv_ch, LANES, 512, 512, 1024, 2048)
    out_dtypes = (BF16, F32, BF16, BF16, BF16, BF16)
    return pl.pallas_call(
        _in1_kernel,
        grid=(bsz, seq // tm),
        in_specs=[rows(D_MODEL)] + [_const_spec(w.shape) for w in weights] + [rows(LANES)] * 2,
        out_specs=[rows(w) for w in out_widths],
        out_shape=[jax.ShapeDtypeStruct((bsz, seq, w), dt) for w, dt in zip(out_widths, out_dtypes)],
        compiler_params=_cparams(("parallel", "parallel")),
        name="in_proj1",
    )(x, *weights, c64, s64)


CONV_PAD = 8
GDN_P1_CHUNKS = 8
GDN_P3_CHUNKS = 8
GDN_LEVELS = 7


def _gdn_level_masks():
    i = np.arange(GDN_CHUNK)[:, None]
    j = np.arange(GDN_CHUNK)[None, :]
    masks = [((i >> (k + 1)) == (j >> (k + 1))) & ((i >> k) != (j >> k)) for k in range(GDN_LEVELS)]
    return jnp.asarray(np.stack(masks), F32)


def _gdn_kernel(qraw_ref, kraw_ref, vraw_ref, wq_ref, wk_ref, wv_ref, gcol_ref, grow_ref, norm_ref, lvl_ref, o_ref,
                xpad_s, q_s, k_s, v_s, bcb_s, gcb_s, w_s, u_s, att_s, mt_s, bt_s, st_s):
    seq = q_s.shape[0]
    nchunk = seq // GDN_CHUNK

    zeros = jnp.zeros((CONV_PAD, LANES), F32)
    xpad_s[0:CONV_PAD, :] = zeros
    xpad_s[CONV_PAD + seq:, :] = zeros

    def conv_silu(raw_ref, w_ref):
        xpad_s[CONV_PAD:CONV_PAD + seq, :] = raw_ref[0]
        acc = None
        for j in range(CONV_K):
            off = CONV_PAD - (CONV_K - 1) // 2 + j
            term = xpad_s[off:off + seq, :] * w_ref[j:j + 1, :]
            acc = term if acc is None else acc + term
        return _silu(acc)

    def l2norm(t):
        return t * lax.rsqrt(jnp.sum(t * t, axis=-1, keepdims=True) + L2_EPS)

    q_s[...] = l2norm(conv_silu(qraw_ref, wq_ref)) * (GDN_DK ** -0.5)
    k_s[...] = l2norm(conv_silu(kraw_ref, wk_ref))
    v_s[...] = conv_silu(vraw_ref, wv_ref)

    for d in range(2):
        bcb_s[d] = jnp.broadcast_to(gcol_ref[0, 0, d:d + 1, :], (LANES, seq)).T
        gcb_s[d] = jnp.broadcast_to(gcol_ref[0, 0, 2 + d:3 + d, :], (LANES, seq)).T

    ii = lax.broadcasted_iota(jnp.int32, (GDN_CHUNK, GDN_CHUNK), 0)
    jj = lax.broadcasted_iota(jnp.int32, (GDN_CHUNK, GDN_CHUNK), 1)
    causal = (ii >= jj, ii <= jj)
    strict = (ii > jj, ii < jj)

    def chunk_rows(c):
        base = pl.multiple_of(c * GDN_CHUNK, GDN_CHUNK)
        return base, pl.ds(base, GDN_CHUNK)

    def chunk_log_decay(d, base):
        return gcb_s[d, pl.ds(base + (GDN_CHUNK - 1 if d == 0 else 0), 1), :]

    def phase1(it, carry):
        chains = [(it * GDN_P1_CHUNKS + j, d) for j in range(GDN_P1_CHUNKS) for d in range(2)]
        rows = [chunk_rows(c) for c, _ in chains]
        kf = [k_s[r, :] for _, r in rows]
        kc = [t.astype(BF16) for t in kf]
        gcb = [gcb_s[d, r, :] for (_, d), (_, r) in zip(chains, rows)]
        bcb = [bcb_s[d, r, :] for (_, d), (_, r) in zip(chains, rows)]
        kb = [kf[i] * bcb[i] for i in range(len(chains))]
        kk = [_dot_nt(kb[i].astype(BF16), kc[i]) for i in range(len(chains))]
        decay = [jnp.exp(jnp.where(causal[d], gcb[i] - grow_ref[0, 0, d, pl.ds(c, 1), :], -jnp.inf))
                 for i, (c, d) in enumerate(chains)]
        a = [jnp.where(strict[d], kk[i] * decay[i], 0.0) for i, (_, d) in enumerate(chains)]
        n = [-(t * lvl_ref[0]) for t in a]
        for lvl in range(1, GDN_LEVELS):
            ak = [t * lvl_ref[lvl] for t in a]
            y = [ak[i] + _dot(ak[i].astype(BF16), n[i].astype(BF16)) for i in range(len(chains))]
            n = [n[i] - y[i] - _dot(n[i].astype(BF16), y[i].astype(BF16)) for i in range(len(chains))]
        nb = [t.astype(BF16) for t in n]
        wpre = [kb[i] * jnp.exp(gcb[i]) for i in range(len(chains))]
        upre = [v_s[r, :] * bcb[i] for i, (_, r) in enumerate(rows)]
        w = [(wpre[i] + _dot(nb[i], wpre[i].astype(BF16))).astype(BF16) for i in range(len(chains))]
        u = [upre[i] + _dot(nb[i], upre[i].astype(BF16)) for i in range(len(chains))]
        qkc = [_dot_nt(q_s[rows[2 * j][1], :].astype(BF16), kc[2 * j]) for j in range(GDN_P1_CHUNKS)]
        qk = [qkc[i // 2] for i in range(len(chains))]
        kd = [(kf[i] * jnp.exp(chunk_log_decay(d, rows[i][0]) - gcb[i])).astype(BF16) for i, (_, d) in enumerate(chains)]
        mt = [_dot_tn(w[i], kd[i]) for i in range(len(chains))]
        bt = [_dot_tn(u[i].astype(BF16), kd[i]) for i in range(len(chains))]
        for i, (c, d) in enumerate(chains):
            r = rows[i][1]
            w_s[d, r, :] = w[i]
            u_s[d, r, :] = u[i]
            att_s[d, r, :] = (qk[i] * decay[i]).astype(BF16)
            mt_s[d, c] = (-mt[i]).astype(BF16)
            bt_s[d, c] = bt[i]
        return carry

    lax.fori_loop(0, nchunk // GDN_P1_CHUNKS, phase1, 0)

    def phase2(i, carry):
        new = []
        for d in range(2):
            c = i if d == 0 else nchunk - 1 - i
            st = carry[d]
            st_s[d, c] = st.astype(BF16)
            gl = chunk_log_decay(d, pl.multiple_of(c * GDN_CHUNK, GDN_CHUNK))
            new.append(st * jnp.exp(gl) + _dot(st.astype(BF16), mt_s[d, c]) + bt_s[d, c])
        return tuple(new)

    zero_state = jnp.zeros((GDN_DV, GDN_DK), F32)
    lax.fori_loop(0, nchunk, phase2, (zero_state, zero_state))

    def phase3(it, carry):
        chains = [(it * GDN_P3_CHUNKS + j, d) for j in range(GDN_P3_CHUNKS) for d in range(2)]
        rows = [chunk_rows(c)[1] for c, _ in chains]
        qg = [(q_s[r, :] * jnp.exp(gcb_s[d, r, :])).astype(BF16) for (_, d), r in zip(chains, rows)]
        m = [_dot_nt(jnp.concatenate([w_s[d, rows[i], :], qg[i]], axis=0), st_s[d, c]) for i, (c, d) in enumerate(chains)]
        vnew = [(u_s[d, rows[i], :] - m[i][:GDN_CHUNK]).astype(BF16) for i, (_, d) in enumerate(chains)]
        od = [m[i][GDN_CHUNK:] + _dot(att_s[d, rows[i], :], vnew[i]) for i, (_, d) in enumerate(chains)]
        for j in range(GDN_P3_CHUNKS):
            o = od[2 * j] + od[2 * j + 1]
            o_ref[0, rows[2 * j], :] = (o * lax.rsqrt(jnp.mean(o * o, axis=-1, keepdims=True) + RMS_EPS)
                                        * norm_ref[...]).astype(o_ref.dtype)
        return carry

    lax.fori_loop(0, nchunk // GDN_P3_CHUNKS, phase3, 0)


def _gdn(qkvc, conv_w, gcol, grow, gdn_norm):
    bsz, seq, _ = qkvc.shape
    nh = GDN_HEADS

    def col(off):
        return pl.BlockSpec((1, seq, LANES), lambda b, h: (b, 0, off + h))

    def wcol(off):
        return pl.BlockSpec((CONV_K, LANES), lambda b, h: (0, off + h))

    nchunk = seq // GDN_CHUNK
    big = pltpu.VMEM((seq, LANES), F32)

    def per_dir(dtype):
        return pltpu.VMEM((2, seq, LANES), dtype)

    def per_chunk(dtype):
        return pltpu.VMEM((2, nchunk, GDN_DV, GDN_DK), dtype)

    return pl.pallas_call(
        _gdn_kernel,
        grid=(bsz, nh),
        in_specs=[col(0), col(nh), col(2 * nh), wcol(0), wcol(nh), wcol(2 * nh),
                  pl.BlockSpec((1, 1, 4, seq), lambda b, h: (b, h, 0, 0)),
                  pl.BlockSpec((1, 1, 2, nchunk, GDN_CHUNK), lambda b, h: (b, h, 0, 0, 0)),
                  _const_spec((1, GDN_DV)), _const_spec((GDN_LEVELS, GDN_CHUNK, GDN_CHUNK))],
        out_specs=pl.BlockSpec((1, seq, GDN_DV), lambda b, h: (b, 0, h)),
        out_shape=jax.ShapeDtypeStruct((bsz, seq, nh * GDN_DV), BF16),
        scratch_shapes=[pltpu.VMEM((seq + 2 * CONV_PAD, LANES), F32), big, big, big,
                        per_dir(F32), per_dir(F32), per_dir(BF16), per_dir(F32), per_dir(BF16),
                        per_chunk(BF16), per_chunk(F32), per_chunk(BF16)],
        compiler_params=_cparams(("parallel", "parallel")),
        name="gated_deltanet",
    )(qkvc, qkvc, qkvc, conv_w, conv_w, conv_w, gcol, grow, gdn_norm.reshape(1, -1), _gdn_level_masks())


GDN_HPS = 1
CONV_ROWS = 256
G_BETA, G_GC = 0, 2


def _gdn2_kernel(qraw_ref, kraw_ref, vraw_ref, wq_ref, wk_ref, wv_ref, grow_ref, norm_ref, lvl_ref, o_ref,
                 xpad_s, q_s, k_s, v_s, w_s, u_s, att_s, qg_s, mt_s, bt_s, st_s, gl_s, o_s):
    seq = q_s.shape[1]
    nchunk = seq // GDN_CHUNK

    zeros = jnp.zeros((CONV_PAD, LANES), F32)
    xpad_s[0:CONV_PAD, :] = zeros
    xpad_s[CONV_PAD + seq:, :] = zeros

    def conv_silu(raw_ref, w_ref, lanes, dst, finish):
        xpad_s[CONV_PAD:CONV_PAD + seq, :] = raw_ref[0, :, lanes].astype(F32)
        for blk in range(seq // CONV_ROWS):
            acc = None
            for j in range(CONV_K):
                off = blk * CONV_ROWS + CONV_PAD - (CONV_K - 1) // 2 + j
                term = xpad_s[off:off + CONV_ROWS, :] * w_ref[j:j + 1, lanes]
                acc = term if acc is None else acc + term
            dst[blk * CONV_ROWS:(blk + 1) * CONV_ROWS, :] = finish(_silu(acc))

    def l2norm(t):
        return t * lax.rsqrt(jnp.sum(t * t, axis=-1, keepdims=True) + L2_EPS)

    for hh in range(GDN_HPS):
        lanes = slice(hh * LANES, (hh + 1) * LANES)
        conv_silu(qraw_ref, wq_ref, lanes, q_s.at[hh], lambda t: l2norm(t) * (GDN_DK ** -0.5))
        conv_silu(kraw_ref, wk_ref, lanes, k_s.at[hh], l2norm)
        conv_silu(vraw_ref, wv_ref, lanes, v_s.at[hh], lambda t: t)

    ii = lax.broadcasted_iota(jnp.int32, (GDN_CHUNK, GDN_CHUNK), 0)
    jj = lax.broadcasted_iota(jnp.int32, (GDN_CHUNK, GDN_CHUNK), 1)
    causal = (ii >= jj, ii <= jj)
    strict = (ii > jj, ii < jj)

    def chunk_rows(c):
        return pl.ds(pl.multiple_of(c * GDN_CHUNK, GDN_CHUNK), GDN_CHUNK)

    def token_major(row):
        return jnp.broadcast_to(row, (LANES, GDN_CHUNK)).T

    def phase1(idx, carry):
        hh = idx // (nchunk // GDN_P1_CHUNKS)
        it = idx % (nchunk // GDN_P1_CHUNKS)
        chains = [(it * GDN_P1_CHUNKS + j, d) for j in range(GDN_P1_CHUNKS) for d in range(2)]
        n_ch = len(chains)
        rows = [chunk_rows(c) for c, _ in chains]
        kf = [k_s[hh, r, :] for r in rows]
        kc = [t.astype(BF16) for t in kf]
        grow = [grow_ref[0, hh, G_GC + d, pl.ds(c, 1), :] for c, d in chains]
        gcb = [token_major(t) for t in grow]
        bcb = [token_major(grow_ref[0, hh, G_BETA + d, pl.ds(c, 1), :]) for c, d in chains]
        kb = [kf[i] * bcb[i] for i in range(n_ch)]
        kk = [_dot_nt(kb[i].astype(BF16), kc[i]) for i in range(n_ch)]
        decay = [jnp.exp(jnp.where(causal[d], gcb[i] - grow[i], -jnp.inf)) for i, (_, d) in enumerate(chains)]
        a = [jnp.where(strict[d], kk[i] * decay[i], 0.0) for i, (_, d) in enumerate(chains)]
        n = [-(t * lvl_ref[0]) for t in a]
        for lvl in range(1, GDN_LEVELS):
            ak = [t * lvl_ref[lvl] for t in a]
            y = [ak[i] + _dot(ak[i].astype(BF16), n[i].astype(BF16)) for i in range(n_ch)]
            n = [n[i] - y[i] - _dot(n[i].astype(BF16), y[i].astype(BF16)) for i in range(n_ch)]
        nb = [t.astype(BF16) for t in n]
        egc = [jnp.exp(t) for t in gcb]
        wpre = [kb[i] * egc[i] for i in range(n_ch)]
        upre = [v_s[hh, rows[i], :] * bcb[i] for i in range(n_ch)]
        w = [(wpre[i] + _dot(nb[i], wpre[i].astype(BF16))).astype(BF16) for i in range(n_ch)]
        u = [upre[i] + _dot(nb[i], upre[i].astype(BF16)) for i in range(n_ch)]
        qf = [q_s[hh, rows[2 * j], :] for j in range(GDN_P1_CHUNKS)]
        qkc = [_dot_nt(qf[j].astype(BF16), kc[2 * j]) for j in range(GDN_P1_CHUNKS)]
        gl = [gcb[i][GDN_CHUNK - 1:GDN_CHUNK, :] if d == 0 else gcb[i][0:1, :] for i, (_, d) in enumerate(chains)]
        kd = [(kf[i] * jnp.exp(gl[i] - gcb[i])).astype(BF16) for i in range(n_ch)]
        mt = [_dot_tn(w[i], kd[i]) for i in range(n_ch)]
        bt = [_dot_tn(u[i].astype(BF16), kd[i]) for i in range(n_ch)]
        for i, (c, d) in enumerate(chains):
            r = rows[i]
            w_s[hh, d, r, :] = w[i]
            u_s[hh, d, r, :] = u[i]
            att_s[hh, d, r, :] = (qkc[i // 2] * decay[i]).astype(BF16)
            qg_s[hh, d, r, :] = (qf[i // 2] * egc[i]).astype(BF16)
            mt_s[hh, d, c] = (-mt[i]).astype(BF16)
            bt_s[hh, d, c] = bt[i]
            gl_s[hh, d, c] = jnp.broadcast_to(jnp.exp(gl[i]), (8, LANES))
        return carry

    lax.fori_loop(0, GDN_HPS * (nchunk // GDN_P1_CHUNKS), phase1, 0)

    def phase2(i, carry):
        new = []
        for hh in range(GDN_HPS):
            for d in range(2):
                c = i if d == 0 else nchunk - 1 - i
                st = carry[2 * hh + d]
                st_s[hh, d, c] = st.astype(BF16)
                new.append(st * gl_s[hh, d, c, 0:1, :] + _dot(st.astype(BF16), mt_s[hh, d, c]) + bt_s[hh, d, c])
        return tuple(new)

    zero_state = jnp.zeros((GDN_DV, GDN_DK), F32)
    lax.fori_loop(0, nchunk, phase2, (zero_state,) * (2 * GDN_HPS))

    def phase3(idx, carry):
        hh = idx // (nchunk // GDN_P3_CHUNKS)
        it = idx % (nchunk // GDN_P3_CHUNKS)
        chains = [(it * GDN_P3_CHUNKS + j, d) for j in range(GDN_P3_CHUNKS) for d in range(2)]
        rows = [chunk_rows(c) for c, _ in chains]
        m = [_dot_nt(jnp.concatenate([w_s[hh, d, rows[i], :], qg_s[hh, d, rows[i], :]], axis=0), st_s[hh, d, c])
             for i, (c, d) in enumerate(chains)]
        vnew = [(u_s[hh, d, rows[i], :] - m[i][:GDN_CHUNK]).astype(BF16) for i, (_, d) in enumerate(chains)]
        od = [m[i][GDN_CHUNK:] + _dot(att_s[hh, d, rows[i], :], vnew[i]) for i, (_, d) in enumerate(chains)]
        for j in range(GDN_P3_CHUNKS):
            o = od[2 * j] + od[2 * j + 1]
            o_s[hh, rows[2 * j], :] = (o * lax.rsqrt(jnp.mean(o * o, axis=-1, keepdims=True) + RMS_EPS)
                                       * norm_ref[...]).astype(o_s.dtype)
        return carry

    lax.fori_loop(0, GDN_HPS * (nchunk // GDN_P3_CHUNKS), phase3, 0)
    for hh in range(GDN_HPS):
        o_ref[0, :, hh * LANES:(hh + 1) * LANES] = o_s[hh]


def _gdn2(qkvc, conv_w, grow, gdn_norm):
    bsz, seq, _ = qkvc.shape
    groups = GDN_HEADS // GDN_HPS
    width = GDN_HPS * LANES
    nchunk = seq // GDN_CHUNK

    def col(off):
        return pl.BlockSpec((1, seq, width), lambda b, g: (b, 0, off + g))

    def wcol(off):
        return pl.BlockSpec((CONV_K, width), lambda b, g: (0, off + g))

    def per_head(shape, dtype):
        return pltpu.VMEM((GDN_HPS,) + shape, dtype)

    return pl.pallas_call(
        _gdn2_kernel,
        grid=(bsz, groups),
        in_specs=[col(0), col(groups), col(2 * groups), wcol(0), wcol(groups), wcol(2 * groups),
                  pl.BlockSpec((1, GDN_HPS, 4, nchunk, GDN_CHUNK), lambda b, g: (b, g, 0, 0, 0)),
                  _const_spec((1, GDN_DV)), _const_spec((GDN_LEVELS, GDN_CHUNK, GDN_CHUNK))],
        out_specs=pl.BlockSpec((1, seq, width), lambda b, g: (b, 0, g)),
        out_shape=jax.ShapeDtypeStruct((bsz, seq, GDN_HEADS * GDN_DV), BF16),
        scratch_shapes=[pltpu.VMEM((seq + 2 * CONV_PAD, LANES), F32),
                        per_head((seq, LANES), F32), per_head((seq, LANES), F32), per_head((seq, LANES), F32),
                        per_head((2, seq, LANES), BF16), per_head((2, seq, LANES), F32),
                        per_head((2, seq, LANES), BF16), per_head((2, seq, LANES), BF16),
                        per_head((2, nchunk, GDN_DV, GDN_DK), BF16), per_head((2, nchunk, GDN_DV, GDN_DK), F32),
                        per_head((2, nchunk, GDN_DV, GDN_DK), BF16), per_head((2, nchunk, 8, LANES), F32),
                        per_head((seq, LANES), BF16)],
        compiler_params=_cparams(("parallel", "parallel")),
        name="gated_deltanet",
    )(qkvc, qkvc, qkvc, conv_w, conv_w, conv_w, grow, gdn_norm.reshape(1, -1), _gdn_level_masks())


def _ret_tables():
    heads = np.arange(RET_HEADS, dtype=np.float64)
    log_gamma = np.log1p(-(2.0 ** (-RET_DECAY_BASE - heads)))[:, None]
    idx = np.arange(RET_CHUNK, dtype=np.float64)
    dmat = np.exp(log_gamma[:, :, None] * np.abs(idx[:, None] - idx[None, :])[None])
    xi = np.exp(log_gamma * (idx + 1.0))
    zeta = np.exp(log_gamma * (RET_CHUNK - 1.0 - idx))
    gch = np.exp(log_gamma * RET_CHUNK)

    def cb(t):
        return jnp.asarray(np.broadcast_to(t[:, :, None], t.shape + (LANES,)), F32)

    return jnp.asarray(dmat, F32), cb(xi), cb(zeta), cb(gch)


def _ret_kernel(q_ref, k_ref, v_ref, d_ref, xi_ref, zeta_ref, gch_ref, nw_ref, nb_ref, o_ref, kvf_s, kvb_s):
    seq = q_ref.shape[1]
    nchunk = seq // RET_CHUNK
    h = pl.program_id(1)
    xi, zeta, gch = xi_ref[0], zeta_ref[0], gch_ref[0]
    lane = lax.broadcasted_iota(jnp.int32, (RET_CHUNK, LANES), 1)
    mine = (lane // RET_DK) == (h % 2)

    def rows(c):
        return slice(c * RET_CHUNK, (c + 1) * RET_CHUNK)

    for c in range(nchunk):
        kc = k_ref[0, rows(c), :].astype(F32)
        vc = v_ref[0, rows(c), :]
        kvf_s[c] = _dot_tn((kc * zeta).astype(BF16), vc)
        kvb_s[c] = _dot_tn((kc * xi).astype(BF16), vc)
    state = jnp.zeros((LANES, RET_DV), F32)
    for c in range(nchunk):
        nxt = state * gch + kvf_s[c]
        kvf_s[c] = state
        state = nxt
    state = jnp.zeros((LANES, RET_DV), F32)
    for c in reversed(range(nchunk)):
        nxt = state * gch + kvb_s[c]
        kvb_s[c] = state
        state = nxt
    for c in range(nchunk):
        q = jnp.where(mine, q_ref[0, rows(c), :], jnp.zeros((RET_CHUNK, LANES), BF16))
        qf = q.astype(F32)
        s = _dot_nt(q, k_ref[0, rows(c), :]) * d_ref[0]
        o = _dot(s.astype(BF16), v_ref[0, rows(c), :])
        o = o + _dot((qf * xi).astype(BF16), kvf_s[c].astype(BF16))
        o = o + _dot((qf * zeta).astype(BF16), kvb_s[c].astype(BF16))
        mu = jnp.mean(o, axis=-1, keepdims=True)
        oc = o - mu
        var = jnp.mean(oc * oc, axis=-1, keepdims=True)
        o_ref[0, rows(c), :] = (oc * lax.rsqrt(var + LN_EPS) * nw_ref[...] + nb_ref[...]).astype(o_ref.dtype)


def _retention(qr, kr, vr, norm_w, norm_b):
    bsz, seq, _ = vr.shape
    dmat, xi, zeta, gch = _ret_tables()
    pair = pl.BlockSpec((1, seq, LANES), lambda b, h: (b, 0, h // 2))
    head = pl.BlockSpec((1, seq, RET_DV), lambda b, h: (b, 0, h))

    def per_head(shape):
        return pl.BlockSpec((1,) + shape, lambda b, h: (h, 0, 0))

    vec = pl.BlockSpec((1, RET_DV), lambda b, h: (0, h))
    state = pltpu.VMEM((seq // RET_CHUNK, LANES, RET_DV), F32)
    return pl.pallas_call(
        _ret_kernel,
        grid=(bsz, RET_HEADS),
        in_specs=[pair, pair, head, per_head((RET_CHUNK, RET_CHUNK)), per_head((RET_CHUNK, LANES)),
                  per_head((RET_CHUNK, LANES)), per_head((1, LANES)), vec, vec],
        out_specs=head,
        out_shape=jax.ShapeDtypeStruct((bsz, seq, RET_HEADS * RET_DV), BF16),
        scratch_shapes=[state, state],
        compiler_params=_cparams(("parallel", "parallel")),
        name="retention",
    )(qr, kr, vr, dmat, xi, zeta, gch, norm_w.reshape(1, -1), norm_b.reshape(1, -1))


def _layer1(x, tabs, w_in, conv_w, a_log, dt_bias, gdn_norm, ret_norm_w, ret_norm_b, w_out, ln_w, ln_b):
    bsz, seq, _ = x.shape
    qkvc, gates, qr, kr, vr, gate = _in1(x, tabs, w_in, a_log, dt_bias)
    gsl = gates[:, :, :4 * GDN_HEADS].reshape(bsz, seq, 4, GDN_HEADS)
    grow = gsl.transpose(0, 3, 2, 1).reshape(bsz, GDN_HEADS, 4, seq // GDN_CHUNK, GDN_CHUNK)
    o_c = _gdn2(qkvc, conv_w, grow, gdn_norm)
    o_d = _retention(qr, kr, vr, ret_norm_w, ret_norm_b)
    return _out_proj(o_c, o_d, gate, x, w_out, ln_w, ln_b)


def kernel(x, positions, even_w_in, even_q_norm, even_w_uq, even_kv_norm, even_w_ukv, even_w_out, even_ln_w, even_ln_b, odd_w_in, odd_conv_w, odd_a_log, odd_dt_bias, odd_gdn_norm, odd_ret_norm_w, odd_ret_norm_b, odd_w_out, odd_ln_w, odd_ln_b):
    tabs = _rope_tables(positions)
    x = _layer0(x, tabs, even_w_in[0], even_q_norm[0], even_w_uq[0], even_kv_norm[0], even_w_ukv[0],
                even_w_out[0], even_ln_w[0], even_ln_b[0])
    x = _layer1(x, tabs, odd_w_in[0], odd_conv_w[0], odd_a_log[0], odd_dt_bias[0], odd_gdn_norm[0],
                odd_ret_norm_w[0], odd_ret_norm_b[0], odd_w_out[0], odd_ln_w[0], odd_ln_b[0])
    return x
```

```python
import functools

import numpy as np
import jax
import jax.numpy as jnp
from jax import lax
from jax.experimental import pallas as pl
from jax.experimental.pallas import tpu as pltpu

F32 = jnp.float32
BF16 = jnp.bfloat16

D_MODEL = 1024
DEPTH = 2
DEEPNORM_ALPHA = (2 * DEPTH) ** 0.25
ROPE_THETA = 10000.0
RMS_EPS = 1e-6
LN_EPS = 1e-5
L2_EPS = 1e-6

MLA_HEADS = 8
MLA_NOPE = 128
MLA_ROPE = 64
MLA_V = 128
Q_LORA = 768
KV_LORA = 256

DIL_HEADS = 8
DIL_HD = 128
DILATED_GROUPS = ((128, 1), (512, 4), (2048, 16))

GDN_HEADS = 8
GDN_DK = 128
GDN_DV = 128
CONV_K = 5
GDN_CHUNK = 128

RET_HEADS = 8
RET_DK = 64
RET_DV = 128
RET_DECAY_BASE = 5.0
RET_CHUNK = 256

LANES = 128
VMEM_LIMIT = 56 * 1024 * 1024


def _cparams(sem):
    return pltpu.CompilerParams(dimension_semantics=sem, vmem_limit_bytes=VMEM_LIMIT)


def _dot(a, b):
    return jnp.dot(a, b, preferred_element_type=F32)


def _dot_nt(a, b):
    return lax.dot_general(a, b, (((1,), (1,)), ((), ())), preferred_element_type=F32)


def _dot_tn(a, b):
    return lax.dot_general(a, b, (((0,), (0,)), ((), ())), preferred_element_type=F32)


def _const_spec(shape):
    nd = len(shape)
    return pl.BlockSpec(shape, lambda *_: (0,) * nd, pipeline_mode=pl.Buffered(1))


def _rope_pairs64(x, cos, sin_signed):
    lane = lax.broadcasted_iota(jnp.int32, x.shape, 1)
    swapped = jnp.where((lane % 64) < 32, pltpu.roll(x, 96, 1), pltpu.roll(x, 32, 1))
    return x * cos + swapped * sin_signed


def _rope_full128(x, cos, sin_signed):
    return x * cos + pltpu.roll(x, 64, 1) * sin_signed


def _splice_kernel(w_ref, o_ref, *, head_end, tail_start, gap):
    w = w_ref[0]
    parts = [w[:, :head_end], w[:, tail_start:]]
    if gap:
        parts.insert(1, jnp.zeros((w.shape[0], gap), w.dtype))
    o_ref[...] = jnp.concatenate(parts, axis=1).astype(o_ref.dtype)


def _splice_columns(w, head_end, tail_start, gap, tr=128):
    _, rows, cols = w.shape
    total = head_end + gap + cols - tail_start
    return pl.pallas_call(
        functools.partial(_splice_kernel, head_end=head_end, tail_start=tail_start, gap=gap),
        grid=(rows // tr,),
        in_specs=[pl.BlockSpec((1, tr, cols), lambda i: (0, i, 0))],
        out_specs=pl.BlockSpec((tr, total), lambda i: (i, 0)),
        out_shape=jax.ShapeDtypeStruct((rows, total), BF16),
        compiler_params=_cparams(("parallel",)),
        name="regroup_weight",
    )(w)


def _silu(x):
    return x * jax.nn.sigmoid(x)


def _rope_tables_kernel(pos_ref, inv_ref, sgn_ref, c64_ref, s64_ref, c128_ref, s128_ref):
    ang = pos_ref[0] * inv_ref[...]
    cos, sin = jnp.cos(ang), jnp.sin(ang)
    c32, s32 = cos[:, :32], sin[:, :32]
    c64, s64 = cos[:, 32:96], sin[:, 32:96]
    c64_ref[0] = jnp.concatenate([c32] * 4, axis=1)
    s64_ref[0] = jnp.concatenate([s32] * 4, axis=1) * sgn_ref[0:1, :]
    c128_ref[0] = jnp.concatenate([c64] * 2, axis=1)
    s128_ref[0] = jnp.concatenate([s64] * 2, axis=1) * sgn_ref[1:2, :]


def _rope_tables(positions):
    bsz, seq = positions.shape
    ts = 512
    inv32 = ROPE_THETA ** (-jnp.arange(0, 64, 2, dtype=F32) / 64)
    inv64 = ROPE_THETA ** (-jnp.arange(0, 128, 2, dtype=F32) / 128)
    inv = jnp.concatenate([inv32, inv64, jnp.zeros((32,), F32)])[None, :]
    sgn = jnp.asarray(np.stack([np.tile(np.repeat([-1.0, 1.0], 32), 2), np.repeat([-1.0, 1.0], 64)]), F32)
    pos = positions.astype(F32)[..., None]
    tab = jax.ShapeDtypeStruct((bsz, seq, LANES), F32)
    tspec = pl.BlockSpec((1, ts, LANES), lambda b, s: (b, s, 0))
    return pl.pallas_call(
        _rope_tables_kernel,
        grid=(bsz, seq // ts),
        in_specs=[pl.BlockSpec((1, ts, 1), lambda b, s: (b, s, 0)), _const_spec((1, LANES)), _const_spec((2, LANES))],
        out_specs=[tspec] * 4,
        out_shape=[tab] * 4,
        compiler_params=_cparams(("parallel", "parallel")),
        name="rope_tables",
    )(pos, inv, sgn)


def _rms(x, w):
    return x * lax.rsqrt(jnp.mean(x * x, axis=-1, keepdims=True) + RMS_EPS) * w


IN0_COLS = np.cumsum((0, Q_LORA, KV_LORA, 2 * MLA_ROPE, 3 * DIL_HEADS * DIL_HD, MLA_HEADS * MLA_V + DIL_HEADS * DIL_HD))


def _in0_kernel(x_ref, w_ref, wuq_ref, wukv_ref, qnorm_ref, kvnorm_ref,
                c64_ref, s64_ref, c128_ref, s128_ref,
                qn_ref, qr_ref, kv_ref, kr_ref, qb_ref, kb_ref, vb_ref, g_ref):
    x = x_ref[0].astype(BF16)
    c64, s64, c128, s128 = c64_ref[0], s64_ref[0], c128_ref[0], s128_ref[0]

    def proj(i):
        return _dot(x, w_ref[:, IN0_COLS[i]:IN0_COLS[i + 1]])

    cq = _rms(proj(0), qnorm_ref[...])
    qa = _dot(cq.astype(BF16), wuq_ref[...]) * ((MLA_NOPE + MLA_ROPE) ** -0.5 * LOG2E)
    nope = MLA_HEADS * MLA_NOPE
    qn_ref[0] = qa[:, :nope].astype(BF16)
    for j in range(MLA_HEADS * MLA_ROPE // LANES):
        blk = qa[:, nope + j * LANES:nope + (j + 1) * LANES]
        qr_ref[0, :, j * LANES:(j + 1) * LANES] = _rope_pairs64(blk, c64, s64).astype(BF16)
    ckv = _rms(proj(1), kvnorm_ref[...])
    kv_ref[0] = _dot(ckv.astype(BF16), wukv_ref[...]).astype(BF16)
    kr_ref[0] = _rope_pairs64(proj(2), c64, s64).astype(BF16)
    hb = proj(3)
    width = DIL_HEADS * DIL_HD
    for h in range(DIL_HEADS):
        sl = slice(h * DIL_HD, (h + 1) * DIL_HD)
        qb_ref[0, :, sl] = _rope_full128(hb[:, sl], c128, s128) * (DIL_HD ** -0.5 * LOG2E)
        kb_ref[0, :, sl] = _rope_full128(hb[:, width + h * DIL_HD:width + (h + 1) * DIL_HD], c128, s128)
    vb_ref[0] = hb[:, 2 * width:]
    g_ref[0] = _silu(proj(4)).astype(BF16)


def _in0(x, tabs, w_in, q_norm, w_uq, kv_norm, w_ukv, tm=512):
    bsz, seq, _ = x.shape
    c64, s64, c128, s128 = tabs
    kr0 = Q_LORA + KV_LORA
    w_all = _splice_columns(w_in, kr0 + MLA_ROPE, kr0, 0)
    wuq3 = w_uq.reshape(Q_LORA, MLA_HEADS, MLA_NOPE + MLA_ROPE)
    wuq = jnp.concatenate([wuq3[:, :, :MLA_NOPE].reshape(Q_LORA, -1), wuq3[:, :, MLA_NOPE:].reshape(Q_LORA, -1)],
                          axis=1).astype(BF16)
    wukv = w_ukv.astype(BF16)
    weights = (w_all, wuq, wukv, q_norm.reshape(1, -1), kv_norm.reshape(1, -1))

    def rows(width):
        return pl.BlockSpec((1, tm, width), lambda b, s: (b, s, 0))

    out_widths = (1024, 512, 2048, LANES, 1024, 1024, 1024, 2048)
    out_dtypes = (BF16, BF16, BF16, BF16, F32, F32, F32, BF16)
    return pl.pallas_call(
        _in0_kernel,
        grid=(bsz, seq // tm),
        in_specs=[rows(D_MODEL)] + [_const_spec(w.shape) for w in weights] + [rows(LANES)] * 4,
        out_specs=[rows(w) for w in out_widths],
        out_shape=[jax.ShapeDtypeStruct((bsz, seq, w), dt) for w, dt in zip(out_widths, out_dtypes)],
        compiler_params=_cparams(("parallel", "parallel")),
        name="in_proj0",
    )(x, *weights, c64, s64, c128, s128)


MLA_SUB = 256
MLA_ONES_ROWS = 16
MLA_AHEAD = 5
LOG2E = 1.4426950408889634


def _mla_kernel(qn_ref, qr_ref, kn_ref, kr_ref, v_ref, o_ref, k_s, vt_s):
    h = pl.program_id(1)

    @pl.when(pl.program_id(2) == 0)
    def _():
        k_s[:, :MLA_NOPE] = kn_ref[0]
        k_s[:, MLA_NOPE:] = kr_ref[0]
        vt_s[:MLA_V, :] = v_ref[0].astype(F32).T.astype(BF16)
        vt_s[MLA_V:, :] = jnp.ones((MLA_ONES_ROWS, vt_s.shape[1]), BF16)

    lane = lax.broadcasted_iota(jnp.int32, (MLA_SUB, LANES), 1)
    mine = (lane // MLA_ROPE) == (h % 2)
    nsub = qn_ref.shape[1] // MLA_SUB

    def scores(i):
        rows = slice(i * MLA_SUB, (i + 1) * MLA_SUB)
        qr = jnp.where(mine, qr_ref[0, rows, :], jnp.zeros((MLA_SUB, LANES), BF16))
        return _dot_nt(k_s[...], jnp.concatenate([qn_ref[0, rows, :], qr], axis=1))

    ahead = [scores(i) for i in range(min(MLA_AHEAD, nsub))]
    for i in range(nsub):
        s = ahead.pop(0)
        if i + MLA_AHEAD < nsub:
            ahead.append(scores(i + MLA_AHEAD))
        m = jnp.max(s, axis=0, keepdims=True)
        p = jnp.exp2(s - m).astype(BF16)
        ot = _dot(vt_s[...], p)
        o = ot[:MLA_V] / ot[MLA_V:MLA_V + 1]
        o_ref[0, i * MLA_SUB:(i + 1) * MLA_SUB, :] = o.T.astype(o_ref.dtype)


def _mla(qn, qr, kv, kr, tq=2048):
    bsz, seq, _ = qn.shape
    return pl.pallas_call(
        _mla_kernel,
        grid=(bsz, MLA_HEADS, seq // tq),
        in_specs=[
            pl.BlockSpec((1, tq, LANES), lambda b, h, i: (b, i, h)),
            pl.BlockSpec((1, tq, LANES), lambda b, h, i: (b, i, h // 2)),
            pl.BlockSpec((1, seq, LANES), lambda b, h, i: (b, 0, 2 * h)),
            pl.BlockSpec((1, seq, LANES), lambda b, h, i: (b, 0, 0)),
            pl.BlockSpec((1, seq, LANES), lambda b, h, i: (b, 0, 2 * h + 1)),
        ],
        out_specs=pl.BlockSpec((1, tq, LANES), lambda b, h, i: (b, i, h)),
        out_shape=jax.ShapeDtypeStruct((bsz, seq, MLA_HEADS * MLA_V), BF16),
        scratch_shapes=[pltpu.VMEM((seq, MLA_NOPE + LANES), BF16), pltpu.VMEM((MLA_V + MLA_ONES_ROWS, seq), BF16)],
        compiler_params=_cparams(("parallel", "parallel", "arbitrary")),
        name="mla_attention",
    )(qn, qr, kv, kr, kv)


DIL_TQ = 128
DIL_RADIUS = 64
DIL_TILES_PER_STEP = 16


def _dil_bias_table():
    i = np.arange(DIL_TQ)[:, None]
    j = np.arange(2 * DIL_TQ)[None, :]
    tabs = [np.where(np.abs(j - off - i) <= DIL_RADIUS, 0.0, -np.inf) for off in (0, DIL_RADIUS, 2 * DIL_RADIUS)]
    return jnp.asarray(np.stack(tabs), F32)


def _dil_kernel(q_ref, k_ref, v_ref, bias_ref, o_ref, qd_s, kd_s, vd_s, m_s, l_s, acc_s):
    seq = q_ref.shape[1]
    vd_s[:, :, DIL_HD:] = jnp.ones((len(DILATED_GROUPS), seq, DIL_HD), BF16)

    def rows_of(start, size, dil):
        return pl.ds(start, size) if dil == 1 else pl.ds(start, size, stride=dil)

    for g, (_, dil) in enumerate(DILATED_GROUPS):
        length = seq // dil
        for src, dst in ((q_ref, qd_s), (k_ref, kd_s), (v_ref, vd_s)):
            for r in range(dil):
                dst[g, r * length:(r + 1) * length, :DIL_HD] = src[0, rows_of(r, length, dil), :].astype(BF16)

    def tiles_step(g, dil, which, order):
        length = seq // dil
        tiles = length // DIL_TQ
        kw = min(length, DIL_TQ + 2 * DIL_RADIUS)
        n = len(which)
        ws = [jnp.clip(t * DIL_TQ - DIL_RADIUS, 0, length - kw) for _, t in which]
        qrows = [pl.ds(pl.multiple_of(r * length + t * DIL_TQ, DIL_TQ), DIL_TQ) for r, t in which]
        krows = [pl.ds(pl.multiple_of(r * length + ws[i], DIL_RADIUS), kw) for i, (r, _) in enumerate(which)]
        if dil == 1:
            out_rows = qrows
        else:
            out_rows = [rows_of(r + t * (DIL_TQ * dil), DIL_TQ, dil) for r, t in which]
        if tiles == 1:
            bias = [bias_ref[0, :, :kw] for _ in which]
        else:
            bias = [bias_ref[jnp.where(t == 0, 0, jnp.where(t == tiles - 1, 2, 1))] for _, t in which]
        s = [_dot_nt(qd_s[g, qrows[i], :], kd_s[g, krows[i], :]) + bias[i] for i in range(n)]
        mk = [jnp.max(t, axis=-1, keepdims=True) for t in s]
        m = [jnp.broadcast_to(t, (DIL_TQ, DIL_HD)) for t in mk]
        p = [jnp.exp2(s[i] - mk[i]).astype(BF16) for i in range(n)]
        ol = [_dot(p[i], vd_s[g, krows[i], :]) for i in range(n)]
        o = [t[:, :DIL_HD] for t in ol]
        l = [t[:, DIL_HD:] for t in ol]
        for i in range(n):
            if order == 0:
                m_s[out_rows[i], :] = m[i]
                l_s[out_rows[i], :] = l[i]
                acc_s[out_rows[i], :] = o[i]
                continue
            m_old = m_s[out_rows[i], :]
            m_new = jnp.maximum(m_old, m[i])
            a_old = jnp.exp2(m_old - m_new)
            a_new = jnp.exp2(m[i] - m_new)
            l_new = a_old * l_s[out_rows[i], :] + a_new * l[i]
            acc_new = a_old * acc_s[out_rows[i], :] + a_new * o[i]
            if order == len(DILATED_GROUPS) - 1:
                o_ref[0, out_rows[i], :] = (acc_new / l_new).astype(o_ref.dtype)
            else:
                m_s[out_rows[i], :] = m_new
                l_s[out_rows[i], :] = l_new
                acc_s[out_rows[i], :] = acc_new

    for order, g in enumerate(reversed(range(len(DILATED_GROUPS)))):
        window, dil = DILATED_GROUPS[g]
        assert window // (2 * dil) == DIL_RADIUS
        assert order < len(DILATED_GROUPS) - 1 or dil == 1
        tiles = seq // dil // DIL_TQ

        def body(it, carry, g=g, dil=dil, tiles=tiles, order=order):
            idx = [it * DIL_TILES_PER_STEP + j for j in range(DIL_TILES_PER_STEP)]
            tiles_step(g, dil, [(i // tiles, i % tiles) for i in idx], order)
            return carry

        lax.fori_loop(0, dil * tiles // DIL_TILES_PER_STEP, body, 0)


def _dilated(qb, kb, vb):
    bsz, seq, _ = qb.shape
    spec = pl.BlockSpec((1, seq, DIL_HD), lambda b, h: (b, 0, h))
    bias = _dil_bias_table()
    grouped = pltpu.VMEM((len(DILATED_GROUPS), seq, DIL_HD), BF16)
    stat = pltpu.VMEM((seq, DIL_HD), F32)
    return pl.pallas_call(
        _dil_kernel,
        grid=(bsz, DIL_HEADS),
        in_specs=[spec] * 3 + [_const_spec(bias.shape)],
        out_specs=spec,
        out_shape=jax.ShapeDtypeStruct((bsz, seq, DIL_HEADS * DIL_HD), BF16),
        scratch_shapes=[grouped, grouped, pltpu.VMEM((len(DILATED_GROUPS), seq, 2 * DIL_HD), BF16), stat, stat, stat],
        compiler_params=_cparams(("parallel", "parallel")),
        name="dilated_attention",
    )(qb, kb, vb, bias)


OUT_SUB = 128


def _out_kernel(o1_ref, o2_ref, g_ref, x_ref, w_ref, lnw_ref, lnb_ref, out_ref):
    half = o1_ref.shape[-1]
    for i in range(o1_ref.shape[1] // OUT_SUB):
        rows = slice(i * OUT_SUB, (i + 1) * OUT_SUB)
        y1 = o1_ref[0, rows, :] * g_ref[0, rows, :half]
        y2 = o2_ref[0, rows, :] * g_ref[0, rows, half:]
        r = DEEPNORM_ALPHA * x_ref[0, rows, :] + _dot(y1, w_ref[:half, :]) + _dot(y2, w_ref[half:, :])
        mu = jnp.mean(r, axis=-1, keepdims=True)
        rc = r - mu
        var = jnp.mean(rc * rc, axis=-1, keepdims=True)
        out_ref[0, rows, :] = rc * lax.rsqrt(var + LN_EPS) * lnw_ref[...] + lnb_ref[...]


def _out_proj(o1, o2, gate, x, w_out, ln_w, ln_b, tm=512):
    bsz, seq, _ = x.shape
    half = o1.shape[-1]

    def rows(width):
        return pl.BlockSpec((1, tm, width), lambda b, s: (b, s, 0))

    w = w_out.astype(BF16)
    return pl.pallas_call(
        _out_kernel,
        grid=(bsz, seq // tm),
        in_specs=[rows(half), rows(half), rows(2 * half), rows(D_MODEL), _const_spec(w.shape),
                  _const_spec((1, D_MODEL)), _const_spec((1, D_MODEL))],
        out_specs=rows(D_MODEL),
        out_shape=jax.ShapeDtypeStruct((bsz, seq, D_MODEL), F32),
        compiler_params=_cparams(("parallel", "parallel")),
        name="out_proj",
    )(o1, o2, gate, x, w, ln_w.reshape(1, -1), ln_b.reshape(1, -1))


def _layer0(x, tabs, w_in, q_norm, w_uq, kv_norm, w_ukv, w_out, ln_w, ln_b):
    qn, qr, kv, kr, qb, kb, vb, gate = _in0(x, tabs, w_in, q_norm, w_uq, kv_norm, w_ukv)
    o_a = _mla(qn, qr, kv, kr)
    o_b = _dilated(qb, kb, vb)
    return _out_proj(o_a, o_b, gate, x, w_out, ln_w, ln_b)


GATE_BETA = 0
GATE_GC = 2 * GDN_HEADS


def _softplus(x):
    return jnp.maximum(x, 0.0) + jnp.log1p(jnp.exp(-jnp.abs(x)))


IN1_COLS = np.cumsum((0, 2 * GDN_HEADS * GDN_DK + GDN_HEADS * GDN_DV, LANES, 2 * RET_HEADS * RET_DK,
                      RET_HEADS * RET_DV, GDN_HEADS * GDN_DV + RET_HEADS * RET_DV))


def _in1_kernel(x_ref, w_ref, gp_ref, c64_ref, s64_ref,
                qkvc_ref, gate_ref, qr_ref, kr_ref, vr_ref, g_ref):
    x = x_ref[0].astype(BF16)
    tm = x.shape[0]
    c64, s64 = c64_ref[0], s64_ref[0]

    def proj(i):
        return _dot(x, w_ref[:, IN1_COLS[i]:IN1_COLS[i + 1]])

    qkvc_ref[0] = proj(0).astype(qkvc_ref.dtype)
    hg = proj(1)
    beta = jax.nn.sigmoid(hg)
    g = -jnp.exp(gp_ref[0:1, :]) * _softplus(hg + gp_ref[1:2, :])
    pos = lax.broadcasted_iota(jnp.int32, hg.shape, 0) % GDN_CHUNK
    pre, suf = g, g
    step = 1
    while step < GDN_CHUNK:
        pre = pre + jnp.where(pos >= step, pltpu.roll(pre, step, 0), 0.0)
        suf = suf + jnp.where(pos < GDN_CHUNK - step, pltpu.roll(suf, tm - step, 0), 0.0)
        step *= 2
    lane = lax.broadcasted_iota(jnp.int32, hg.shape, 1)
    gc = jnp.where(lane < GATE_GC + GDN_HEADS, pre, suf)
    gate_ref[0] = jnp.where(lane < GATE_GC, beta, gc)
    hqk = proj(2)
    width = RET_HEADS * RET_DK
    for j in range(width // LANES):
        qr_ref[0, :, j * LANES:(j + 1) * LANES] = _rope_pairs64(hqk[:, j * LANES:(j + 1) * LANES], c64, s64).astype(BF16)
        kblk = hqk[:, width + j * LANES:width + (j + 1) * LANES]
        kr_ref[0, :, j * LANES:(j + 1) * LANES] = (_rope_pairs64(kblk, c64, s64) * (RET_DK ** -0.5)).astype(BF16)
    vr_ref[0] = proj(3).astype(BF16)
    g_ref[0] = _silu(proj(4)).astype(BF16)


def _in1(x, tabs, w_in, a_log, dt_bias, tm=512):
    bsz, seq, _ = x.shape
    c64, s64, _, _ = tabs
    conv_ch = 2 * GDN_HEADS * GDN_DK + GDN_HEADS * GDN_DV
    gate_end = conv_ch + 4 * GDN_HEADS
    w_all = _splice_columns(w_in, gate_end, gate_end, LANES - 4 * GDN_HEADS)
    pad = (GATE_GC, LANES - GATE_GC - 2 * GDN_HEADS)
    gp = jnp.stack([jnp.pad(a_log.reshape(-1), pad), jnp.pad(dt_bias.reshape(-1), pad)]).astype(F32)
    weights = (w_all, gp)

    def rows(width):
        return pl.BlockSpec((1, tm, width), lambda b, s: (b, s, 0))

    out_widths = (conv_ch, LANES, 512, 512, 1024, 2048)
    out_dtypes = (BF16, F32, BF16, BF16, BF16, BF16)
    return pl.pallas_call(
        _in1_kernel,
        grid=(bsz, seq // tm),
        in_specs=[rows(D_MODEL)] + [_const_spec(w.shape) for w in weights] + [rows(LANES)] * 2,
        out_specs=[rows(w) for w in out_widths],
        out_shape=[jax.ShapeDtypeStruct((bsz, seq, w), dt) for w, dt in zip(out_widths, out_dtypes)],
        compiler_params=_cparams(("parallel", "parallel")),
        name="in_proj1",
    )(x, *weights, c64, s64)


CONV_PAD = 8
GDN_P1_CHUNKS = 8
GDN_P3_CHUNKS = 16
GDN_LEVELS = 7


def _gdn_level_masks():
    i = np.arange(GDN_CHUNK)[:, None]
    j = np.arange(GDN_CHUNK)[None, :]
    masks = [((i >> (k + 1)) == (j >> (k + 1))) & ((i >> k) != (j >> k)) for k in range(GDN_LEVELS)]
    return jnp.asarray(np.stack(masks), F32)


GDN_HPS = 1
CONV_ROWS = 256
G_BETA, G_GC = 0, 2


def _gdn2_kernel(qraw_ref, kraw_ref, vraw_ref, wq_ref, wk_ref, wv_ref, grow_ref, norm_ref, lvl_ref, o_ref,
                 xpad_s, q_s, k_s, v_s, w_s, u_s, att_s, qg_s, mt_s, bt_s, st_s, gl_s, o_s):
    seq = q_s.shape[1]
    nchunk = seq // GDN_CHUNK

    zeros = jnp.zeros((CONV_PAD, LANES), F32)
    xpad_s[0:CONV_PAD, :] = zeros
    xpad_s[CONV_PAD + seq:, :] = zeros

    def conv_silu(raw_ref, w_ref, lanes, dst, finish):
        xpad_s[CONV_PAD:CONV_PAD + seq, :] = raw_ref[0, :, lanes].astype(F32)
        for blk in range(seq // CONV_ROWS):
            acc = None
            for j in range(CONV_K):
                off = blk * CONV_ROWS + CONV_PAD - (CONV_K - 1) // 2 + j
                term = xpad_s[off:off + CONV_ROWS, :] * w_ref[j:j + 1, lanes]
                acc = term if acc is None else acc + term
            dst[blk * CONV_ROWS:(blk + 1) * CONV_ROWS, :] = finish(_silu(acc))

    def l2norm(t):
        return t * lax.rsqrt(jnp.sum(t * t, axis=-1, keepdims=True) + L2_EPS)

    for hh in range(GDN_HPS):
        lanes = slice(hh * LANES, (hh + 1) * LANES)
        conv_silu(qraw_ref, wq_ref, lanes, q_s.at[hh], lambda t: l2norm(t) * (GDN_DK ** -0.5))
        conv_silu(kraw_ref, wk_ref, lanes, k_s.at[hh], l2norm)
        conv_silu(vraw_ref, wv_ref, lanes, v_s.at[hh], lambda t: t)

    ii = lax.broadcasted_iota(jnp.int32, (GDN_CHUNK, GDN_CHUNK), 0)
    jj = lax.broadcasted_iota(jnp.int32, (GDN_CHUNK, GDN_CHUNK), 1)
    causal = (ii >= jj, ii <= jj)
    strict = (ii > jj, ii < jj)

    def chunk_rows(c):
        return pl.ds(pl.multiple_of(c * GDN_CHUNK, GDN_CHUNK), GDN_CHUNK)

    def token_major(row):
        return jnp.broadcast_to(row, (LANES, GDN_CHUNK)).T

    def phase1(it, carry, hh):
        chains = [(it * GDN_P1_CHUNKS + j, d) for j in range(GDN_P1_CHUNKS) for d in range(2)]
        n_ch = len(chains)
        rows = [chunk_rows(c) for c, _ in chains]
        kf = [k_s[hh, r, :] for r in rows]
        kc = [t.astype(BF16) for t in kf]
        grow = [grow_ref[0, hh, G_GC + d, pl.ds(c, 1), :] for c, d in chains]
        gcb = [token_major(t) for t in grow]
        bcb = [token_major(grow_ref[0, hh, G_BETA + d, pl.ds(c, 1), :]) for c, d in chains]
        kb = [kf[i] * bcb[i] for i in range(n_ch)]
        kk = [_dot_nt(kb[i].astype(BF16), kc[i]) for i in range(n_ch)]
        decay = [jnp.exp(jnp.where(causal[d], gcb[i] - grow[i], -jnp.inf)) for i, (_, d) in enumerate(chains)]
        a = [jnp.where(strict[d], kk[i] * decay[i], 0.0) for i, (_, d) in enumerate(chains)]
        n = [-(t * lvl_ref[0]) for t in a]
        for lvl in range(1, GDN_LEVELS):
            ak = [t * lvl_ref[lvl] for t in a]
            y = [ak[i] + _dot(ak[i].astype(BF16), n[i].astype(BF16)) for i in range(n_ch)]
            n = [n[i] - y[i] - _dot(n[i].astype(BF16), y[i].astype(BF16)) for i in range(n_ch)]
        nb = [t.astype(BF16) for t in n]
        egc = [jnp.exp(t) for t in gcb]
        wpre = [kb[i] * egc[i] for i in range(n_ch)]
        upre = [v_s[hh, rows[i], :] * bcb[i] for i in range(n_ch)]
        w = [(wpre[i] + _dot(nb[i], wpre[i].astype(BF16))).astype(BF16) for i in range(n_ch)]
        u = [upre[i] + _dot(nb[i], upre[i].astype(BF16)) for i in range(n_ch)]
        qf = [q_s[hh, rows[2 * j], :] for j in range(GDN_P1_CHUNKS)]
        qkc = [_dot_nt(qf[j].astype(BF16), kc[2 * j]) for j in range(GDN_P1_CHUNKS)]
        gl = [gcb[i][GDN_CHUNK - 1:GDN_CHUNK, :] if d == 0 else gcb[i][0:1, :] for i, (_, d) in enumerate(chains)]
        kd = [(kf[i] * jnp.exp(gl[i] - gcb[i])).astype(BF16) for i in range(n_ch)]
        mt = [_dot_tn(w[i], kd[i]) for i in range(n_ch)]
        bt = [_dot_tn(u[i].astype(BF16), kd[i]) for i in range(n_ch)]
        for i, (c, d) in enumerate(chains):
            r = rows[i]
            w_s[hh, d, r, :] = w[i]
            u_s[hh, d, r, :] = u[i]
            att_s[hh, d, r, :] = (qkc[i // 2] * decay[i]).astype(BF16)
            qg_s[hh, d, r, :] = (qf[i // 2] * egc[i]).astype(BF16)
            mt_s[hh, d, c] = (-mt[i]).astype(BF16)
            bt_s[hh, d, c] = bt[i]
            gl_s[hh, d, c] = jnp.broadcast_to(jnp.exp(gl[i]), (8, LANES))
        return carry

    for hh in range(GDN_HPS):
        lax.fori_loop(0, nchunk // GDN_P1_CHUNKS, functools.partial(phase1, hh=hh), 0)

    def phase2(i, carry):
        new = []
        for hh in range(GDN_HPS):
            for d in range(2):
                c = i if d == 0 else nchunk - 1 - i
                st = carry[2 * hh + d]
                st_s[hh, d, c] = st.astype(BF16)
                new.append(st * gl_s[hh, d, c, 0:1, :] + _dot(st.astype(BF16), mt_s[hh, d, c]) + bt_s[hh, d, c])
        return tuple(new)

    zero_state = jnp.zeros((GDN_DV, GDN_DK), F32)
    lax.fori_loop(0, nchunk, phase2, (zero_state,) * (2 * GDN_HPS))

    def phase3(it, carry, hh):
        chains = [(it * GDN_P3_CHUNKS + j, d) for j in range(GDN_P3_CHUNKS) for d in range(2)]
        rows = [chunk_rows(c) for c, _ in chains]
        m = [_dot_nt(jnp.concatenate([w_s[hh, d, rows[i], :], qg_s[hh, d, rows[i], :]], axis=0), st_s[hh, d, c])
             for i, (c, d) in enumerate(chains)]
        vnew = [(u_s[hh, d, rows[i], :] - m[i][:GDN_CHUNK]).astype(BF16) for i, (_, d) in enumerate(chains)]
        od = [m[i][GDN_CHUNK:] + _dot(att_s[hh, d, rows[i], :], vnew[i]) for i, (_, d) in enumerate(chains)]
        for j in range(GDN_P3_CHUNKS):
            o = od[2 * j] + od[2 * j + 1]
            o_s[hh, rows[2 * j], :] = (o * lax.rsqrt(jnp.mean(o * o, axis=-1, keepdims=True) + RMS_EPS)
                                       * norm_ref[...]).astype(o_s.dtype)
        return carry

    for hh in range(GDN_HPS):
        lax.fori_loop(0, nchunk // GDN_P3_CHUNKS, functools.partial(phase3, hh=hh), 0)
    for hh in range(GDN_HPS):
        o_ref[0, :, hh * LANES:(hh + 1) * LANES] = o_s[hh]


def _gdn2(qkvc, conv_w, grow, gdn_norm):
    bsz, seq, _ = qkvc.shape
    groups = GDN_HEADS // GDN_HPS
    width = GDN_HPS * LANES
    nchunk = seq // GDN_CHUNK

    def col(off):
        return pl.BlockSpec((1, seq, width), lambda b, g: (b, 0, off + g))

    def wcol(off):
        return pl.BlockSpec((CONV_K, width), lambda b, g: (0, off + g))

    def per_head(shape, dtype):
        return pltpu.VMEM((GDN_HPS,) + shape, dtype)

    return pl.pallas_call(
        _gdn2_kernel,
        grid=(bsz, groups),
        in_specs=[col(0), col(groups), col(2 * groups), wcol(0), wcol(groups), wcol(2 * groups),
                  pl.BlockSpec((1, GDN_HPS, 4, nchunk, GDN_CHUNK), lambda b, g: (b, g, 0, 0, 0)),
                  _const_spec((1, GDN_DV)), _const_spec((GDN_LEVELS, GDN_CHUNK, GDN_CHUNK))],
        out_specs=pl.BlockSpec((1, seq, width), lambda b, g: (b, 0, g)),
        out_shape=jax.ShapeDtypeStruct((bsz, seq, GDN_HEADS * GDN_DV), BF16),
        scratch_shapes=[pltpu.VMEM((seq + 2 * CONV_PAD, LANES), F32),
                        per_head((seq, LANES), F32), per_head((seq, LANES), F32), per_head((seq, LANES), F32),
                        per_head((2, seq, LANES), BF16), per_head((2, seq, LANES), F32),
                        per_head((2, seq, LANES), BF16), per_head((2, seq, LANES), BF16),
                        per_head((2, nchunk, GDN_DV, GDN_DK), BF16), per_head((2, nchunk, GDN_DV, GDN_DK), F32),
                        per_head((2, nchunk, GDN_DV, GDN_DK), BF16), per_head((2, nchunk, 8, LANES), F32),
                        per_head((seq, LANES), BF16)],
        compiler_params=_cparams(("parallel", "parallel")),
        name="gated_deltanet",
    )(qkvc, qkvc, qkvc, conv_w, conv_w, conv_w, grow, gdn_norm.reshape(1, -1), _gdn_level_masks())


def _ret_tables():
    heads = np.arange(RET_HEADS, dtype=np.float64)
    log_gamma = np.log1p(-(2.0 ** (-RET_DECAY_BASE - heads)))[:, None]
    idx = np.arange(RET_CHUNK, dtype=np.float64)
    dmat = np.exp(log_gamma[:, :, None] * np.abs(idx[:, None] - idx[None, :])[None])
    xi = np.exp(log_gamma * (idx + 1.0))
    zeta = np.exp(log_gamma * (RET_CHUNK - 1.0 - idx))
    gch = np.exp(log_gamma * RET_CHUNK)

    def cb(t):
        return jnp.asarray(np.broadcast_to(t[:, :, None], t.shape + (LANES,)), F32)

    return jnp.asarray(dmat, F32), cb(xi), cb(zeta), cb(gch)


def _ret_kernel(q_ref, k_ref, v_ref, d_ref, xi_ref, zeta_ref, gch_ref, nw_ref, nb_ref, o_ref, kvf_s, kvb_s):
    seq = q_ref.shape[1]
    nchunk = seq // RET_CHUNK
    h = pl.program_id(1)
    xi, zeta, gch = xi_ref[0], zeta_ref[0], gch_ref[0]
    lane = lax.broadcasted_iota(jnp.int32, (RET_CHUNK, LANES), 1)
    mine = (lane // RET_DK) == (h % 2)

    def rows(c):
        return slice(c * RET_CHUNK, (c + 1) * RET_CHUNK)

    for c in range(nchunk):
        kc = k_ref[0, rows(c), :].astype(F32)
        vc = v_ref[0, rows(c), :]
        kvf_s[c] = _dot_tn((kc * zeta).astype(BF16), vc)
        kvb_s[c] = _dot_tn((kc * xi).astype(BF16), vc)
    state = jnp.zeros((LANES, RET_DV), F32)
    for c in range(nchunk):
        nxt = state * gch + kvf_s[c]
        kvf_s[c] = state
        state = nxt
    state = jnp.zeros((LANES, RET_DV), F32)
    for c in reversed(range(nchunk)):
        nxt = state * gch + kvb_s[c]
        kvb_s[c] = state
        state = nxt
    for c in range(nchunk):
        q = jnp.where(mine, q_ref[0, rows(c), :], jnp.zeros((RET_CHUNK, LANES), BF16))
        qf = q.astype(F32)
        s = _dot_nt(q, k_ref[0, rows(c), :]) * d_ref[0]
        o = _dot(s.astype(BF16), v_ref[0, rows(c), :])
        o = o + _dot((qf * xi).astype(BF16), kvf_s[c].astype(BF16))
        o = o + _dot((qf * zeta).astype(BF16), kvb_s[c].astype(BF16))
        mu = jnp.mean(o, axis=-1, keepdims=True)
        oc = o - mu
        var = jnp.mean(oc * oc, axis=-1, keepdims=True)
        o_ref[0, rows(c), :] = (oc * lax.rsqrt(var + LN_EPS) * nw_ref[...] + nb_ref[...]).astype(o_ref.dtype)


def _retention(qr, kr, vr, norm_w, norm_b):
    bsz, seq, _ = vr.shape
    dmat, xi, zeta, gch = _ret_tables()
    pair = pl.BlockSpec((1, seq, LANES), lambda b, h: (b, 0, h // 2))
    head = pl.BlockSpec((1, seq, RET_DV), lambda b, h: (b, 0, h))

    def per_head(shape):
        return pl.BlockSpec((1,) + shape, lambda b, h: (h, 0, 0))

    vec = pl.BlockSpec((1, RET_DV), lambda b, h: (0, h))
    state = pltpu.VMEM((seq // RET_CHUNK, LANES, RET_DV), F32)
    return pl.pallas_call(
        _ret_kernel,
        grid=(bsz, RET_HEADS),
        in_specs=[pair, pair, head, per_head((RET_CHUNK, RET_CHUNK)), per_head((RET_CHUNK, LANES)),
                  per_head((RET_CHUNK, LANES)), per_head((1, LANES)), vec, vec],
        out_specs=head,
        out_shape=jax.ShapeDtypeStruct((bsz, seq, RET_HEADS * RET_DV), BF16),
        scratch_shapes=[state, state],
        compiler_params=_cparams(("parallel", "parallel")),
        name="retention",
    )(qr, kr, vr, dmat, xi, zeta, gch, norm_w.reshape(1, -1), norm_b.reshape(1, -1))


def _layer1(x, tabs, w_in, conv_w, a_log, dt_bias, gdn_norm, ret_norm_w, ret_norm_b, w_out, ln_w, ln_b):
    bsz, seq, _ = x.shape
    qkvc, gates, qr, kr, vr, gate = _in1(x, tabs, w_in, a_log, dt_bias)
    gsl = gates[:, :, :4 * GDN_HEADS].reshape(bsz, seq, 4, GDN_HEADS)
    grow = gsl.transpose(0, 3, 2, 1).reshape(bsz, GDN_HEADS, 4, seq // GDN_CHUNK, GDN_CHUNK)
    o_c = _gdn2(qkvc, conv_w, grow, gdn_norm)
    o_d = _retention(qr, kr, vr, ret_norm_w, ret_norm_b)
    return _out_proj(o_c, o_d, gate, x, w_out, ln_w, ln_b)


def kernel(x, positions, even_w_in, even_q_norm, even_w_uq, even_kv_norm, even_w_ukv, even_w_out, even_ln_w, even_ln_b, odd_w_in, odd_conv_w, odd_a_log, odd_dt_bias, odd_gdn_norm, odd_ret_norm_w, odd_ret_norm_b, odd_w_out, odd_ln_w, odd_ln_b):
    tabs = _rope_tables(positions)
    x = _layer0(x, tabs, even_w_in, even_q_norm[0], even_w_uq[0], even_kv_norm[0], even_w_ukv[0],
                even_w_out[0], even_ln_w[0], even_ln_b[0])
    x = _layer1(x, tabs, odd_w_in, odd_conv_w[0], odd_a_log[0], odd_dt_bias[0], odd_gdn_norm[0],
                odd_ret_norm_w[0], odd_ret_norm_b[0], odd_w_out[0], odd_ln_w[0], odd_ln_b[0])
    return x
```

```python
import functools

import numpy as np
import jax
import jax.numpy as jnp
from jax import lax
from jax.experimental import pallas as pl
from jax.experimental.pallas import tpu as pltpu

F32 = jnp.float32
BF16 = jnp.bfloat16

D_MODEL = 1024
DEPTH = 2
DEEPNORM_ALPHA = (2 * DEPTH) ** 0.25
ROPE_THETA = 10000.0
RMS_EPS = 1e-6
LN_EPS = 1e-5
L2_EPS = 1e-6

MLA_HEADS = 8
MLA_NOPE = 128
MLA_ROPE = 64
MLA_V = 128
Q_LORA = 768
KV_LORA = 256

DIL_HEADS = 8
DIL_HD = 128
DILATED_GROUPS = ((128, 1), (512, 4), (2048, 16))

GDN_HEADS = 8
GDN_DK = 128
GDN_DV = 128
CONV_K = 5
GDN_CHUNK = 128

RET_HEADS = 8
RET_DK = 64
RET_DV = 128
RET_DECAY_BASE = 5.0
RET_CHUNK = 256

LANES = 128
VMEM_LIMIT = 56 * 1024 * 1024


def _cparams(sem):
    return pltpu.CompilerParams(dimension_semantics=sem, vmem_limit_bytes=VMEM_LIMIT)


def _dot(a, b):
    return jnp.dot(a, b, preferred_element_type=F32)


def _dot_nt(a, b):
    return lax.dot_general(a, b, (((1,), (1,)), ((), ())), preferred_element_type=F32)


def _dot_tn(a, b):
    return lax.dot_general(a, b, (((0,), (0,)), ((), ())), preferred_element_type=F32)


def _const_spec(shape):
    nd = len(shape)
    return pl.BlockSpec(shape, lambda *_: (0,) * nd, pipeline_mode=pl.Buffered(1))


def _rope_pairs64(x, cos, sin_signed):
    lane = lax.broadcasted_iota(jnp.int32, x.shape, 1)
    swapped = jnp.where((lane % 64) < 32, pltpu.roll(x, 96, 1), pltpu.roll(x, 32, 1))
    return x * cos + swapped * sin_signed


def _rope_full128(x, cos, sin_signed):
    return x * cos + pltpu.roll(x, 64, 1) * sin_signed


def _splice_kernel(w_ref, o_ref, *, head_end, tail_start, gap):
    w = w_ref[0]
    parts = [w[:, :head_end], w[:, tail_start:]]
    if gap:
        parts.insert(1, jnp.zeros((w.shape[0], gap), w.dtype))
    o_ref[...] = jnp.concatenate(parts, axis=1).astype(o_ref.dtype)


def _splice_columns(w, head_end, tail_start, gap, tr=128):
    _, rows, cols = w.shape
    total = head_end + gap + cols - tail_start
    return pl.pallas_call(
        functools.partial(_splice_kernel, head_end=head_end, tail_start=tail_start, gap=gap),
        grid=(rows // tr,),
        in_specs=[pl.BlockSpec((1, tr, cols), lambda i: (0, i, 0))],
        out_specs=pl.BlockSpec((tr, total), lambda i: (i, 0)),
        out_shape=jax.ShapeDtypeStruct((rows, total), BF16),
        compiler_params=_cparams(("parallel",)),
        name="regroup_weight",
    )(w)


def _silu(x):
    return x * jax.nn.sigmoid(x)


def _rope_tables_kernel(pos_ref, inv_ref, sgn_ref, c64_ref, s64_ref, c128_ref, s128_ref):
    ang = pos_ref[0] * inv_ref[...]
    cos, sin = jnp.cos(ang), jnp.sin(ang)
    c32, s32 = cos[:, :32], sin[:, :32]
    c64, s64 = cos[:, 32:96], sin[:, 32:96]
    c64_ref[0] = jnp.concatenate([c32] * 4, axis=1)
    s64_ref[0] = jnp.concatenate([s32] * 4, axis=1) * sgn_ref[0:1, :]
    c128_ref[0] = jnp.concatenate([c64] * 2, axis=1)
    s128_ref[0] = jnp.concatenate([s64] * 2, axis=1) * sgn_ref[1:2, :]


def _rope_tables(positions):
    bsz, seq = positions.shape
    ts = 512
    inv32 = ROPE_THETA ** (-jnp.arange(0, 64, 2, dtype=F32) / 64)
    inv64 = ROPE_THETA ** (-jnp.arange(0, 128, 2, dtype=F32) / 128)
    inv = jnp.concatenate([inv32, inv64, jnp.zeros((32,), F32)])[None, :]
    sgn = jnp.asarray(np.stack([np.tile(np.repeat([-1.0, 1.0], 32), 2), np.repeat([-1.0, 1.0], 64)]), F32)
    pos = positions.astype(F32)[..., None]
    tab = jax.ShapeDtypeStruct((bsz, seq, LANES), F32)
    tspec = pl.BlockSpec((1, ts, LANES), lambda b, s: (b, s, 0))
    return pl.pallas_call(
        _rope_tables_kernel,
        grid=(bsz, seq // ts),
        in_specs=[pl.BlockSpec((1, ts, 1), lambda b, s: (b, s, 0)), _const_spec((1, LANES)), _const_spec((2, LANES))],
        out_specs=[tspec] * 4,
        out_shape=[tab] * 4,
        compiler_params=_cparams(("parallel", "parallel")),
        name="rope_tables",
    )(pos, inv, sgn)


def _rms(x, w):
    return x * lax.rsqrt(jnp.mean(x * x, axis=-1, keepdims=True) + RMS_EPS) * w


IN0_COLS = np.cumsum((0, Q_LORA, KV_LORA, 2 * MLA_ROPE, 3 * DIL_HEADS * DIL_HD, MLA_HEADS * MLA_V + DIL_HEADS * DIL_HD))


def _in0_kernel(x_ref, w_ref, wuq_ref, wukv_ref, qnorm_ref, kvnorm_ref,
                c64_ref, s64_ref, c128_ref, s128_ref,
                qn_ref, qr_ref, kv_ref, kr_ref, qb_ref, kb_ref, vb_ref, g_ref):
    x = x_ref[0].astype(BF16)
    c64, s64, c128, s128 = c64_ref[0], s64_ref[0], c128_ref[0], s128_ref[0]

    def proj(i):
        return _dot(x, w_ref[:, IN0_COLS[i]:IN0_COLS[i + 1]])

    cq = _rms(proj(0), qnorm_ref[...])
    qa = _dot(cq.astype(BF16), wuq_ref[...]) * ((MLA_NOPE + MLA_ROPE) ** -0.5 * LOG2E)
    nope = MLA_HEADS * MLA_NOPE
    qn_ref[0] = qa[:, :nope].astype(BF16)
    for j in range(MLA_HEADS * MLA_ROPE // LANES):
        blk = qa[:, nope + j * LANES:nope + (j + 1) * LANES]
        qr_ref[0, :, j * LANES:(j + 1) * LANES] = _rope_pairs64(blk, c64, s64).astype(BF16)
    ckv = _rms(proj(1), kvnorm_ref[...])
    kv_ref[0] = _dot(ckv.astype(BF16), wukv_ref[...]).astype(BF16)
    kr_ref[0] = _rope_pairs64(proj(2), c64, s64).astype(BF16)
    hb = proj(3)
    width = DIL_HEADS * DIL_HD
    for h in range(DIL_HEADS):
        sl = slice(h * DIL_HD, (h + 1) * DIL_HD)
        qb_ref[0, :, sl] = _rope_full128(hb[:, sl], c128, s128) * (DIL_HD ** -0.5 * LOG2E)
        kb_ref[0, :, sl] = _rope_full128(hb[:, width + h * DIL_HD:width + (h + 1) * DIL_HD], c128, s128)
    vb_ref[0] = hb[:, 2 * width:]
    g_ref[0] = _silu(proj(4)).astype(BF16)


def _in0(x, tabs, w_in, q_norm, w_uq, kv_norm, w_ukv, tm=512):
    bsz, seq, _ = x.shape
    c64, s64, c128, s128 = tabs
    kr0 = Q_LORA + KV_LORA
    w_all = _splice_columns(w_in, kr0 + MLA_ROPE, kr0, 0)
    wuq3 = w_uq.reshape(Q_LORA, MLA_HEADS, MLA_NOPE + MLA_ROPE)
    wuq = jnp.concatenate([wuq3[:, :, :MLA_NOPE].reshape(Q_LORA, -1), wuq3[:, :, MLA_NOPE:].reshape(Q_LORA, -1)],
                          axis=1).astype(BF16)
    wukv = w_ukv.astype(BF16)
    weights = (w_all, wuq, wukv, q_norm.reshape(1, -1), kv_norm.reshape(1, -1))

    def rows(width):
        return pl.BlockSpec((1, tm, width), lambda b, s: (b, s, 0))

    out_widths = (1024, 512, 2048, LANES, 1024, 1024, 1024, 2048)
    out_dtypes = (BF16, BF16, BF16, BF16, F32, F32, F32, BF16)
    return pl.pallas_call(
        _in0_kernel,
        grid=(bsz, seq // tm),
        in_specs=[rows(D_MODEL)] + [_const_spec(w.shape) for w in weights] + [rows(LANES)] * 4,
        out_specs=[rows(w) for w in out_widths],
        out_shape=[jax.ShapeDtypeStruct((bsz, seq, w), dt) for w, dt in zip(out_widths, out_dtypes)],
        compiler_params=_cparams(("parallel", "parallel")),
        name="in_proj0",
    )(x, *weights, c64, s64, c128, s128)


MLA_SUB = 256
MLA_ONES_ROWS = 16
MLA_AHEAD = 5
LOG2E = 1.4426950408889634


def _mla_kernel(qn_ref, qr_ref, kn_ref, kr_ref, v_ref, o_ref, k_s, vt_s):
    h = pl.program_id(1)

    @pl.when(pl.program_id(2) == 0)
    def _():
        k_s[:, :MLA_NOPE] = kn_ref[0]
        k_s[:, MLA_NOPE:] = kr_ref[0]
        vt_s[:MLA_V, :] = v_ref[0].astype(F32).T.astype(BF16)
        vt_s[MLA_V:, :] = jnp.ones((MLA_ONES_ROWS, vt_s.shape[1]), BF16)

    lane = lax.broadcasted_iota(jnp.int32, (MLA_SUB, LANES), 1)
    mine = (lane // MLA_ROPE) == (h % 2)
    nsub = qn_ref.shape[1] // MLA_SUB

    def scores(i):
        rows = slice(i * MLA_SUB, (i + 1) * MLA_SUB)
        qr = jnp.where(mine, qr_ref[0, rows, :], jnp.zeros((MLA_SUB, LANES), BF16))
        return _dot_nt(k_s[...], jnp.concatenate([qn_ref[0, rows, :], qr], axis=1))

    ahead = [scores(i) for i in range(min(MLA_AHEAD, nsub))]
    for i in range(nsub):
        s = ahead.pop(0)
        if i + MLA_AHEAD < nsub:
            ahead.append(scores(i + MLA_AHEAD))
        m = jnp.max(s, axis=0, keepdims=True)
        p = jnp.exp2(s - m).astype(BF16)
        ot = _dot(vt_s[...], p)
        o = ot[:MLA_V] / ot[MLA_V:MLA_V + 1]
        o_ref[0, i * MLA_SUB:(i + 1) * MLA_SUB, :] = o.T.astype(o_ref.dtype)


def _mla(qn, qr, kv, kr, tq=2048):
    bsz, seq, _ = qn.shape
    return pl.pallas_call(
        _mla_kernel,
        grid=(bsz, MLA_HEADS, seq // tq),
        in_specs=[
            pl.BlockSpec((1, tq, LANES), lambda b, h, i: (b, i, h)),
            pl.BlockSpec((1, tq, LANES), lambda b, h, i: (b, i, h // 2)),
            pl.BlockSpec((1, seq, LANES), lambda b, h, i: (b, 0, 2 * h)),
            pl.BlockSpec((1, seq, LANES), lambda b, h, i: (b, 0, 0)),
            pl.BlockSpec((1, seq, LANES), lambda b, h, i: (b, 0, 2 * h + 1)),
        ],
        out_specs=pl.BlockSpec((1, tq, LANES), lambda b, h, i: (b, i, h)),
        out_shape=jax.ShapeDtypeStruct((bsz, seq, MLA_HEADS * MLA_V), BF16),
        scratch_shapes=[pltpu.VMEM((seq, MLA_NOPE + LANES), BF16), pltpu.VMEM((MLA_V + MLA_ONES_ROWS, seq), BF16)],
        compiler_params=_cparams(("parallel", "parallel", "arbitrary")),
        name="mla_attention",
    )(qn, qr, kv, kr, kv)


DIL_TQ = 128
DIL_RADIUS = 64
DIL_TILES_PER_STEP = 16


def _dil_bias_table():
    i = np.arange(DIL_TQ)[:, None]
    j = np.arange(2 * DIL_TQ)[None, :]
    tabs = [np.where(np.abs(j - off - i) <= DIL_RADIUS, 0.0, -np.inf) for off in (0, DIL_RADIUS, 2 * DIL_RADIUS)]
    return jnp.asarray(np.stack(tabs), F32)


def _dil_kernel(q_ref, k_ref, v_ref, bias_ref, o_ref, qd_s, kd_s, vd_s, m_s, l_s, acc_s):
    seq = q_ref.shape[1]
    vd_s[:, :, DIL_HD:] = jnp.ones((len(DILATED_GROUPS), seq, DIL_HD), BF16)

    def rows_of(start, size, dil):
        return pl.ds(start, size) if dil == 1 else pl.ds(start, size, stride=dil)

    for g, (_, dil) in enumerate(DILATED_GROUPS):
        length = seq // dil
        for src, dst in ((q_ref, qd_s), (k_ref, kd_s), (v_ref, vd_s)):
            for r in range(dil):
                dst[g, r * length:(r + 1) * length, :DIL_HD] = src[0, rows_of(r, length, dil), :].astype(BF16)

    def tiles_step(g, dil, which, order):
        length = seq // dil
        tiles = length // DIL_TQ
        kw = min(length, DIL_TQ + 2 * DIL_RADIUS)
        n = len(which)
        ws = [jnp.clip(t * DIL_TQ - DIL_RADIUS, 0, length - kw) for _, t in which]
        qrows = [pl.ds(pl.multiple_of(r * length + t * DIL_TQ, DIL_TQ), DIL_TQ) for r, t in which]
        krows = [pl.ds(pl.multiple_of(r * length + ws[i], DIL_RADIUS), kw) for i, (r, _) in enumerate(which)]
        if dil == 1:
            out_rows = qrows
        else:
            out_rows = [rows_of(r + t * (DIL_TQ * dil), DIL_TQ, dil) for r, t in which]
        if tiles == 1:
            bias = [bias_ref[0, :, :kw] for _ in which]
        else:
            bias = [bias_ref[jnp.where(t == 0, 0, jnp.where(t == tiles - 1, 2, 1))] for _, t in which]
        s = [_dot_nt(qd_s[g, qrows[i], :], kd_s[g, krows[i], :]) + bias[i] for i in range(n)]
        mk = [jnp.max(t, axis=-1, keepdims=True) for t in s]
        m = [jnp.broadcast_to(t, (DIL_TQ, DIL_HD)) for t in mk]
        p = [jnp.exp2(s[i] - mk[i]).astype(BF16) for i in range(n)]
        ol = [_dot(p[i], vd_s[g, krows[i], :]) for i in range(n)]
        o = [t[:, :DIL_HD] for t in ol]
        l = [t[:, DIL_HD:] for t in ol]
        for i in range(n):
            if order == 0:
                m_s[out_rows[i], :] = m[i]
                l_s[out_rows[i], :] = l[i]
                acc_s[out_rows[i], :] = o[i]
                continue
            m_old = m_s[out_rows[i], :]
            m_new = jnp.maximum(m_old, m[i])
            a_old = jnp.exp2(m_old - m_new)
            a_new = jnp.exp2(m[i] - m_new)
            l_new = a_old * l_s[out_rows[i], :] + a_new * l[i]
            acc_new = a_old * acc_s[out_rows[i], :] + a_new * o[i]
            if order == len(DILATED_GROUPS) - 1:
                o_ref[0, out_rows[i], :] = (acc_new / l_new).astype(o_ref.dtype)
            else:
                m_s[out_rows[i], :] = m_new
                l_s[out_rows[i], :] = l_new
                acc_s[out_rows[i], :] = acc_new

    for order, g in enumerate(reversed(range(len(DILATED_GROUPS)))):
        window, dil = DILATED_GROUPS[g]
        assert window // (2 * dil) == DIL_RADIUS
        assert order < len(DILATED_GROUPS) - 1 or dil == 1
        tiles = seq // dil // DIL_TQ

        def body(it, carry, g=g, dil=dil, tiles=tiles, order=order):
            idx = [it * DIL_TILES_PER_STEP + j for j in range(DIL_TILES_PER_STEP)]
            tiles_step(g, dil, [(i // tiles, i % tiles) for i in idx], order)
            return carry

        lax.fori_loop(0, dil * tiles // DIL_TILES_PER_STEP, body, 0)


def _dilated(qb, kb, vb):
    bsz, seq, _ = qb.shape
    spec = pl.BlockSpec((1, seq, DIL_HD), lambda b, h: (b, 0, h))
    bias = _dil_bias_table()
    grouped = pltpu.VMEM((len(DILATED_GROUPS), seq, DIL_HD), BF16)
    stat = pltpu.VMEM((seq, DIL_HD), F32)
    return pl.pallas_call(
        _dil_kernel,
        grid=(bsz, DIL_HEADS),
        in_specs=[spec] * 3 + [_const_spec(bias.shape)],
        out_specs=spec,
        out_shape=jax.ShapeDtypeStruct((bsz, seq, DIL_HEADS * DIL_HD), BF16),
        scratch_shapes=[grouped, grouped, pltpu.VMEM((len(DILATED_GROUPS), seq, 2 * DIL_HD), BF16), stat, stat, stat],
        compiler_params=_cparams(("parallel", "parallel")),
        name="dilated_attention",
    )(qb, kb, vb, bias)


OUT_SUB = 128


def _out_kernel(o1_ref, o2_ref, g_ref, x_ref, w_ref, lnw_ref, lnb_ref, out_ref):
    half = o1_ref.shape[-1]
    for i in range(o1_ref.shape[1] // OUT_SUB):
        rows = slice(i * OUT_SUB, (i + 1) * OUT_SUB)
        y1 = o1_ref[0, rows, :] * g_ref[0, rows, :half]
        y2 = o2_ref[0, rows, :] * g_ref[0, rows, half:]
        r = DEEPNORM_ALPHA * x_ref[0, rows, :] + _dot(y1, w_ref[:half, :]) + _dot(y2, w_ref[half:, :])
        mu = jnp.mean(r, axis=-1, keepdims=True)
        rc = r - mu
        var = jnp.mean(rc * rc, axis=-1, keepdims=True)
        out_ref[0, rows, :] = rc * lax.rsqrt(var + LN_EPS) * lnw_ref[...] + lnb_ref[...]


def _out_proj(o1, o2, gate, x, w_out, ln_w, ln_b, tm=512):
    bsz, seq, _ = x.shape
    half = o1.shape[-1]

    def rows(width):
        return pl.BlockSpec((1, tm, width), lambda b, s: (b, s, 0))

    w = w_out.astype(BF16)
    return pl.pallas_call(
        _out_kernel,
        grid=(bsz, seq // tm),
        in_specs=[rows(half), rows(half), rows(2 * half), rows(D_MODEL), _const_spec(w.shape),
                  _const_spec((1, D_MODEL)), _const_spec((1, D_MODEL))],
        out_specs=rows(D_MODEL),
        out_shape=jax.ShapeDtypeStruct((bsz, seq, D_MODEL), F32),
        compiler_params=_cparams(("parallel", "parallel")),
        name="out_proj",
    )(o1, o2, gate, x, w, ln_w.reshape(1, -1), ln_b.reshape(1, -1))


def _layer0(x, tabs, w_in, q_norm, w_uq, kv_norm, w_ukv, w_out, ln_w, ln_b):
    qn, qr, kv, kr, qb, kb, vb, gate = _in0(x, tabs, w_in, q_norm, w_uq, kv_norm, w_ukv)
    o_a = _mla(qn, qr, kv, kr)
    o_b = _dilated(qb, kb, vb)
    return _out_proj(o_a, o_b, gate, x, w_out, ln_w, ln_b)


GATE_BETA = 0
GATE_GC = 2 * GDN_HEADS


def _softplus(x):
    return jnp.maximum(x, 0.0) + jnp.log1p(jnp.exp(-jnp.abs(x)))


IN1_COLS = np.cumsum((0, 2 * GDN_HEADS * GDN_DK + GDN_HEADS * GDN_DV, LANES, 2 * RET_HEADS * RET_DK,
                      RET_HEADS * RET_DV, GDN_HEADS * GDN_DV + RET_HEADS * RET_DV))


def _in1_kernel(x_ref, w_ref, gp_ref, c64_ref, s64_ref,
                qkvc_ref, gate_ref, qr_ref, kr_ref, vr_ref, g_ref):
    x = x_ref[0].astype(BF16)
    tm = x.shape[0]
    c64, s64 = c64_ref[0], s64_ref[0]

    def proj(i):
        return _dot(x, w_ref[:, IN1_COLS[i]:IN1_COLS[i + 1]])

    qkvc_ref[0] = proj(0).astype(qkvc_ref.dtype)
    hg = proj(1)
    beta = jax.nn.sigmoid(hg)
    g = -jnp.exp(gp_ref[0:1, :]) * _softplus(hg + gp_ref[1:2, :])
    pos = lax.broadcasted_iota(jnp.int32, hg.shape, 0) % GDN_CHUNK
    pre, suf = g, g
    step = 1
    while step < GDN_CHUNK:
        pre = pre + jnp.where(pos >= step, pltpu.roll(pre, step, 0), 0.0)
        suf = suf + jnp.where(pos < GDN_CHUNK - step, pltpu.roll(suf, tm - step, 0), 0.0)
        step *= 2
    lane = lax.broadcasted_iota(jnp.int32, hg.shape, 1)
    gc = jnp.where(lane < GATE_GC + GDN_HEADS, pre, suf)
    gate_ref[0] = jnp.where(lane < GATE_GC, beta, gc)
    hqk = proj(2)
    width = RET_HEADS * RET_DK
    for j in range(width // LANES):
        qr_ref[0, :, j * LANES:(j + 1) * LANES] = _rope_pairs64(hqk[:, j * LANES:(j + 1) * LANES], c64, s64).astype(BF16)
        kblk = hqk[:, width + j * LANES:width + (j + 1) * LANES]
        kr_ref[0, :, j * LANES:(j + 1) * LANES] = (_rope_pairs64(kblk, c64, s64) * (RET_DK ** -0.5)).astype(BF16)
    vr_ref[0] = proj(3).astype(BF16)
    g_ref[0] = _silu(proj(4)).astype(BF16)


def _in1(x, tabs, w_in, a_log, dt_bias, tm=512):
    bsz, seq, _ = x.shape
    c64, s64, _, _ = tabs
    conv_ch = 2 * GDN_HEADS * GDN_DK + GDN_HEADS * GDN_DV
    gate_end = conv_ch + 4 * GDN_HEADS
    w_all = _splice_columns(w_in, gate_end, gate_end, LANES - 4 * GDN_HEADS)
    pad = (GATE_GC, LANES - GATE_GC - 2 * GDN_HEADS)
    gp = jnp.stack([jnp.pad(a_log.reshape(-1), pad), jnp.pad(dt_bias.reshape(-1), pad)]).astype(F32)
    weights = (w_all, gp)

    def rows(width):
        return pl.BlockSpec((1, tm, width), lambda b, s: (b, s, 0))

    out_widths = (conv_ch, LANES, 512, 512, 1024, 2048)
    out_dtypes = (BF16, F32, BF16, BF16, BF16, BF16)
    return pl.pallas_call(
        _in1_kernel,
        grid=(bsz, seq // tm),
        in_specs=[rows(D_MODEL)] + [_const_spec(w.shape) for w in weights] + [rows(LANES)] * 2,
        out_specs=[rows(w) for w in out_widths],
        out_shape=[jax.ShapeDtypeStruct((bsz, seq, w), dt) for w, dt in zip(out_widths, out_dtypes)],
        compiler_params=_cparams(("parallel", "parallel")),
        name="in_proj1",
    )(x, *weights, c64, s64)


CONV_PAD = 8
GDN_P1_CHUNKS = 8
GDN_LEVELS = 7


def _gdn_level_masks():
    i = np.arange(GDN_CHUNK)[:, None]
    j = np.arange(GDN_CHUNK)[None, :]
    masks = [((i >> (k + 1)) == (j >> (k + 1))) & ((i >> k) != (j >> k)) for k in range(GDN_LEVELS)]
    return jnp.asarray(np.stack(masks), F32)


GDN_HPS = 1
CONV_ROWS = 256
G_BETA, G_GC = 0, 2


def _gdn2_kernel(qraw_ref, kraw_ref, vraw_ref, wq_ref, wk_ref, wv_ref, grow_ref, norm_ref, lvl_ref, o_ref,
                 xpad_s, q_s, k_s, v_s, w_s, u_s, att_s, qg_s, mt_s, bt_s, gl_s, od_s):
    seq = q_s.shape[1]
    nchunk = seq // GDN_CHUNK

    zeros = jnp.zeros((CONV_PAD, LANES), F32)
    xpad_s[0:CONV_PAD, :] = zeros
    xpad_s[CONV_PAD + seq:, :] = zeros

    def conv_silu(raw_ref, w_ref, lanes, dst, finish):
        xpad_s[CONV_PAD:CONV_PAD + seq, :] = raw_ref[0, :, lanes].astype(F32)
        for blk in range(seq // CONV_ROWS):
            acc = None
            for j in range(CONV_K):
                off = blk * CONV_ROWS + CONV_PAD - (CONV_K - 1) // 2 + j
                term = xpad_s[off:off + CONV_ROWS, :] * w_ref[j:j + 1, lanes]
                acc = term if acc is None else acc + term
            dst[blk * CONV_ROWS:(blk + 1) * CONV_ROWS, :] = finish(_silu(acc))

    def l2norm(t):
        return t * lax.rsqrt(jnp.sum(t * t, axis=-1, keepdims=True) + L2_EPS)

    for hh in range(GDN_HPS):
        lanes = slice(hh * LANES, (hh + 1) * LANES)
        conv_silu(qraw_ref, wq_ref, lanes, q_s.at[hh], lambda t: l2norm(t) * (GDN_DK ** -0.5))
        conv_silu(kraw_ref, wk_ref, lanes, k_s.at[hh], l2norm)
        conv_silu(vraw_ref, wv_ref, lanes, v_s.at[hh], lambda t: t)

    ii = lax.broadcasted_iota(jnp.int32, (GDN_CHUNK, GDN_CHUNK), 0)
    jj = lax.broadcasted_iota(jnp.int32, (GDN_CHUNK, GDN_CHUNK), 1)
    causal = (ii >= jj, ii <= jj)
    strict = (ii > jj, ii < jj)

    def chunk_rows(c):
        return pl.ds(pl.multiple_of(c * GDN_CHUNK, GDN_CHUNK), GDN_CHUNK)

    def token_major(row):
        return jnp.broadcast_to(row, (LANES, GDN_CHUNK)).T

    def phase1(it, carry, hh):
        chains = [(it * GDN_P1_CHUNKS + j, d) for j in range(GDN_P1_CHUNKS) for d in range(2)]
        n_ch = len(chains)
        rows = [chunk_rows(c) for c, _ in chains]
        kf = [k_s[hh, r, :] for r in rows]
        kc = [t.astype(BF16) for t in kf]
        grow = [grow_ref[0, hh, G_GC + d, pl.ds(c, 1), :] for c, d in chains]
        gcb = [token_major(t) for t in grow]
        bcb = [token_major(grow_ref[0, hh, G_BETA + d, pl.ds(c, 1), :]) for c, d in chains]
        kb = [kf[i] * bcb[i] for i in range(n_ch)]
        kk = [_dot_nt(kb[i].astype(BF16), kc[i]) for i in range(n_ch)]
        decay = [jnp.exp(jnp.where(causal[d], gcb[i] - grow[i], -jnp.inf)) for i, (_, d) in enumerate(chains)]
        a = [jnp.where(strict[d], kk[i] * decay[i], 0.0) for i, (_, d) in enumerate(chains)]
        n = [-(t * lvl_ref[0]) for t in a]
        for lvl in range(1, GDN_LEVELS):
            ak = [t * lvl_ref[lvl] for t in a]
            y = [ak[i] + _dot(ak[i].astype(BF16), n[i].astype(BF16)) for i in range(n_ch)]
            n = [n[i] - y[i] - _dot(n[i].astype(BF16), y[i].astype(BF16)) for i in range(n_ch)]
        nb = [t.astype(BF16) for t in n]
        egc = [jnp.exp(t) for t in gcb]
        wpre = [kb[i] * egc[i] for i in range(n_ch)]
        upre = [v_s[hh, rows[i], :] * bcb[i] for i in range(n_ch)]
        w = [(wpre[i] + _dot(nb[i], wpre[i].astype(BF16))).astype(BF16) for i in range(n_ch)]
        u = [upre[i] + _dot(nb[i], upre[i].astype(BF16)) for i in range(n_ch)]
        qf = [q_s[hh, rows[2 * j], :] for j in range(GDN_P1_CHUNKS)]
        qkc = [_dot_nt(qf[j].astype(BF16), kc[2 * j]) for j in range(GDN_P1_CHUNKS)]
        gl = [gcb[i][GDN_CHUNK - 1:GDN_CHUNK, :] if d == 0 else gcb[i][0:1, :] for i, (_, d) in enumerate(chains)]
        kd = [(kf[i] * jnp.exp(gl[i] - gcb[i])).astype(BF16) for i in range(n_ch)]
        mt = [_dot_tn(w[i], kd[i]) for i in range(n_ch)]
        bt = [_dot_tn(u[i].astype(BF16), kd[i]) for i in range(n_ch)]
        for i, (c, d) in enumerate(chains):
            r = rows[i]
            w_s[hh, d, r, :] = w[i]
            u_s[hh, d, r, :] = u[i]
            att_s[hh, d, r, :] = (qkc[i // 2] * decay[i]).astype(BF16)
            qg_s[hh, d, r, :] = (qf[i // 2] * egc[i]).astype(BF16)
            mt_s[hh, d, c] = (-mt[i]).astype(BF16)
            bt_s[hh, d, c] = bt[i]
            gl_s[hh, d, c] = jnp.broadcast_to(jnp.exp(gl[i]), (8, LANES))
        return carry

    for hh in range(GDN_HPS):
        lax.fori_loop(0, nchunk // GDN_P1_CHUNKS, functools.partial(phase1, hh=hh), 0)

    chains = [(hh, d) for hh in range(GDN_HPS) for d in range(2)]
    state = [jnp.zeros((GDN_DV, GDN_DK), F32) for _ in chains]
    def finish(pending):
        for (hh, d), r, m in pending:
            vnew = (u_s[hh, d, r, :] - m[:GDN_CHUNK]).astype(BF16)
            od_s[hh, d, r, :] = m[GDN_CHUNK:] + _dot(att_s[hh, d, r, :], vnew)

    pending = []
    for i in range(nchunk):
        cs = [i if d == 0 else nchunk - 1 - i for _, d in chains]
        rows = [slice(c * GDN_CHUNK, (c + 1) * GDN_CHUNK) for c in cs]
        stb = [t.astype(BF16) for t in state]
        state = [state[j] * gl_s[hh, d, cs[j], 0:1, :] + _dot(stb[j], mt_s[hh, d, cs[j]]) + bt_s[hh, d, cs[j]]
                 for j, (hh, d) in enumerate(chains)]
        finish(pending)
        pending = [(chains[j], rows[j],
                    _dot_nt(jnp.concatenate([w_s[hh, d, rows[j], :], qg_s[hh, d, rows[j], :]], axis=0), stb[j]))
                   for j, (hh, d) in enumerate(chains)]
    finish(pending)

    for hh in range(GDN_HPS):
        for blk in range(seq // CONV_ROWS):
            r = slice(blk * CONV_ROWS, (blk + 1) * CONV_ROWS)
            o = od_s[hh, 0, r, :] + od_s[hh, 1, r, :]
            o_ref[0, r, hh * LANES:(hh + 1) * LANES] = (
                o * lax.rsqrt(jnp.mean(o * o, axis=-1, keepdims=True) + RMS_EPS) * norm_ref[...]).astype(o_ref.dtype)


def _gdn2(qkvc, conv_w, grow, gdn_norm):
    bsz, seq, _ = qkvc.shape
    groups = GDN_HEADS // GDN_HPS
    width = GDN_HPS * LANES
    nchunk = seq // GDN_CHUNK

    def col(off):
        return pl.BlockSpec((1, seq, width), lambda b, g: (b, 0, off + g))

    def wcol(off):
        return pl.BlockSpec((CONV_K, width), lambda b, g: (0, off + g))

    def per_head(shape, dtype):
        return pltpu.VMEM((GDN_HPS,) + shape, dtype)

    return pl.pallas_call(
        _gdn2_kernel,
        grid=(bsz, groups),
        in_specs=[col(0), col(groups), col(2 * groups), wcol(0), wcol(groups), wcol(2 * groups),
                  pl.BlockSpec((1, GDN_HPS, 4, nchunk, GDN_CHUNK), lambda b, g: (b, g, 0, 0, 0)),
                  _const_spec((1, GDN_DV)), _const_spec((GDN_LEVELS, GDN_CHUNK, GDN_CHUNK))],
        out_specs=pl.BlockSpec((1, seq, width), lambda b, g: (b, 0, g)),
        out_shape=jax.ShapeDtypeStruct((bsz, seq, GDN_HEADS * GDN_DV), BF16),
        scratch_shapes=[pltpu.VMEM((seq + 2 * CONV_PAD, LANES), F32),
                        per_head((seq, LANES), F32), per_head((seq, LANES), F32), per_head((seq, LANES), F32),
                        per_head((2, seq, LANES), BF16), per_head((2, seq, LANES), F32),
                        per_head((2, seq, LANES), BF16), per_head((2, seq, LANES), BF16),
                        per_head((2, nchunk, GDN_DV, GDN_DK), BF16), per_head((2, nchunk, GDN_DV, GDN_DK), F32),
                        per_head((2, nchunk, 8, LANES), F32), per_head((2, seq, LANES), F32)],
        compiler_params=_cparams(("parallel", "parallel")),
        name="gated_deltanet",
    )(qkvc, qkvc, qkvc, conv_w, conv_w, conv_w, grow, gdn_norm.reshape(1, -1), _gdn_level_masks())


def _ret_tables():
    heads = np.arange(RET_HEADS, dtype=np.float64)
    log_gamma = np.log1p(-(2.0 ** (-RET_DECAY_BASE - heads)))[:, None]
    idx = np.arange(RET_CHUNK, dtype=np.float64)
    dmat = np.exp(log_gamma[:, :, None] * np.abs(idx[:, None] - idx[None, :])[None])
    xi = np.exp(log_gamma * (idx + 1.0))
    zeta = np.exp(log_gamma * (RET_CHUNK - 1.0 - idx))
    gch = np.exp(log_gamma * RET_CHUNK)

    def cb(t):
        return jnp.asarray(np.broadcast_to(t[:, :, None], t.shape + (LANES,)), F32)

    return jnp.asarray(dmat, F32), cb(xi), cb(zeta), cb(gch)


def _ret_kernel(q_ref, k_ref, v_ref, d_ref, xi_ref, zeta_ref, gch_ref, nw_ref, nb_ref, o_ref, kvf_s, kvb_s):
    seq = q_ref.shape[1]
    nchunk = seq // RET_CHUNK
    h = pl.program_id(1)
    xi, zeta, gch = xi_ref[0], zeta_ref[0], gch_ref[0]
    lane = lax.broadcasted_iota(jnp.int32, (RET_CHUNK, LANES), 1)
    mine = (lane // RET_DK) == (h % 2)

    def rows(c):
        return slice(c * RET_CHUNK, (c + 1) * RET_CHUNK)

    for c in range(nchunk):
        kc = k_ref[0, rows(c), :].astype(F32)
        vc = v_ref[0, rows(c), :]
        kvf_s[c] = _dot_tn((kc * zeta).astype(BF16), vc)
        kvb_s[c] = _dot_tn((kc * xi).astype(BF16), vc)
    state = jnp.zeros((LANES, RET_DV), F32)
    for c in range(nchunk):
        nxt = state * gch + kvf_s[c]
        kvf_s[c] = state
        state = nxt
    state = jnp.zeros((LANES, RET_DV), F32)
    for c in reversed(range(nchunk)):
        nxt = state * gch + kvb_s[c]
        kvb_s[c] = state
        state = nxt
    for c in range(nchunk):
        q = jnp.where(mine, q_ref[0, rows(c), :], jnp.zeros((RET_CHUNK, LANES), BF16))
        qf = q.astype(F32)
        s = _dot_nt(q, k_ref[0, rows(c), :]) * d_ref[0]
        o = _dot(s.astype(BF16), v_ref[0, rows(c), :])
        o = o + _dot((qf * xi).astype(BF16), kvf_s[c].astype(BF16))
        o = o + _dot((qf * zeta).astype(BF16), kvb_s[c].astype(BF16))
        mu = jnp.mean(o, axis=-1, keepdims=True)
        oc = o - mu
        var = jnp.mean(oc * oc, axis=-1, keepdims=True)
        o_ref[0, rows(c), :] = (oc * lax.rsqrt(var + LN_EPS) * nw_ref[...] + nb_ref[...]).astype(o_ref.dtype)


def _retention(qr, kr, vr, norm_w, norm_b):
    bsz, seq, _ = vr.shape
    dmat, xi, zeta, gch = _ret_tables()
    pair = pl.BlockSpec((1, seq, LANES), lambda b, h: (b, 0, h // 2))
    head = pl.BlockSpec((1, seq, RET_DV), lambda b, h: (b, 0, h))

    def per_head(shape):
        return pl.BlockSpec((1,) + shape, lambda b, h: (h, 0, 0))

    vec = pl.BlockSpec((1, RET_DV), lambda b, h: (0, h))
    state = pltpu.VMEM((seq // RET_CHUNK, LANES, RET_DV), F32)
    return pl.pallas_call(
        _ret_kernel,
        grid=(bsz, RET_HEADS),
        in_specs=[pair, pair, head, per_head((RET_CHUNK, RET_CHUNK)), per_head((RET_CHUNK, LANES)),
                  per_head((RET_CHUNK, LANES)), per_head((1, LANES)), vec, vec],
        out_specs=head,
        out_shape=jax.ShapeDtypeStruct((bsz, seq, RET_HEADS * RET_DV), BF16),
        scratch_shapes=[state, state],
        compiler_params=_cparams(("parallel", "parallel")),
        name="retention",
    )(qr, kr, vr, dmat, xi, zeta, gch, norm_w.reshape(1, -1), norm_b.reshape(1, -1))


def _layer1(x, tabs, w_in, conv_w, a_log, dt_bias, gdn_norm, ret_norm_w, ret_norm_b, w_out, ln_w, ln_b):
    bsz, seq, _ = x.shape
    qkvc, gates, qr, kr, vr, gate = _in1(x, tabs, w_in, a_log, dt_bias)
    gsl = gates[:, :, :4 * GDN_HEADS].reshape(bsz, seq, 4, GDN_HEADS)
    grow = gsl.transpose(0, 3, 2, 1).reshape(bsz, GDN_HEADS, 4, seq // GDN_CHUNK, GDN_CHUNK)
    o_c = _gdn2(qkvc, conv_w, grow, gdn_norm)
    o_d = _retention(qr, kr, vr, ret_norm_w, ret_norm_b)
    return _out_proj(o_c, o_d, gate, x, w_out, ln_w, ln_b)


def kernel(x, positions, even_w_in, even_q_norm, even_w_uq, even_kv_norm, even_w_ukv, even_w_out, even_ln_w, even_ln_b, odd_w_in, odd_conv_w, odd_a_log, odd_dt_bias, odd_gdn_norm, odd_ret_norm_w, odd_ret_norm_b, odd_w_out, odd_ln_w, odd_ln_b):
    tabs = _rope_tables(positions)
    x = _layer0(x, tabs, even_w_in, even_q_norm[0], even_w_uq[0], even_kv_norm[0], even_w_ukv[0],
                even_w_out[0], even_ln_w[0], even_ln_b[0])
    x = _layer1(x, tabs, odd_w_in, odd_conv_w[0], odd_a_log[0], odd_dt_bias[0], odd_gdn_norm[0],
                odd_ret_norm_w[0], odd_ret_norm_b[0], odd_w_out[0], odd_ln_w[0], odd_ln_b[0])
    return x
```

```python
import functools

import numpy as np
import jax
import jax.numpy as jnp
from jax import lax
from jax.experimental import pallas as pl
from jax.experimental.pallas import tpu as pltpu

F32 = jnp.float32
BF16 = jnp.bfloat16

D_MODEL = 1024
DEPTH = 2
DEEPNORM_ALPHA = (2 * DEPTH) ** 0.25
ROPE_THETA = 10000.0
RMS_EPS = 1e-6
LN_EPS = 1e-5
L2_EPS = 1e-6

MLA_HEADS = 8
MLA_NOPE = 128
MLA_ROPE = 64
MLA_V = 128
Q_LORA = 768
KV_LORA = 256

DIL_HEADS = 8
DIL_HD = 128
DILATED_GROUPS = ((128, 1), (512, 4), (2048, 16))

GDN_HEADS = 8
GDN_DK = 128
GDN_DV = 128
CONV_K = 5
GDN_CHUNK = 128

RET_HEADS = 8
RET_DK = 64
RET_DV = 128
RET_DECAY_BASE = 5.0
RET_CHUNK = 256

LANES = 128
VMEM_LIMIT = 56 * 1024 * 1024


def _cparams(sem):
    return pltpu.CompilerParams(dimension_semantics=sem, vmem_limit_bytes=VMEM_LIMIT)


def _dot(a, b):
    return jnp.dot(a, b, preferred_element_type=F32)


def _dot_nt(a, b):
    return lax.dot_general(a, b, (((1,), (1,)), ((), ())), preferred_element_type=F32)


def _dot_tn(a, b):
    return lax.dot_general(a, b, (((0,), (0,)), ((), ())), preferred_element_type=F32)


def _const_spec(shape):
    nd = len(shape)
    return pl.BlockSpec(shape, lambda *_: (0,) * nd, pipeline_mode=pl.Buffered(1))


def _rope_pairs64(x, cos, sin_signed):
    lane = lax.broadcasted_iota(jnp.int32, x.shape, 1)
    swapped = jnp.where((lane % 64) < 32, pltpu.roll(x, 96, 1), pltpu.roll(x, 32, 1))
    return x * cos + swapped * sin_signed


def _rope_full128(x, cos, sin_signed):
    return x * cos + pltpu.roll(x, 64, 1) * sin_signed


def _splice_kernel(w_ref, o_ref, *, head_end, tail_start, gap):
    w = w_ref[0]
    parts = [w[:, :head_end], w[:, tail_start:]]
    if gap:
        parts.insert(1, jnp.zeros((w.shape[0], gap), w.dtype))
    o_ref[...] = jnp.concatenate(parts, axis=1).astype(o_ref.dtype)


def _splice_columns(w, head_end, tail_start, gap, tr=128):
    _, rows, cols = w.shape
    total = head_end + gap + cols - tail_start
    return pl.pallas_call(
        functools.partial(_splice_kernel, head_end=head_end, tail_start=tail_start, gap=gap),
        grid=(rows // tr,),
        in_specs=[pl.BlockSpec((1, tr, cols), lambda i: (0, i, 0))],
        out_specs=pl.BlockSpec((tr, total), lambda i: (i, 0)),
        out_shape=jax.ShapeDtypeStruct((rows, total), BF16),
        compiler_params=_cparams(("parallel",)),
        name="regroup_weight",
    )(w)


def _silu(x):
    return x * jax.nn.sigmoid(x)


def _rope_tables_kernel(pos_ref, inv_ref, sgn_ref, c64_ref, s64_ref, c128_ref, s128_ref):
    ang = pos_ref[0] * inv_ref[...]
    cos, sin = jnp.cos(ang), jnp.sin(ang)
    c32, s32 = cos[:, :32], sin[:, :32]
    c64, s64 = cos[:, 32:96], sin[:, 32:96]
    c64_ref[0] = jnp.concatenate([c32] * 4, axis=1)
    s64_ref[0] = jnp.concatenate([s32] * 4, axis=1) * sgn_ref[0:1, :]
    c128_ref[0] = jnp.concatenate([c64] * 2, axis=1)
    s128_ref[0] = jnp.concatenate([s64] * 2, axis=1) * sgn_ref[1:2, :]


def _rope_tables(positions):
    bsz, seq = positions.shape
    ts = 512
    inv32 = ROPE_THETA ** (-jnp.arange(0, 64, 2, dtype=F32) / 64)
    inv64 = ROPE_THETA ** (-jnp.arange(0, 128, 2, dtype=F32) / 128)
    inv = jnp.concatenate([inv32, inv64, jnp.zeros((32,), F32)])[None, :]
    sgn = jnp.asarray(np.stack([np.tile(np.repeat([-1.0, 1.0], 32), 2), np.repeat([-1.0, 1.0], 64)]), F32)
    pos = positions.astype(F32)[..., None]
    tab = jax.ShapeDtypeStruct((bsz, seq, LANES), F32)
    tspec = pl.BlockSpec((1, ts, LANES), lambda b, s: (b, s, 0))
    return pl.pallas_call(
        _rope_tables_kernel,
        grid=(bsz, seq // ts),
        in_specs=[pl.BlockSpec((1, ts, 1), lambda b, s: (b, s, 0)), _const_spec((1, LANES)), _const_spec((2, LANES))],
        out_specs=[tspec] * 4,
        out_shape=[tab] * 4,
        compiler_params=_cparams(("parallel", "parallel")),
        name="rope_tables",
    )(pos, inv, sgn)


def _rms(x, w):
    return x * lax.rsqrt(jnp.mean(x * x, axis=-1, keepdims=True) + RMS_EPS) * w


IN0_COLS = np.cumsum((0, Q_LORA, KV_LORA, 2 * MLA_ROPE, 3 * DIL_HEADS * DIL_HD, MLA_HEADS * MLA_V + DIL_HEADS * DIL_HD))


def _in0_kernel(x_ref, w_ref, wuq_ref, wukv_ref, qnorm_ref, kvnorm_ref,
                c64_ref, s64_ref, c128_ref, s128_ref,
                qn_ref, qr_ref, kv_ref, kr_ref, qb_ref, kb_ref, vb_ref, g_ref):
    x = x_ref[0].astype(BF16)
    c64, s64, c128, s128 = c64_ref[0], s64_ref[0], c128_ref[0], s128_ref[0]

    def proj(i):
        return _dot(x, w_ref[:, IN0_COLS[i]:IN0_COLS[i + 1]])

    cq = _rms(proj(0), qnorm_ref[...])
    qa = _dot(cq.astype(BF16), wuq_ref[...]) * ((MLA_NOPE + MLA_ROPE) ** -0.5 * LOG2E)
    nope = MLA_HEADS * MLA_NOPE
    qn_ref[0] = qa[:, :nope].astype(BF16)
    for j in range(MLA_HEADS * MLA_ROPE // LANES):
        blk = qa[:, nope + j * LANES:nope + (j + 1) * LANES]
        qr_ref[0, :, j * LANES:(j + 1) * LANES] = _rope_pairs64(blk, c64, s64).astype(BF16)
    ckv = _rms(proj(1), kvnorm_ref[...])
    kv_ref[0] = _dot(ckv.astype(BF16), wukv_ref[...]).astype(BF16)
    kr_ref[0] = _rope_pairs64(proj(2), c64, s64).astype(BF16)
    hb = proj(3)
    width = DIL_HEADS * DIL_HD
    for h in range(DIL_HEADS):
        sl = slice(h * DIL_HD, (h + 1) * DIL_HD)
        qb_ref[0, :, sl] = _rope_full128(hb[:, sl], c128, s128) * (DIL_HD ** -0.5 * LOG2E)
        kb_ref[0, :, sl] = _rope_full128(hb[:, width + h * DIL_HD:width + (h + 1) * DIL_HD], c128, s128)
    vb_ref[0] = hb[:, 2 * width:]
    g_ref[0] = _silu(proj(4)).astype(BF16)


def _in0(x, tabs, w_in, q_norm, w_uq, kv_norm, w_ukv, tm=512):
    bsz, seq, _ = x.shape
    c64, s64, c128, s128 = tabs
    kr0 = Q_LORA + KV_LORA
    w_all = _splice_columns(w_in, kr0 + MLA_ROPE, kr0, 0)
    wuq3 = w_uq.reshape(Q_LORA, MLA_HEADS, MLA_NOPE + MLA_ROPE)
    wuq = jnp.concatenate([wuq3[:, :, :MLA_NOPE].reshape(Q_LORA, -1), wuq3[:, :, MLA_NOPE:].reshape(Q_LORA, -1)],
                          axis=1).astype(BF16)
    wukv = w_ukv.astype(BF16)
    weights = (w_all, wuq, wukv, q_norm.reshape(1, -1), kv_norm.reshape(1, -1))

    def rows(width):
        return pl.BlockSpec((1, tm, width), lambda b, s: (b, s, 0))

    out_widths = (1024, 512, 2048, LANES, 1024, 1024, 1024, 2048)
    out_dtypes = (BF16, BF16, BF16, BF16, F32, F32, F32, BF16)
    return pl.pallas_call(
        _in0_kernel,
        grid=(bsz, seq // tm),
        in_specs=[rows(D_MODEL)] + [_const_spec(w.shape) for w in weights] + [rows(LANES)] * 4,
        out_specs=[rows(w) for w in out_widths],
        out_shape=[jax.ShapeDtypeStruct((bsz, seq, w), dt) for w, dt in zip(out_widths, out_dtypes)],
        compiler_params=_cparams(("parallel", "parallel")),
        name="in_proj0",
    )(x, *weights, c64, s64, c128, s128)


MLA_SUB = 256
MLA_ONES_ROWS = 16
MLA_AHEAD = 5
LOG2E = 1.4426950408889634


def _mla_kernel(qn_ref, qr_ref, kn_ref, kr_ref, v_ref, o_ref, k_s, vt_s):
    h = pl.program_id(1)

    @pl.when(pl.program_id(2) == 0)
    def _():
        k_s[:, :MLA_NOPE] = kn_ref[0]
        k_s[:, MLA_NOPE:] = kr_ref[0]
        vt_s[:MLA_V, :] = v_ref[0].astype(F32).T.astype(BF16)
        vt_s[MLA_V:, :] = jnp.ones((MLA_ONES_ROWS, vt_s.shape[1]), BF16)

    lane = lax.broadcasted_iota(jnp.int32, (MLA_SUB, LANES), 1)
    mine = (lane // MLA_ROPE) == (h % 2)
    nsub = qn_ref.shape[1] // MLA_SUB

    def scores(i):
        rows = slice(i * MLA_SUB, (i + 1) * MLA_SUB)
        qr = jnp.where(mine, qr_ref[0, rows, :], jnp.zeros((MLA_SUB, LANES), BF16))
        return _dot_nt(k_s[...], jnp.concatenate([qn_ref[0, rows, :], qr], axis=1))

    ahead = [scores(i) for i in range(min(MLA_AHEAD, nsub))]
    for i in range(nsub):
        s = ahead.pop(0)
        if i + MLA_AHEAD < nsub:
            ahead.append(scores(i + MLA_AHEAD))
        m = jnp.max(s, axis=0, keepdims=True)
        p = jnp.exp2(s - m).astype(BF16)
        ot = _dot(vt_s[...], p)
        o = ot[:MLA_V] / ot[MLA_V:MLA_V + 1]
        o_ref[0, i * MLA_SUB:(i + 1) * MLA_SUB, :] = o.T.astype(o_ref.dtype)


def _mla(qn, qr, kv, kr, tq=2048):
    bsz, seq, _ = qn.shape
    return pl.pallas_call(
        _mla_kernel,
        grid=(bsz, MLA_HEADS, seq // tq),
        in_specs=[
            pl.BlockSpec((1, tq, LANES), lambda b, h, i: (b, i, h)),
            pl.BlockSpec((1, tq, LANES), lambda b, h, i: (b, i, h // 2)),
            pl.BlockSpec((1, seq, LANES), lambda b, h, i: (b, 0, 2 * h)),
            pl.BlockSpec((1, seq, LANES), lambda b, h, i: (b, 0, 0)),
            pl.BlockSpec((1, seq, LANES), lambda b, h, i: (b, 0, 2 * h + 1)),
        ],
        out_specs=pl.BlockSpec((1, tq, LANES), lambda b, h, i: (b, i, h)),
        out_shape=jax.ShapeDtypeStruct((bsz, seq, MLA_HEADS * MLA_V), BF16),
        scratch_shapes=[pltpu.VMEM((seq, MLA_NOPE + LANES), BF16), pltpu.VMEM((MLA_V + MLA_ONES_ROWS, seq), BF16)],
        compiler_params=_cparams(("parallel", "parallel", "arbitrary")),
        name="mla_attention",
    )(qn, qr, kv, kr, kv)


DIL_TQ = 128
DIL_RADIUS = 64
DIL_TILES_PER_STEP = 16


def _dil_bias_table():
    i = np.arange(DIL_TQ)[:, None]
    j = np.arange(2 * DIL_TQ)[None, :]
    tabs = [np.where(np.abs(j - off - i) <= DIL_RADIUS, 0.0, -np.inf) for off in (0, DIL_RADIUS, 2 * DIL_RADIUS)]
    return jnp.asarray(np.stack(tabs), F32)


def _dil_kernel(q_ref, k_ref, v_ref, bias_ref, o_ref, qd_s, kd_s, vd_s, m_s, l_s, acc_s):
    seq = q_ref.shape[1]
    vd_s[:, :, DIL_HD:] = jnp.ones((len(DILATED_GROUPS), seq, DIL_HD), BF16)

    def rows_of(start, size, dil):
        return pl.ds(start, size) if dil == 1 else pl.ds(start, size, stride=dil)

    for g, (_, dil) in enumerate(DILATED_GROUPS):
        length = seq // dil
        for src, dst in ((q_ref, qd_s), (k_ref, kd_s), (v_ref, vd_s)):
            for r in range(dil):
                dst[g, r * length:(r + 1) * length, :DIL_HD] = src[0, rows_of(r, length, dil), :].astype(BF16)

    def tiles_step(g, dil, which, order):
        length = seq // dil
        tiles = length // DIL_TQ
        kw = min(length, DIL_TQ + 2 * DIL_RADIUS)
        n = len(which)
        ws = [jnp.clip(t * DIL_TQ - DIL_RADIUS, 0, length - kw) for _, t in which]
        qrows = [pl.ds(pl.multiple_of(r * length + t * DIL_TQ, DIL_TQ), DIL_TQ) for r, t in which]
        krows = [pl.ds(pl.multiple_of(r * length + ws[i], DIL_RADIUS), kw) for i, (r, _) in enumerate(which)]
        if dil == 1:
            out_rows = qrows
        else:
            out_rows = [rows_of(r + t * (DIL_TQ * dil), DIL_TQ, dil) for r, t in which]
        if tiles == 1:
            bias = [bias_ref[0, :, :kw] for _ in which]
        else:
            bias = [bias_ref[jnp.where(t == 0, 0, jnp.where(t == tiles - 1, 2, 1))] for _, t in which]
        s = [_dot_nt(qd_s[g, qrows[i], :], kd_s[g, krows[i], :]) + bias[i] for i in range(n)]
        mk = [jnp.max(t, axis=-1, keepdims=True) for t in s]
        m = [jnp.broadcast_to(t, (DIL_TQ, DIL_HD)) for t in mk]
        p = [jnp.exp2(s[i] - mk[i]).astype(BF16) for i in range(n)]
        ol = [_dot(p[i], vd_s[g, krows[i], :]) for i in range(n)]
        o = [t[:, :DIL_HD] for t in ol]
        l = [t[:, DIL_HD:] for t in ol]
        for i in range(n):
            if order == 0:
                m_s[out_rows[i], :] = m[i]
                l_s[out_rows[i], :] = l[i]
                acc_s[out_rows[i], :] = o[i]
                continue
            m_old = m_s[out_rows[i], :]
            m_new = jnp.maximum(m_old, m[i])
            a_old = jnp.exp2(m_old - m_new)
            a_new = jnp.exp2(m[i] - m_new)
            l_new = a_old * l_s[out_rows[i], :] + a_new * l[i]
            acc_new = a_old * acc_s[out_rows[i], :] + a_new * o[i]
            if order == len(DILATED_GROUPS) - 1:
                o_ref[0, out_rows[i], :] = (acc_new / l_new).astype(o_ref.dtype)
            else:
                m_s[out_rows[i], :] = m_new
                l_s[out_rows[i], :] = l_new
                acc_s[out_rows[i], :] = acc_new

    for order, g in enumerate(reversed(range(len(DILATED_GROUPS)))):
        window, dil = DILATED_GROUPS[g]
        assert window // (2 * dil) == DIL_RADIUS
        assert order < len(DILATED_GROUPS) - 1 or dil == 1
        tiles = seq // dil // DIL_TQ

        def body(it, carry, g=g, dil=dil, tiles=tiles, order=order):
            idx = [it * DIL_TILES_PER_STEP + j for j in range(DIL_TILES_PER_STEP)]
            tiles_step(g, dil, [(i // tiles, i % tiles) for i in idx], order)
            return carry

        lax.fori_loop(0, dil * tiles // DIL_TILES_PER_STEP, body, 0)


def _dilated(qb, kb, vb):
    bsz, seq, _ = qb.shape
    spec = pl.BlockSpec((1, seq, DIL_HD), lambda b, h: (b, 0, h))
    bias = _dil_bias_table()
    grouped = pltpu.VMEM((len(DILATED_GROUPS), seq, DIL_HD), BF16)
    stat = pltpu.VMEM((seq, DIL_HD), F32)
    return pl.pallas_call(
        _dil_kernel,
        grid=(bsz, DIL_HEADS),
        in_specs=[spec] * 3 + [_const_spec(bias.shape)],
        out_specs=spec,
        out_shape=jax.ShapeDtypeStruct((bsz, seq, DIL_HEADS * DIL_HD), BF16),
        scratch_shapes=[grouped, grouped, pltpu.VMEM((len(DILATED_GROUPS), seq, 2 * DIL_HD), BF16), stat, stat, stat],
        compiler_params=_cparams(("parallel", "parallel")),
        name="dilated_attention",
    )(qb, kb, vb, bias)


OUT_SUB = 256


def _out_kernel(o1_ref, o2_ref, g_ref, x_ref, w_ref, lnw_ref, lnb_ref, out_ref):
    half = o1_ref.shape[-1]
    for i in range(o1_ref.shape[1] // OUT_SUB):
        rows = slice(i * OUT_SUB, (i + 1) * OUT_SUB)
        y1 = o1_ref[0, rows, :] * g_ref[0, rows, :half]
        y2 = o2_ref[0, rows, :] * g_ref[0, rows, half:]
        r = DEEPNORM_ALPHA * x_ref[0, rows, :] + _dot(y1, w_ref[:half, :]) + _dot(y2, w_ref[half:, :])
        mu = jnp.mean(r, axis=-1, keepdims=True)
        rc = r - mu
        var = jnp.mean(rc * rc, axis=-1, keepdims=True)
        out_ref[0, rows, :] = rc * lax.rsqrt(var + LN_EPS) * lnw_ref[...] + lnb_ref[...]


def _out_proj(o1, o2, gate, x, w_out, ln_w, ln_b, tm=512):
    bsz, seq, _ = x.shape
    half = o1.shape[-1]

    def rows(width):
        return pl.BlockSpec((1, tm, width), lambda b, s: (b, s, 0))

    w = w_out.astype(BF16)
    return pl.pallas_call(
        _out_kernel,
        grid=(bsz, seq // tm),
        in_specs=[rows(half), rows(half), rows(2 * half), rows(D_MODEL), _const_spec(w.shape),
                  _const_spec((1, D_MODEL)), _const_spec((1, D_MODEL))],
        out_specs=rows(D_MODEL),
        out_shape=jax.ShapeDtypeStruct((bsz, seq, D_MODEL), F32),
        compiler_params=_cparams(("parallel", "parallel")),
        name="out_proj",
    )(o1, o2, gate, x, w, ln_w.reshape(1, -1), ln_b.reshape(1, -1))


def _layer0(x, tabs, w_in, q_norm, w_uq, kv_norm, w_ukv, w_out, ln_w, ln_b):
    qn, qr, kv, kr, qb, kb, vb, gate = _in0(x, tabs, w_in, q_norm, w_uq, kv_norm, w_ukv)
    o_a = _mla(qn, qr, kv, kr)
    o_b = _dilated(qb, kb, vb)
    return _out_proj(o_a, o_b, gate, x, w_out, ln_w, ln_b)


GATE_BETA = 0
GATE_GC = 2 * GDN_HEADS


def _softplus(x):
    return jnp.maximum(x, 0.0) + jnp.log1p(jnp.exp(-jnp.abs(x)))


IN1_COLS = np.cumsum((0, 2 * GDN_HEADS * GDN_DK + GDN_HEADS * GDN_DV, LANES, 2 * RET_HEADS * RET_DK,
                      RET_HEADS * RET_DV, GDN_HEADS * GDN_DV + RET_HEADS * RET_DV))


def _in1_kernel(x_ref, w_ref, gp_ref, c64_ref, s64_ref,
                qkvc_ref, gate_ref, qr_ref, kr_ref, vr_ref, g_ref):
    x = x_ref[0].astype(BF16)
    tm = x.shape[0]
    c64, s64 = c64_ref[0], s64_ref[0]

    def proj(i):
        return _dot(x, w_ref[:, IN1_COLS[i]:IN1_COLS[i + 1]])

    qkvc_ref[0] = proj(0).astype(qkvc_ref.dtype)
    hg = proj(1)
    beta = jax.nn.sigmoid(hg)
    g = -jnp.exp(gp_ref[0:1, :]) * _softplus(hg + gp_ref[1:2, :])
    pos = lax.broadcasted_iota(jnp.int32, hg.shape, 0) % GDN_CHUNK
    pre, suf = g, g
    step = 1
    while step < GDN_CHUNK:
        pre = pre + jnp.where(pos >= step, pltpu.roll(pre, step, 0), 0.0)
        suf = suf + jnp.where(pos < GDN_CHUNK - step, pltpu.roll(suf, tm - step, 0), 0.0)
        step *= 2
    lane = lax.broadcasted_iota(jnp.int32, hg.shape, 1)
    gc = jnp.where(lane < GATE_GC + GDN_HEADS, pre, suf)
    gate_ref[0] = jnp.where(lane < GATE_GC, beta, gc)
    hqk = proj(2)
    width = RET_HEADS * RET_DK
    for j in range(width // LANES):
        qr_ref[0, :, j * LANES:(j + 1) * LANES] = _rope_pairs64(hqk[:, j * LANES:(j + 1) * LANES], c64, s64).astype(BF16)
        kblk = hqk[:, width + j * LANES:width + (j + 1) * LANES]
        kr_ref[0, :, j * LANES:(j + 1) * LANES] = (_rope_pairs64(kblk, c64, s64) * (RET_DK ** -0.5)).astype(BF16)
    vr_ref[0] = proj(3).astype(BF16)
    g_ref[0] = _silu(proj(4)).astype(BF16)


def _in1(x, tabs, w_in, a_log, dt_bias, tm=512):
    bsz, seq, _ = x.shape
    c64, s64, _, _ = tabs
    conv_ch = 2 * GDN_HEADS * GDN_DK + GDN_HEADS * GDN_DV
    gate_end = conv_ch + 4 * GDN_HEADS
    w_all = _splice_columns(w_in, gate_end, gate_end, LANES - 4 * GDN_HEADS)
    pad = (GATE_GC, LANES - GATE_GC - 2 * GDN_HEADS)
    gp = jnp.stack([jnp.pad(a_log.reshape(-1), pad), jnp.pad(dt_bias.reshape(-1), pad)]).astype(F32)
    weights = (w_all, gp)

    def rows(width):
        return pl.BlockSpec((1, tm, width), lambda b, s: (b, s, 0))

    out_widths = (conv_ch, LANES, 512, 512, 1024, 2048)
    out_dtypes = (BF16, F32, BF16, BF16, BF16, BF16)
    return pl.pallas_call(
        _in1_kernel,
        grid=(bsz, seq // tm),
        in_specs=[rows(D_MODEL)] + [_const_spec(w.shape) for w in weights] + [rows(LANES)] * 2,
        out_specs=[rows(w) for w in out_widths],
        out_shape=[jax.ShapeDtypeStruct((bsz, seq, w), dt) for w, dt in zip(out_widths, out_dtypes)],
        compiler_params=_cparams(("parallel", "parallel")),
        name="in_proj1",
    )(x, *weights, c64, s64)


CONV_PAD = 8
GDN_P1_CHUNKS = 8
GDN_LEVELS = 7


def _gdn_level_masks():
    i = np.arange(GDN_CHUNK)[:, None]
    j = np.arange(GDN_CHUNK)[None, :]
    masks = [((i >> (k + 1)) == (j >> (k + 1))) & ((i >> k) != (j >> k)) for k in range(GDN_LEVELS)]
    return jnp.asarray(np.stack(masks), F32)


GDN_HPS = 1
CONV_ROWS = 256
G_BETA, G_GC = 0, 2


def _gdn2_kernel(qraw_ref, kraw_ref, vraw_ref, wq_ref, wk_ref, wv_ref, grow_ref, norm_ref, lvl_ref, o_ref,
                 xpad_s, q_s, k_s, v_s, w_s, u_s, att_s, qg_s, mt_s, bt_s, gl_s, od_s):
    seq = q_s.shape[1]
    nchunk = seq // GDN_CHUNK

    zeros = jnp.zeros((3, CONV_PAD, LANES), F32)
    xpad_s[:, 0:CONV_PAD, :] = zeros
    xpad_s[:, CONV_PAD + seq:, :] = zeros

    def l2norm(t):
        return t * lax.rsqrt(jnp.sum(t * t, axis=-1, keepdims=True) + L2_EPS)

    finishers = (lambda t: l2norm(t) * (GDN_DK ** -0.5), l2norm, lambda t: t)

    def conv_silu_rows(hh, row0, nrows):
        lanes = slice(hh * LANES, (hh + 1) * LANES)
        for a, (w_ref, dst) in enumerate(((wq_ref, q_s), (wk_ref, k_s), (wv_ref, v_s))):
            for blk in range(row0 // CONV_ROWS, (row0 + nrows) // CONV_ROWS):
                acc = None
                for j in range(CONV_K):
                    off = blk * CONV_ROWS + CONV_PAD - (CONV_K - 1) // 2 + j
                    term = xpad_s[a, off:off + CONV_ROWS, :] * w_ref[j:j + 1, lanes]
                    acc = term if acc is None else acc + term
                dst[hh, blk * CONV_ROWS:(blk + 1) * CONV_ROWS, :] = finishers[a](_silu(acc))

    ii = lax.broadcasted_iota(jnp.int32, (GDN_CHUNK, GDN_CHUNK), 0)
    jj = lax.broadcasted_iota(jnp.int32, (GDN_CHUNK, GDN_CHUNK), 1)
    causal = (ii >= jj, ii <= jj)
    strict = (ii > jj, ii < jj)

    def chunk_rows(c):
        return slice(c * GDN_CHUNK, (c + 1) * GDN_CHUNK)

    def token_major(row):
        return jnp.broadcast_to(row, (LANES, GDN_CHUNK)).T

    def phase1(it, hh):
        chains = [(it * GDN_P1_CHUNKS + j, d) for j in range(GDN_P1_CHUNKS) for d in range(2)]
        n_ch = len(chains)
        rows = [chunk_rows(c) for c, _ in chains]
        kf = [k_s[hh, r, :] for r in rows]
        kc = [t.astype(BF16) for t in kf]
        grow = [grow_ref[0, hh, G_GC + d, pl.ds(c, 1), :] for c, d in chains]
        gcb = [token_major(t) for t in grow]
        bcb = [token_major(grow_ref[0, hh, G_BETA + d, pl.ds(c, 1), :]) for c, d in chains]
        kb = [kf[i] * bcb[i] for i in range(n_ch)]
        kk = [_dot_nt(kb[i].astype(BF16), kc[i]) for i in range(n_ch)]
        decay = [jnp.exp(jnp.where(causal[d], gcb[i] - grow[i], -jnp.inf)) for i, (_, d) in enumerate(chains)]
        a = [jnp.where(strict[d], kk[i] * decay[i], 0.0) for i, (_, d) in enumerate(chains)]
        egc = [jnp.exp(t) for t in gcb]
        wpre = [kb[i] * egc[i] for i in range(n_ch)]
        upre = [v_s[hh, rows[i], :] * bcb[i] for i in range(n_ch)]
        qf = [q_s[hh, rows[2 * j], :] for j in range(GDN_P1_CHUNKS)]
        qkc = [_dot_nt(qf[j].astype(BF16), kc[2 * j]) for j in range(GDN_P1_CHUNKS)]
        gl = [gcb[i][GDN_CHUNK - 1:GDN_CHUNK, :] if d == 0 else gcb[i][0:1, :] for i, (_, d) in enumerate(chains)]
        kd = [(kf[i] * jnp.exp(gl[i] - gcb[i])).astype(BF16) for i in range(n_ch)]
        for i, (c, d) in enumerate(chains):
            att_s[hh, d, rows[i], :] = (qkc[i // 2] * decay[i]).astype(BF16)
            qg_s[hh, d, rows[i], :] = (qf[i // 2] * egc[i]).astype(BF16)
            gl_s[hh, d, c] = jnp.broadcast_to(jnp.exp(gl[i]), (8, LANES))
        n = [-(t * lvl_ref[0]) for t in a]
        for lvl in range(1, GDN_LEVELS):
            ak = [t * lvl_ref[lvl] for t in a]
            y = [ak[i] + _dot(ak[i].astype(BF16), n[i].astype(BF16)) for i in range(n_ch)]
            n = [n[i] - y[i] - _dot(n[i].astype(BF16), y[i].astype(BF16)) for i in range(n_ch)]
        nb = [t.astype(BF16) for t in n]
        w = [(wpre[i] + _dot(nb[i], wpre[i].astype(BF16))).astype(BF16) for i in range(n_ch)]
        u = [upre[i] + _dot(nb[i], upre[i].astype(BF16)) for i in range(n_ch)]
        mt = [_dot_tn(w[i], kd[i]) for i in range(n_ch)]
        bt = [_dot_tn(u[i].astype(BF16), kd[i]) for i in range(n_ch)]
        for i, (c, d) in enumerate(chains):
            w_s[hh, d, rows[i], :] = w[i]
            u_s[hh, d, rows[i], :] = u[i]
            mt_s[hh, d, c] = (-mt[i]).astype(BF16)
            bt_s[hh, d, c] = bt[i]

    for hh in range(GDN_HPS):
        lanes = slice(hh * LANES, (hh + 1) * LANES)
        for a, raw_ref in enumerate((qraw_ref, kraw_ref, vraw_ref)):
            xpad_s[a, CONV_PAD:CONV_PAD + seq, :] = raw_ref[0, :, lanes].astype(F32)
        for it in range(nchunk // GDN_P1_CHUNKS):
            conv_silu_rows(hh, it * GDN_P1_CHUNKS * GDN_CHUNK, GDN_P1_CHUNKS * GDN_CHUNK)
            phase1(it, hh)

    chains = [(hh, d) for hh in range(GDN_HPS) for d in range(2)]
    state = [jnp.zeros((GDN_DV, GDN_DK), F32) for _ in chains]
    def finish(pending):
        for (hh, d), r, m in pending:
            vnew = (u_s[hh, d, r, :] - m[:GDN_CHUNK]).astype(BF16)
            od_s[hh, d, r, :] = m[GDN_CHUNK:] + _dot(att_s[hh, d, r, :], vnew)

    pending = []
    for i in range(nchunk):
        cs = [i if d == 0 else nchunk - 1 - i for _, d in chains]
        rows = [slice(c * GDN_CHUNK, (c + 1) * GDN_CHUNK) for c in cs]
        stb = [t.astype(BF16) for t in state]
        state = [state[j] * gl_s[hh, d, cs[j], 0:1, :] + _dot(stb[j], mt_s[hh, d, cs[j]]) + bt_s[hh, d, cs[j]]
                 for j, (hh, d) in enumerate(chains)]
        finish(pending)
        pending = [(chains[j], rows[j],
                    _dot_nt(jnp.concatenate([w_s[hh, d, rows[j], :], qg_s[hh, d, rows[j], :]], axis=0), stb[j]))
                   for j, (hh, d) in enumerate(chains)]
    finish(pending)

    for hh in range(GDN_HPS):
        for blk in range(seq // CONV_ROWS):
            r = slice(blk * CONV_ROWS, (blk + 1) * CONV_ROWS)
            o = od_s[hh, 0, r, :] + od_s[hh, 1, r, :]
            o_ref[0, r, hh * LANES:(hh + 1) * LANES] = (
                o * lax.rsqrt(jnp.mean(o * o, axis=-1, keepdims=True) + RMS_EPS) * norm_ref[...]).astype(o_ref.dtype)


def _gdn2(qkvc, conv_w, grow, gdn_norm):
    bsz, seq, _ = qkvc.shape
    groups = GDN_HEADS // GDN_HPS
    width = GDN_HPS * LANES
    nchunk = seq // GDN_CHUNK

    def col(off):
        return pl.BlockSpec((1, seq, width), lambda b, g: (b, 0, off + g))

    def wcol(off):
        return pl.BlockSpec((CONV_K, width), lambda b, g: (0, off + g))

    def per_head(shape, dtype):
        return pltpu.VMEM((GDN_HPS,) + shape, dtype)

    return pl.pallas_call(
        _gdn2_kernel,
        grid=(bsz, groups),
        in_specs=[col(0), col(groups), col(2 * groups), wcol(0), wcol(groups), wcol(2 * groups),
                  pl.BlockSpec((1, GDN_HPS, 4, nchunk, GDN_CHUNK), lambda b, g: (b, g, 0, 0, 0)),
                  _const_spec((1, GDN_DV)), _const_spec((GDN_LEVELS, GDN_CHUNK, GDN_CHUNK))],
        out_specs=pl.BlockSpec((1, seq, width), lambda b, g: (b, 0, g)),
        out_shape=jax.ShapeDtypeStruct((bsz, seq, GDN_HEADS * GDN_DV), BF16),
        scratch_shapes=[pltpu.VMEM((3, seq + 2 * CONV_PAD, LANES), F32),
                        per_head((seq, LANES), F32), per_head((seq, LANES), F32), per_head((seq, LANES), F32),
                        per_head((2, seq, LANES), BF16), per_head((2, seq, LANES), F32),
                        per_head((2, seq, LANES), BF16), per_head((2, seq, LANES), BF16),
                        per_head((2, nchunk, GDN_DV, GDN_DK), BF16), per_head((2, nchunk, GDN_DV, GDN_DK), F32),
                        per_head((2, nchunk, 8, LANES), F32), per_head((2, seq, LANES), F32)],
        compiler_params=_cparams(("parallel", "parallel")),
        name="gated_deltanet",
    )(qkvc, qkvc, qkvc, conv_w, conv_w, conv_w, grow, gdn_norm.reshape(1, -1), _gdn_level_masks())


def _ret_tables():
    heads = np.arange(RET_HEADS, dtype=np.float64)
    log_gamma = np.log1p(-(2.0 ** (-RET_DECAY_BASE - heads)))[:, None]
    idx = np.arange(RET_CHUNK, dtype=np.float64)
    dmat = np.exp(log_gamma[:, :, None] * np.abs(idx[:, None] - idx[None, :])[None])
    xi = np.exp(log_gamma * (idx + 1.0))
    zeta = np.exp(log_gamma * (RET_CHUNK - 1.0 - idx))
    gch = np.exp(log_gamma * RET_CHUNK)

    def cb(t):
        return jnp.asarray(np.broadcast_to(t[:, :, None], t.shape + (LANES,)), F32)

    return jnp.asarray(dmat, F32), cb(xi), cb(zeta), cb(gch)


def _ret_kernel(q_ref, k_ref, v_ref, d_ref, xi_ref, zeta_ref, gch_ref, nw_ref, nb_ref, o_ref, kvf_s, kvb_s):
    seq = q_ref.shape[1]
    nchunk = seq // RET_CHUNK
    h = pl.program_id(1)
    xi, zeta, gch = xi_ref[0], zeta_ref[0], gch_ref[0]
    lane = lax.broadcasted_iota(jnp.int32, (RET_CHUNK, LANES), 1)
    mine = (lane // RET_DK) == (h % 2)

    def rows(c):
        return slice(c * RET_CHUNK, (c + 1) * RET_CHUNK)

    for c in range(nchunk):
        kc = k_ref[0, rows(c), :].astype(F32)
        vc = v_ref[0, rows(c), :]
        kvf_s[c] = _dot_tn((kc * zeta).astype(BF16), vc)
        kvb_s[c] = _dot_tn((kc * xi).astype(BF16), vc)
    state = jnp.zeros((LANES, RET_DV), F32)
    for c in range(nchunk):
        nxt = state * gch + kvf_s[c]
        kvf_s[c] = state
        state = nxt
    state = jnp.zeros((LANES, RET_DV), F32)
    for c in reversed(range(nchunk)):
        nxt = state * gch + kvb_s[c]
        kvb_s[c] = state
        state = nxt
    for c in range(nchunk):
        q = jnp.where(mine, q_ref[0, rows(c), :], jnp.zeros((RET_CHUNK, LANES), BF16))
        qf = q.astype(F32)
        s = _dot_nt(q, k_ref[0, rows(c), :]) * d_ref[0]
        o = _dot(s.astype(BF16), v_ref[0, rows(c), :])
        o = o + _dot((qf * xi).astype(BF16), kvf_s[c].astype(BF16))
        o = o + _dot((qf * zeta).astype(BF16), kvb_s[c].astype(BF16))
        mu = jnp.mean(o, axis=-1, keepdims=True)
        oc = o - mu
        var = jnp.mean(oc * oc, axis=-1, keepdims=True)
        o_ref[0, rows(c), :] = (oc * lax.rsqrt(var + LN_EPS) * nw_ref[...] + nb_ref[...]).astype(o_ref.dtype)


def _retention(qr, kr, vr, norm_w, norm_b):
    bsz, seq, _ = vr.shape
    dmat, xi, zeta, gch = _ret_tables()
    pair = pl.BlockSpec((1, seq, LANES), lambda b, h: (b, 0, h // 2))
    head = pl.BlockSpec((1, seq, RET_DV), lambda b, h: (b, 0, h))

    def per_head(shape):
        return pl.BlockSpec((1,) + shape, lambda b, h: (h, 0, 0))

    vec = pl.BlockSpec((1, RET_DV), lambda b, h: (0, h))
    state = pltpu.VMEM((seq // RET_CHUNK, LANES, RET_DV), F32)
    return pl.pallas_call(
        _ret_kernel,
        grid=(bsz, RET_HEADS),
        in_specs=[pair, pair, head, per_head((RET_CHUNK, RET_CHUNK)), per_head((RET_CHUNK, LANES)),
                  per_head((RET_CHUNK, LANES)), per_head((1, LANES)), vec, vec],
        out_specs=head,
        out_shape=jax.ShapeDtypeStruct((bsz, seq, RET_HEADS * RET_DV), BF16),
        scratch_shapes=[state, state],
        compiler_params=_cparams(("parallel", "parallel")),
        name="retention",
    )(qr, kr, vr, dmat, xi, zeta, gch, norm_w.reshape(1, -1), norm_b.reshape(1, -1))


def _layer1(x, tabs, w_in, conv_w, a_log, dt_bias, gdn_norm, ret_norm_w, ret_norm_b, w_out, ln_w, ln_b):
    bsz, seq, _ = x.shape
    qkvc, gates, qr, kr, vr, gate = _in1(x, tabs, w_in, a_log, dt_bias)
    gsl = gates[:, :, :4 * GDN_HEADS].reshape(bsz, seq, 4, GDN_HEADS)
    grow = gsl.transpose(0, 3, 2, 1).reshape(bsz, GDN_HEADS, 4, seq // GDN_CHUNK, GDN_CHUNK)
    o_c = _gdn2(qkvc, conv_w, grow, gdn_norm)
    o_d = _retention(qr, kr, vr, ret_norm_w, ret_norm_b)
    return _out_proj(o_c, o_d, gate, x, w_out, ln_w, ln_b)


def kernel(x, positions, even_w_in, even_q_norm, even_w_uq, even_kv_norm, even_w_ukv, even_w_out, even_ln_w, even_ln_b, odd_w_in, odd_conv_w, odd_a_log, odd_dt_bias, odd_gdn_norm, odd_ret_norm_w, odd_ret_norm_b, odd_w_out, odd_ln_w, odd_ln_b):
    tabs = _rope_tables(positions)
    x = _layer0(x, tabs, even_w_in, even_q_norm[0], even_w_uq[0], even_kv_norm[0], even_w_ukv[0],
                even_w_out[0], even_ln_w[0], even_ln_b[0])
    x = _layer1(x, tabs, odd_w_in, odd_conv_w[0], odd_a_log[0], odd_dt_bias[0], odd_gdn_norm[0],
                odd_ret_norm_w[0], odd_ret_norm_b[0], odd_w_out[0], odd_ln_w[0], odd_ln_b[0])
    return x
```

```python
import functools

import numpy as np
import jax
import jax.numpy as jnp
from jax import lax
from jax.experimental import pallas as pl
from jax.experimental.pallas import tpu as pltpu

F32 = jnp.float32
BF16 = jnp.bfloat16

D_MODEL = 1024
DEPTH = 2
DEEPNORM_ALPHA = (2 * DEPTH) ** 0.25
ROPE_THETA = 10000.0
RMS_EPS = 1e-6
LN_EPS = 1e-5
L2_EPS = 1e-6

MLA_HEADS = 8
MLA_NOPE = 128
MLA_ROPE = 64
MLA_V = 128
Q_LORA = 768
KV_LORA = 256

DIL_HEADS = 8
DIL_HD = 128
DILATED_GROUPS = ((128, 1), (512, 4), (2048, 16))

GDN_HEADS = 8
GDN_DK = 128
GDN_DV = 128
CONV_K = 5
GDN_CHUNK = 128

RET_HEADS = 8
RET_DK = 64
RET_DV = 128
RET_DECAY_BASE = 5.0
RET_CHUNK = 256

LANES = 128
VMEM_LIMIT = 56 * 1024 * 1024


def _cparams(sem):
    return pltpu.CompilerParams(dimension_semantics=sem, vmem_limit_bytes=VMEM_LIMIT)


def _dot(a, b):
    return jnp.dot(a, b, preferred_element_type=F32)


def _dot_nt(a, b):
    return lax.dot_general(a, b, (((1,), (1,)), ((), ())), preferred_element_type=F32)


def _dot_tn(a, b):
    return lax.dot_general(a, b, (((0,), (0,)), ((), ())), preferred_element_type=F32)


def _const_spec(shape):
    nd = len(shape)
    return pl.BlockSpec(shape, lambda *_: (0,) * nd, pipeline_mode=pl.Buffered(1))


def _rope_pairs64(x, cos, sin_signed):
    lane = lax.broadcasted_iota(jnp.int32, x.shape, 1)
    swapped = jnp.where((lane % 64) < 32, pltpu.roll(x, 96, 1), pltpu.roll(x, 32, 1))
    return x * cos + swapped * sin_signed


def _rope_full128(x, cos, sin_signed):
    return x * cos + pltpu.roll(x, 64, 1) * sin_signed


def _splice_kernel(w_ref, o_ref, *, head_end, tail_start, gap):
    w = w_ref[0]
    parts = [w[:, :head_end], w[:, tail_start:]]
    if gap:
        parts.insert(1, jnp.zeros((w.shape[0], gap), w.dtype))
    o_ref[...] = jnp.concatenate(parts, axis=1).astype(o_ref.dtype)


def _splice_columns(w, head_end, tail_start, gap, tr=128):
    _, rows, cols = w.shape
    total = head_end + gap + cols - tail_start
    return pl.pallas_call(
        functools.partial(_splice_kernel, head_end=head_end, tail_start=tail_start, gap=gap),
        grid=(rows // tr,),
        in_specs=[pl.BlockSpec((1, tr, cols), lambda i: (0, i, 0))],
        out_specs=pl.BlockSpec((tr, total), lambda i: (i, 0)),
        out_shape=jax.ShapeDtypeStruct((rows, total), BF16),
        compiler_params=_cparams(("parallel",)),
        name="regroup_weight",
    )(w)


def _silu(x):
    return x * jax.nn.sigmoid(x)


def _rope_tables_kernel(pos_ref, inv_ref, sgn_ref, c64_ref, s64_ref, c128_ref, s128_ref):
    ang = pos_ref[0] * inv_ref[...]
    cos, sin = jnp.cos(ang), jnp.sin(ang)
    c32, s32 = cos[:, :32], sin[:, :32]
    c64, s64 = cos[:, 32:96], sin[:, 32:96]
    c64_ref[0] = jnp.concatenate([c32] * 4, axis=1)
    s64_ref[0] = jnp.concatenate([s32] * 4, axis=1) * sgn_ref[0:1, :]
    c128_ref[0] = jnp.concatenate([c64] * 2, axis=1)
    s128_ref[0] = jnp.concatenate([s64] * 2, axis=1) * sgn_ref[1:2, :]


def _rope_tables(positions):
    bsz, seq = positions.shape
    ts = 512
    inv32 = ROPE_THETA ** (-jnp.arange(0, 64, 2, dtype=F32) / 64)
    inv64 = ROPE_THETA ** (-jnp.arange(0, 128, 2, dtype=F32) / 128)
    inv = jnp.concatenate([inv32, inv64, jnp.zeros((32,), F32)])[None, :]
    sgn = jnp.asarray(np.stack([np.tile(np.repeat([-1.0, 1.0], 32), 2), np.repeat([-1.0, 1.0], 64)]), F32)
    pos = positions.astype(F32)[..., None]
    tab = jax.ShapeDtypeStruct((bsz, seq, LANES), F32)
    tspec = pl.BlockSpec((1, ts, LANES), lambda b, s: (b, s, 0))
    return pl.pallas_call(
        _rope_tables_kernel,
        grid=(bsz, seq // ts),
        in_specs=[pl.BlockSpec((1, ts, 1), lambda b, s: (b, s, 0)), _const_spec((1, LANES)), _const_spec((2, LANES))],
        out_specs=[tspec] * 4,
        out_shape=[tab] * 4,
        compiler_params=_cparams(("parallel", "parallel")),
        name="rope_tables",
    )(pos, inv, sgn)


def _rms(x, w):
    return x * lax.rsqrt(jnp.mean(x * x, axis=-1, keepdims=True) + RMS_EPS) * w


IN0_COLS = np.cumsum((0, Q_LORA, KV_LORA, 2 * MLA_ROPE, 3 * DIL_HEADS * DIL_HD, MLA_HEADS * MLA_V + DIL_HEADS * DIL_HD))


def _in0_kernel(x_ref, w_ref, wuq_ref, wukv_ref, qnorm_ref, kvnorm_ref,
                c64_ref, s64_ref, c128_ref, s128_ref,
                qn_ref, qr_ref, kv_ref, kr_ref, qb_ref, kb_ref, vb_ref, g_ref):
    x = x_ref[0].astype(BF16)
    c64, s64, c128, s128 = c64_ref[0], s64_ref[0], c128_ref[0], s128_ref[0]

    def proj(i):
        return _dot(x, w_ref[:, IN0_COLS[i]:IN0_COLS[i + 1]])

    cq = _rms(proj(0), qnorm_ref[...])
    qa = _dot(cq.astype(BF16), wuq_ref[...]) * ((MLA_NOPE + MLA_ROPE) ** -0.5 * LOG2E)
    nope = MLA_HEADS * MLA_NOPE
    qn_ref[0] = qa[:, :nope].astype(BF16)
    for j in range(MLA_HEADS * MLA_ROPE // LANES):
        blk = qa[:, nope + j * LANES:nope + (j + 1) * LANES]
        qr_ref[0, :, j * LANES:(j + 1) * LANES] = _rope_pairs64(blk, c64, s64).astype(BF16)
    ckv = _rms(proj(1), kvnorm_ref[...])
    kv_ref[0] = _dot(ckv.astype(BF16), wukv_ref[...]).astype(BF16)
    kr_ref[0] = _rope_pairs64(proj(2), c64, s64).astype(BF16)
    hb = proj(3)
    width = DIL_HEADS * DIL_HD
    for h in range(DIL_HEADS):
        sl = slice(h * DIL_HD, (h + 1) * DIL_HD)
        qb_ref[0, :, sl] = _rope_full128(hb[:, sl], c128, s128) * (DIL_HD ** -0.5 * LOG2E)
        kb_ref[0, :, sl] = _rope_full128(hb[:, width + h * DIL_HD:width + (h + 1) * DIL_HD], c128, s128)
    vb_ref[0] = hb[:, 2 * width:]
    g_ref[0] = _silu(proj(4)).astype(BF16)


def _in0(x, tabs, w_in, q_norm, w_uq, kv_norm, w_ukv, tm=512):
    bsz, seq, _ = x.shape
    c64, s64, c128, s128 = tabs
    kr0 = Q_LORA + KV_LORA
    w_all = _splice_columns(w_in, kr0 + MLA_ROPE, kr0, 0)
    wuq3 = w_uq.reshape(Q_LORA, MLA_HEADS, MLA_NOPE + MLA_ROPE)
    wuq = jnp.concatenate([wuq3[:, :, :MLA_NOPE].reshape(Q_LORA, -1), wuq3[:, :, MLA_NOPE:].reshape(Q_LORA, -1)],
                          axis=1).astype(BF16)
    wukv = w_ukv.astype(BF16)
    weights = (w_all, wuq, wukv, q_norm.reshape(1, -1), kv_norm.reshape(1, -1))

    def rows(width):
        return pl.BlockSpec((1, tm, width), lambda b, s: (b, s, 0))

    out_widths = (1024, 512, 2048, LANES, 1024, 1024, 1024, 2048)
    out_dtypes = (BF16, BF16, BF16, BF16, F32, F32, F32, BF16)
    return pl.pallas_call(
        _in0_kernel,
        grid=(bsz, seq // tm),
        in_specs=[rows(D_MODEL)] + [_const_spec(w.shape) for w in weights] + [rows(LANES)] * 4,
        out_specs=[rows(w) for w in out_widths],
        out_shape=[jax.ShapeDtypeStruct((bsz, seq, w), dt) for w, dt in zip(out_widths, out_dtypes)],
        compiler_params=_cparams(("parallel", "parallel")),
        name="in_proj0",
    )(x, *weights, c64, s64, c128, s128)


MLA_SUB = 256
MLA_ONES_ROWS = 16
MLA_AHEAD = 3
LOG2E = 1.4426950408889634


def _mla_kernel(qn_ref, qr_ref, kn_ref, kr_ref, v_ref, o_ref, k_s, vt_s):
    h = pl.program_id(1)

    @pl.when(pl.program_id(2) == 0)
    def _():
        k_s[:, :MLA_NOPE] = kn_ref[0]
        k_s[:, MLA_NOPE:] = kr_ref[0]
        vt_s[:MLA_V, :] = v_ref[0].astype(F32).T.astype(BF16)
        vt_s[MLA_V:, :] = jnp.ones((MLA_ONES_ROWS, vt_s.shape[1]), BF16)

    lane = lax.broadcasted_iota(jnp.int32, (MLA_SUB, LANES), 1)
    mine = (lane // MLA_ROPE) == (h % 2)
    nsub = qn_ref.shape[1] // MLA_SUB

    def scores(i):
        rows = slice(i * MLA_SUB, (i + 1) * MLA_SUB)
        qr = jnp.where(mine, qr_ref[0, rows, :], jnp.zeros((MLA_SUB, LANES), BF16))
        return _dot_nt(k_s[...], jnp.concatenate([qn_ref[0, rows, :], qr], axis=1)).astype(BF16)

    ahead = [scores(i) for i in range(min(MLA_AHEAD, nsub))]
    for i in range(nsub):
        s = ahead.pop(0)
        if i + MLA_AHEAD < nsub:
            ahead.append(scores(i + MLA_AHEAD))
        m = jnp.max(s, axis=0, keepdims=True)
        p = jnp.exp2(s - m)
        ot = _dot(vt_s[...], p)
        o = ot[:MLA_V] / ot[MLA_V:MLA_V + 1]
        o_ref[0, i * MLA_SUB:(i + 1) * MLA_SUB, :] = o.T.astype(o_ref.dtype)


def _mla(qn, qr, kv, kr, tq=2048):
    bsz, seq, _ = qn.shape
    return pl.pallas_call(
        _mla_kernel,
        grid=(bsz, MLA_HEADS, seq // tq),
        in_specs=[
            pl.BlockSpec((1, tq, LANES), lambda b, h, i: (b, i, h)),
            pl.BlockSpec((1, tq, LANES), lambda b, h, i: (b, i, h // 2)),
            pl.BlockSpec((1, seq, LANES), lambda b, h, i: (b, 0, 2 * h)),
            pl.BlockSpec((1, seq, LANES), lambda b, h, i: (b, 0, 0)),
            pl.BlockSpec((1, seq, LANES), lambda b, h, i: (b, 0, 2 * h + 1)),
        ],
        out_specs=pl.BlockSpec((1, tq, LANES), lambda b, h, i: (b, i, h)),
        out_shape=jax.ShapeDtypeStruct((bsz, seq, MLA_HEADS * MLA_V), BF16),
        scratch_shapes=[pltpu.VMEM((seq, MLA_NOPE + LANES), BF16), pltpu.VMEM((MLA_V + MLA_ONES_ROWS, seq), BF16)],
        compiler_params=_cparams(("parallel", "parallel", "arbitrary")),
        name="mla_attention",
    )(qn, qr, kv, kr, kv)


DIL_TQ = 128
DIL_RADIUS = 64
DIL_TILES_PER_STEP = 16


def _dil_bias_table():
    i = np.arange(DIL_TQ)[:, None]
    j = np.arange(2 * DIL_TQ)[None, :]
    tabs = [np.where(np.abs(j - off - i) <= DIL_RADIUS, 0.0, -np.inf) for off in (0, DIL_RADIUS, 2 * DIL_RADIUS)]
    return jnp.asarray(np.stack(tabs), F32)


def _dil_kernel(q_ref, k_ref, v_ref, bias_ref, o_ref, qd_s, kd_s, vd_s, m_s, l_s, acc_s):
    seq = q_ref.shape[1]
    vd_s[:, :, DIL_HD:] = jnp.ones((len(DILATED_GROUPS), seq, DIL_HD), BF16)

    def rows_of(start, size, dil):
        return pl.ds(start, size) if dil == 1 else pl.ds(start, size, stride=dil)

    for g, (_, dil) in enumerate(DILATED_GROUPS):
        length = seq // dil
        for src, dst in ((q_ref, qd_s), (k_ref, kd_s), (v_ref, vd_s)):
            for r in range(dil):
                dst[g, r * length:(r + 1) * length, :DIL_HD] = src[0, rows_of(r, length, dil), :].astype(BF16)

    def tiles_step(g, dil, which, order):
        length = seq // dil
        tiles = length // DIL_TQ
        kw = min(length, DIL_TQ + 2 * DIL_RADIUS)
        n = len(which)
        ws = [jnp.clip(t * DIL_TQ - DIL_RADIUS, 0, length - kw) for _, t in which]
        qrows = [pl.ds(pl.multiple_of(r * length + t * DIL_TQ, DIL_TQ), DIL_TQ) for r, t in which]
        krows = [pl.ds(pl.multiple_of(r * length + ws[i], DIL_RADIUS), kw) for i, (r, _) in enumerate(which)]
        if dil == 1:
            out_rows = qrows
        else:
            out_rows = [rows_of(r + t * (DIL_TQ * dil), DIL_TQ, dil) for r, t in which]
        if tiles == 1:
            bias = [bias_ref[0, :, :kw] for _ in which]
        else:
            bias = [bias_ref[jnp.where(t == 0, 0, jnp.where(t == tiles - 1, 2, 1))] for _, t in which]
        s = [(_dot_nt(qd_s[g, qrows[i], :], kd_s[g, krows[i], :]) + bias[i]).astype(BF16) for i in range(n)]
        mk = [jnp.max(t, axis=-1, keepdims=True) for t in s]
        m = [jnp.broadcast_to(t.astype(F32), (DIL_TQ, DIL_HD)) for t in mk]
        p = [jnp.exp2(s[i] - mk[i]) for i in range(n)]
        ol = [_dot(p[i], vd_s[g, krows[i], :]) for i in range(n)]
        o = [t[:, :DIL_HD] for t in ol]
        l = [t[:, DIL_HD:] for t in ol]
        for i in range(n):
            if order == 0:
                m_s[out_rows[i], :] = m[i]
                l_s[out_rows[i], :] = l[i]
                acc_s[out_rows[i], :] = o[i]
                continue
            m_old = m_s[out_rows[i], :]
            m_new = jnp.maximum(m_old, m[i])
            a_old = jnp.exp2(m_old - m_new)
            a_new = jnp.exp2(m[i] - m_new)
            l_new = a_old * l_s[out_rows[i], :] + a_new * l[i]
            acc_new = a_old * acc_s[out_rows[i], :] + a_new * o[i]
            if order == len(DILATED_GROUPS) - 1:
                o_ref[0, out_rows[i], :] = (acc_new / l_new).astype(o_ref.dtype)
            else:
                m_s[out_rows[i], :] = m_new
                l_s[out_rows[i], :] = l_new
                acc_s[out_rows[i], :] = acc_new

    for order, g in enumerate(reversed(range(len(DILATED_GROUPS)))):
        window, dil = DILATED_GROUPS[g]
        assert window // (2 * dil) == DIL_RADIUS
        assert order < len(DILATED_GROUPS) - 1 or dil == 1
        tiles = seq // dil // DIL_TQ

        def body(it, carry, g=g, dil=dil, tiles=tiles, order=order):
            idx = [it * DIL_TILES_PER_STEP + j for j in range(DIL_TILES_PER_STEP)]
            tiles_step(g, dil, [(i // tiles, i % tiles) for i in idx], order)
            return carry

        lax.fori_loop(0, dil * tiles // DIL_TILES_PER_STEP, body, 0)


def _dilated(qb, kb, vb):
    bsz, seq, _ = qb.shape
    spec = pl.BlockSpec((1, seq, DIL_HD), lambda b, h: (b, 0, h))
    bias = _dil_bias_table()
    grouped = pltpu.VMEM((len(DILATED_GROUPS), seq, DIL_HD), BF16)
    stat = pltpu.VMEM((seq, DIL_HD), F32)
    return pl.pallas_call(
        _dil_kernel,
        grid=(bsz, DIL_HEADS),
        in_specs=[spec] * 3 + [_const_spec(bias.shape)],
        out_specs=spec,
        out_shape=jax.ShapeDtypeStruct((bsz, seq, DIL_HEADS * DIL_HD), BF16),
        scratch_shapes=[grouped, grouped, pltpu.VMEM((len(DILATED_GROUPS), seq, 2 * DIL_HD), BF16), stat, stat, stat],
        compiler_params=_cparams(("parallel", "parallel")),
        name="dilated_attention",
    )(qb, kb, vb, bias)


OUT_SUB = 256


def _out_kernel(o1_ref, o2_ref, g_ref, x_ref, w_ref, lnw_ref, lnb_ref, out_ref):
    half = o1_ref.shape[-1]
    for i in range(o1_ref.shape[1] // OUT_SUB):
        rows = slice(i * OUT_SUB, (i + 1) * OUT_SUB)
        y1 = o1_ref[0, rows, :] * g_ref[0, rows, :half]
        y2 = o2_ref[0, rows, :] * g_ref[0, rows, half:]
        r = DEEPNORM_ALPHA * x_ref[0, rows, :] + _dot(y1, w_ref[:half, :]) + _dot(y2, w_ref[half:, :])
        mu = jnp.mean(r, axis=-1, keepdims=True)
        rc = r - mu
        var = jnp.mean(rc * rc, axis=-1, keepdims=True)
        out_ref[0, rows, :] = rc * lax.rsqrt(var + LN_EPS) * lnw_ref[...] + lnb_ref[...]


def _out_proj(o1, o2, gate, x, w_out, ln_w, ln_b, tm=512):
    bsz, seq, _ = x.shape
    half = o1.shape[-1]

    def rows(width):
        return pl.BlockSpec((1, tm, width), lambda b, s: (b, s, 0))

    w = w_out.astype(BF16)
    return pl.pallas_call(
        _out_kernel,
        grid=(bsz, seq // tm),
        in_specs=[rows(half), rows(half), rows(2 * half), rows(D_MODEL), _const_spec(w.shape),
                  _const_spec((1, D_MODEL)), _const_spec((1, D_MODEL))],
        out_specs=rows(D_MODEL),
        out_shape=jax.ShapeDtypeStruct((bsz, seq, D_MODEL), F32),
        compiler_params=_cparams(("parallel", "parallel")),
        name="out_proj",
    )(o1, o2, gate, x, w, ln_w.reshape(1, -1), ln_b.reshape(1, -1))


def _layer0(x, tabs, w_in, q_norm, w_uq, kv_norm, w_ukv, w_out, ln_w, ln_b):
    qn, qr, kv, kr, qb, kb, vb, gate = _in0(x, tabs, w_in, q_norm, w_uq, kv_norm, w_ukv)
    o_a = _mla(qn, qr, kv, kr)
    o_b = _dilated(qb, kb, vb)
    return _out_proj(o_a, o_b, gate, x, w_out, ln_w, ln_b)


GATE_BETA = 0
GATE_GC = 2 * GDN_HEADS


def _softplus(x):
    return jnp.maximum(x, 0.0) + jnp.log1p(jnp.exp(-jnp.abs(x)))


IN1_COLS = np.cumsum((0, 2 * GDN_HEADS * GDN_DK + GDN_HEADS * GDN_DV, LANES, 2 * RET_HEADS * RET_DK,
                      RET_HEADS * RET_DV, GDN_HEADS * GDN_DV + RET_HEADS * RET_DV))


def _in1_kernel(x_ref, w_ref, gp_ref, c64_ref, s64_ref,
                qkvc_ref, gate_ref, qr_ref, kr_ref, vr_ref, g_ref):
    x = x_ref[0].astype(BF16)
    tm = x.shape[0]
    c64, s64 = c64_ref[0], s64_ref[0]

    def proj(i):
        return _dot(x, w_ref[:, IN1_COLS[i]:IN1_COLS[i + 1]])

    qkvc_ref[0] = proj(0).astype(qkvc_ref.dtype)
    hg = proj(1)
    beta = jax.nn.sigmoid(hg)
    g = -jnp.exp(gp_ref[0:1, :]) * _softplus(hg + gp_ref[1:2, :])
    pos = lax.broadcasted_iota(jnp.int32, hg.shape, 0) % GDN_CHUNK
    pre, suf = g, g
    step = 1
    while step < GDN_CHUNK:
        pre = pre + jnp.where(pos >= step, pltpu.roll(pre, step, 0), 0.0)
        suf = suf + jnp.where(pos < GDN_CHUNK - step, pltpu.roll(suf, tm - step, 0), 0.0)
        step *= 2
    lane = lax.broadcasted_iota(jnp.int32, hg.shape, 1)
    gc = jnp.where(lane < GATE_GC + GDN_HEADS, pre, suf)
    gate_ref[0] = jnp.where(lane < GATE_GC, beta, gc)
    hqk = proj(2)
    width = RET_HEADS * RET_DK
    for j in range(width // LANES):
        qr_ref[0, :, j * LANES:(j + 1) * LANES] = _rope_pairs64(hqk[:, j * LANES:(j + 1) * LANES], c64, s64).astype(BF16)
        kblk = hqk[:, width + j * LANES:width + (j + 1) * LANES]
        kr_ref[0, :, j * LANES:(j + 1) * LANES] = (_rope_pairs64(kblk, c64, s64) * (RET_DK ** -0.5)).astype(BF16)
    vr_ref[0] = proj(3).astype(BF16)
    g_ref[0] = _silu(proj(4)).astype(BF16)


def _in1(x, tabs, w_in, a_log, dt_bias, tm=512):
    bsz, seq, _ = x.shape
    c64, s64, _, _ = tabs
    conv_ch = 2 * GDN_HEADS * GDN_DK + GDN_HEADS * GDN_DV
    gate_end = conv_ch + 4 * GDN_HEADS
    w_all = _splice_columns(w_in, gate_end, gate_end, LANES - 4 * GDN_HEADS)
    pad = (GATE_GC, LANES - GATE_GC - 2 * GDN_HEADS)
    gp = jnp.stack([jnp.pad(a_log.reshape(-1), pad), jnp.pad(dt_bias.reshape(-1), pad)]).astype(F32)
    weights = (w_all, gp)

    def rows(width):
        return pl.BlockSpec((1, tm, width), lambda b, s: (b, s, 0))

    out_widths = (conv_ch, LANES, 512, 512, 1024, 2048)
    out_dtypes = (BF16, F32, BF16, BF16, BF16, BF16)
    return pl.pallas_call(
        _in1_kernel,
        grid=(bsz, seq // tm),
        in_specs=[rows(D_MODEL)] + [_const_spec(w.shape) for w in weights] + [rows(LANES)] * 2,
        out_specs=[rows(w) for w in out_widths],
        out_shape=[jax.ShapeDtypeStruct((bsz, seq, w), dt) for w, dt in zip(out_widths, out_dtypes)],
        compiler_params=_cparams(("parallel", "parallel")),
        name="in_proj1",
    )(x, *weights, c64, s64)


CONV_PAD = 8
GDN_P1_CHUNKS = 8
GDN_LEVELS = 7


def _gdn_level_masks():
    i = np.arange(GDN_CHUNK)[:, None]
    j = np.arange(GDN_CHUNK)[None, :]
    masks = [((i >> (k + 1)) == (j >> (k + 1))) & ((i >> k) != (j >> k)) for k in range(GDN_LEVELS)]
    return jnp.asarray(np.stack(masks), F32)


GDN_HPS = 1
CONV_ROWS = 256
G_BETA, G_GC = 0, 2


def _gdn2_kernel(qraw_ref, kraw_ref, vraw_ref, wq_ref, wk_ref, wv_ref, grow_ref, norm_ref, lvl_ref, o_ref,
                 xpad_s, q_s, k_s, v_s, w_s, u_s, att_s, qg_s, mt_s, bt_s, gl_s, od_s):
    seq = q_s.shape[1]
    nchunk = seq // GDN_CHUNK

    zeros = jnp.zeros((3, CONV_PAD, LANES), F32)
    xpad_s[:, 0:CONV_PAD, :] = zeros
    xpad_s[:, CONV_PAD + seq:, :] = zeros

    def l2norm(t):
        return t * lax.rsqrt(jnp.sum(t * t, axis=-1, keepdims=True) + L2_EPS)

    finishers = (lambda t: l2norm(t) * (GDN_DK ** -0.5), l2norm, lambda t: t)

    def conv_silu_rows(hh, row0, nrows):
        lanes = slice(hh * LANES, (hh + 1) * LANES)
        for a, (w_ref, dst) in enumerate(((wq_ref, q_s), (wk_ref, k_s), (wv_ref, v_s))):
            for blk in range(row0 // CONV_ROWS, (row0 + nrows) // CONV_ROWS):
                acc = None
                for j in range(CONV_K):
                    off = blk * CONV_ROWS + CONV_PAD - (CONV_K - 1) // 2 + j
                    term = xpad_s[a, off:off + CONV_ROWS, :] * w_ref[j:j + 1, lanes]
                    acc = term if acc is None else acc + term
                dst[hh, blk * CONV_ROWS:(blk + 1) * CONV_ROWS, :] = finishers[a](_silu(acc))

    ii = lax.broadcasted_iota(jnp.int32, (GDN_CHUNK, GDN_CHUNK), 0)
    jj = lax.broadcasted_iota(jnp.int32, (GDN_CHUNK, GDN_CHUNK), 1)
    causal = (ii >= jj, ii <= jj)
    strict = (ii > jj, ii < jj)

    def chunk_rows(c):
        return slice(c * GDN_CHUNK, (c + 1) * GDN_CHUNK)

    def token_major(row):
        return jnp.broadcast_to(row, (LANES, GDN_CHUNK)).T

    def phase1(it, hh):
        chains = [(it * GDN_P1_CHUNKS + j, d) for j in range(GDN_P1_CHUNKS) for d in range(2)]
        n_ch = len(chains)
        rows = [chunk_rows(c) for c, _ in chains]
        kf = [k_s[hh, r, :] for r in rows]
        kc = [t.astype(BF16) for t in kf]
        grow = [grow_ref[0, hh, G_GC + d, pl.ds(c, 1), :] for c, d in chains]
        gcb = [token_major(t) for t in grow]
        bcb = [token_major(grow_ref[0, hh, G_BETA + d, pl.ds(c, 1), :]) for c, d in chains]
        kb = [kf[i] * bcb[i] for i in range(n_ch)]
        kk = [_dot_nt(kb[i].astype(BF16), kc[i]) for i in range(n_ch)]
        decay = [jnp.exp(jnp.where(causal[d], gcb[i] - grow[i], -jnp.inf)) for i, (_, d) in enumerate(chains)]
        a = [jnp.where(strict[d], kk[i] * decay[i], 0.0) for i, (_, d) in enumerate(chains)]
        egc = [jnp.exp(t) for t in gcb]
        wpre = [kb[i] * egc[i] for i in range(n_ch)]
        upre = [v_s[hh, rows[i], :] * bcb[i] for i in range(n_ch)]
        qf = [q_s[hh, rows[2 * j], :] for j in range(GDN_P1_CHUNKS)]
        qkc = [_dot_nt(qf[j].astype(BF16), kc[2 * j]) for j in range(GDN_P1_CHUNKS)]
        gl = [gcb[i][GDN_CHUNK - 1:GDN_CHUNK, :] if d == 0 else gcb[i][0:1, :] for i, (_, d) in enumerate(chains)]
        kd = [(kf[i] * jnp.exp(gl[i] - gcb[i])).astype(BF16) for i in range(n_ch)]
        for i, (c, d) in enumerate(chains):
            att_s[hh, d, rows[i], :] = (qkc[i // 2] * decay[i]).astype(BF16)
            qg_s[hh, d, rows[i], :] = (qf[i // 2] * egc[i]).astype(BF16)
            gl_s[hh, d, c] = jnp.broadcast_to(jnp.exp(gl[i]), (8, LANES))
        n = [-(t * lvl_ref[0]) for t in a]
        for lvl in range(1, GDN_LEVELS):
            ak = [t * lvl_ref[lvl] for t in a]
            y = [ak[i] + _dot(ak[i].astype(BF16), n[i].astype(BF16)) for i in range(n_ch)]
            n = [n[i] - y[i] - _dot(n[i].astype(BF16), y[i].astype(BF16)) for i in range(n_ch)]
        nb = [t.astype(BF16) for t in n]
        w = [(wpre[i] + _dot(nb[i], wpre[i].astype(BF16))).astype(BF16) for i in range(n_ch)]
        u = [upre[i] + _dot(nb[i], upre[i].astype(BF16)) for i in range(n_ch)]
        mt = [_dot_tn(w[i], kd[i]) for i in range(n_ch)]
        bt = [_dot_tn(u[i].astype(BF16), kd[i]) for i in range(n_ch)]
        for i, (c, d) in enumerate(chains):
            w_s[hh, d, rows[i], :] = w[i]
            u_s[hh, d, rows[i], :] = u[i]
            mt_s[hh, d, c] = (-mt[i]).astype(BF16)
            bt_s[hh, d, c] = bt[i]

    for hh in range(GDN_HPS):
        lanes = slice(hh * LANES, (hh + 1) * LANES)
        for a, raw_ref in enumerate((qraw_ref, kraw_ref, vraw_ref)):
            xpad_s[a, CONV_PAD:CONV_PAD + seq, :] = raw_ref[0, :, lanes].astype(F32)
        for it in range(nchunk // GDN_P1_CHUNKS):
            conv_silu_rows(hh, it * GDN_P1_CHUNKS * GDN_CHUNK, GDN_P1_CHUNKS * GDN_CHUNK)
            phase1(it, hh)

    chains = [(hh, d) for hh in range(GDN_HPS) for d in range(2)]
    state = [jnp.zeros((GDN_DV, GDN_DK), F32) for _ in chains]
    def finish(pending):
        for (hh, d), r, m in pending:
            vnew = (u_s[hh, d, r, :] - m[:GDN_CHUNK]).astype(BF16)
            od_s[hh, d, r, :] = m[GDN_CHUNK:] + _dot(att_s[hh, d, r, :], vnew)

    pending = []
    for i in range(nchunk):
        cs = [i if d == 0 else nchunk - 1 - i for _, d in chains]
        rows = [slice(c * GDN_CHUNK, (c + 1) * GDN_CHUNK) for c in cs]
        stb = [t.astype(BF16) for t in state]
        state = [state[j] * gl_s[hh, d, cs[j], 0:1, :] + _dot(stb[j], mt_s[hh, d, cs[j]]) + bt_s[hh, d, cs[j]]
                 for j, (hh, d) in enumerate(chains)]
        finish(pending)
        pending = [(chains[j], rows[j],
                    _dot_nt(jnp.concatenate([w_s[hh, d, rows[j], :], qg_s[hh, d, rows[j], :]], axis=0), stb[j]))
                   for j, (hh, d) in enumerate(chains)]
    finish(pending)

    for hh in range(GDN_HPS):
        for blk in range(seq // CONV_ROWS):
            r = slice(blk * CONV_ROWS, (blk + 1) * CONV_ROWS)
            o = od_s[hh, 0, r, :] + od_s[hh, 1, r, :]
            o_ref[0, r, hh * LANES:(hh + 1) * LANES] = (
                o * lax.rsqrt(jnp.mean(o * o, axis=-1, keepdims=True) + RMS_EPS) * norm_ref[...]).astype(o_ref.dtype)


def _gdn2(qkvc, conv_w, grow, gdn_norm):
    bsz, seq, _ = qkvc.shape
    groups = GDN_HEADS // GDN_HPS
    width = GDN_HPS * LANES
    nchunk = seq // GDN_CHUNK

    def col(off):
        return pl.BlockSpec((1, seq, width), lambda b, g: (b, 0, off + g))

    def wcol(off):
        return pl.BlockSpec((CONV_K, width), lambda b, g: (0, off + g))

    def per_head(shape, dtype):
        return pltpu.VMEM((GDN_HPS,) + shape, dtype)

    return pl.pallas_call(
        _gdn2_kernel,
        grid=(bsz, groups),
        in_specs=[col(0), col(groups), col(2 * groups), wcol(0), wcol(groups), wcol(2 * groups),
                  pl.BlockSpec((1, GDN_HPS, 4, nchunk, GDN_CHUNK), lambda b, g: (b, g, 0, 0, 0)),
                  _const_spec((1, GDN_DV)), _const_spec((GDN_LEVELS, GDN_CHUNK, GDN_CHUNK))],
        out_specs=pl.BlockSpec((1, seq, width), lambda b, g: (b, 0, g)),
        out_shape=jax.ShapeDtypeStruct((bsz, seq, GDN_HEADS * GDN_DV), BF16),
        scratch_shapes=[pltpu.VMEM((3, seq + 2 * CONV_PAD, LANES), F32),
                        per_head((seq, LANES), F32), per_head((seq, LANES), F32), per_head((seq, LANES), F32),
                        per_head((2, seq, LANES), BF16), per_head((2, seq, LANES), F32),
                        per_head((2, seq, LANES), BF16), per_head((2, seq, LANES), BF16),
                        per_head((2, nchunk, GDN_DV, GDN_DK), BF16), per_head((2, nchunk, GDN_DV, GDN_DK), F32),
                        per_head((2, nchunk, 8, LANES), F32), per_head((2, seq, LANES), F32)],
        compiler_params=_cparams(("parallel", "parallel")),
        name="gated_deltanet",
    )(qkvc, qkvc, qkvc, conv_w, conv_w, conv_w, grow, gdn_norm.reshape(1, -1), _gdn_level_masks())


def _ret_tables():
    heads = np.arange(RET_HEADS, dtype=np.float64)
    log_gamma = np.log1p(-(2.0 ** (-RET_DECAY_BASE - heads)))[:, None]
    idx = np.arange(RET_CHUNK, dtype=np.float64)
    dmat = np.exp(log_gamma[:, :, None] * np.abs(idx[:, None] - idx[None, :])[None])
    xi = np.exp(log_gamma * (idx + 1.0))
    zeta = np.exp(log_gamma * (RET_CHUNK - 1.0 - idx))
    gch = np.exp(log_gamma * RET_CHUNK)

    def cb(t):
        return jnp.asarray(np.broadcast_to(t[:, :, None], t.shape + (LANES,)), F32)

    return jnp.asarray(dmat, F32), cb(xi), cb(zeta), cb(gch)


def _ret_kernel(q_ref, k_ref, v_ref, d_ref, xi_ref, zeta_ref, gch_ref, nw_ref, nb_ref, o_ref, kvf_s, kvb_s):
    seq = q_ref.shape[1]
    nchunk = seq // RET_CHUNK
    h = pl.program_id(1)
    xi, zeta, gch = xi_ref[0], zeta_ref[0], gch_ref[0]
    lane = lax.broadcasted_iota(jnp.int32, (RET_CHUNK, LANES), 1)
    mine = (lane // RET_DK) == (h % 2)

    def rows(c):
        return slice(c * RET_CHUNK, (c + 1) * RET_CHUNK)

    for c in range(nchunk):
        kc = k_ref[0, rows(c), :].astype(F32)
        vc = v_ref[0, rows(c), :]
        kvf_s[c] = _dot_tn((kc * zeta).astype(BF16), vc)
        kvb_s[c] = _dot_tn((kc * xi).astype(BF16), vc)
    state = jnp.zeros((LANES, RET_DV), F32)
    for c in range(nchunk):
        nxt = state * gch + kvf_s[c]
        kvf_s[c] = state
        state = nxt
    state = jnp.zeros((LANES, RET_DV), F32)
    for c in reversed(range(nchunk)):
        nxt = state * gch + kvb_s[c]
        kvb_s[c] = state
        state = nxt
    for c in range(nchunk):
        q = jnp.where(mine, q_ref[0, rows(c), :], jnp.zeros((RET_CHUNK, LANES), BF16))
        qf = q.astype(F32)
        s = _dot_nt(q, k_ref[0, rows(c), :]) * d_ref[0]
        o = _dot(s.astype(BF16), v_ref[0, rows(c), :])
        o = o + _dot((qf * xi).astype(BF16), kvf_s[c].astype(BF16))
        o = o + _dot((qf * zeta).astype(BF16), kvb_s[c].astype(BF16))
        mu = jnp.mean(o, axis=-1, keepdims=True)
        oc = o - mu
        var = jnp.mean(oc * oc, axis=-1, keepdims=True)
        o_ref[0, rows(c), :] = (oc * lax.rsqrt(var + LN_EPS) * nw_ref[...] + nb_ref[...]).astype(o_ref.dtype)


def _retention(qr, kr, vr, norm_w, norm_b):
    bsz, seq, _ = vr.shape
    dmat, xi, zeta, gch = _ret_tables()
    pair = pl.BlockSpec((1, seq, LANES), lambda b, h: (b, 0, h // 2))
    head = pl.BlockSpec((1, seq, RET_DV), lambda b, h: (b, 0, h))

    def per_head(shape):
        return pl.BlockSpec((1,) + shape, lambda b, h: (h, 0, 0))

    vec = pl.BlockSpec((1, RET_DV), lambda b, h: (0, h))
    state = pltpu.VMEM((seq // RET_CHUNK, LANES, RET_DV), F32)
    return pl.pallas_call(
        _ret_kernel,
        grid=(bsz, RET_HEADS),
        in_specs=[pair, pair, head, per_head((RET_CHUNK, RET_CHUNK)), per_head((RET_CHUNK, LANES)),
                  per_head((RET_CHUNK, LANES)), per_head((1, LANES)), vec, vec],
        out_specs=head,
        out_shape=jax.ShapeDtypeStruct((bsz, seq, RET_HEADS * RET_DV), BF16),
        scratch_shapes=[state, state],
        compiler_params=_cparams(("parallel", "parallel")),
        name="retention",
    )(qr, kr, vr, dmat, xi, zeta, gch, norm_w.reshape(1, -1), norm_b.reshape(1, -1))


def _layer1(x, tabs, w_in, conv_w, a_log, dt_bias, gdn_norm, ret_norm_w, ret_norm_b, w_out, ln_w, ln_b):
    bsz, seq, _ = x.shape
    qkvc, gates, qr, kr, vr, gate = _in1(x, tabs, w_in, a_log, dt_bias)
    gsl = gates[:, :, :4 * GDN_HEADS].reshape(bsz, seq, 4, GDN_HEADS)
    grow = gsl.transpose(0, 3, 2, 1).reshape(bsz, GDN_HEADS, 4, seq // GDN_CHUNK, GDN_CHUNK)
    o_c = _gdn2(qkvc, conv_w, grow, gdn_norm)
    o_d = _retention(qr, kr, vr, ret_norm_w, ret_norm_b)
    return _out_proj(o_c, o_d, gate, x, w_out, ln_w, ln_b)


def kernel(x, positions, even_w_in, even_q_norm, even_w_uq, even_kv_norm, even_w_ukv, even_w_out, even_ln_w, even_ln_b, odd_w_in, odd_conv_w, odd_a_log, odd_dt_bias, odd_gdn_norm, odd_ret_norm_w, odd_ret_norm_b, odd_w_out, odd_ln_w, odd_ln_b):
    tabs = _rope_tables(positions)
    x = _layer0(x, tabs, even_w_in, even_q_norm[0], even_w_uq[0], even_kv_norm[0], even_w_ukv[0],
                even_w_out[0], even_ln_w[0], even_ln_b[0])
    x = _layer1(x, tabs, odd_w_in, odd_conv_w[0], odd_a_log[0], odd_dt_bias[0], odd_gdn_norm[0],
                odd_ret_norm_w[0], odd_ret_norm_b[0], odd_w_out[0], odd_ln_w[0], odd_ln_b[0])
    return x
```

```python
import functools

import numpy as np
import jax
import jax.numpy as jnp
from jax import lax
from jax.experimental import pallas as pl
from jax.experimental.pallas import tpu as pltpu

F32 = jnp.float32
BF16 = jnp.bfloat16

D_MODEL = 1024
DEPTH = 2
DEEPNORM_ALPHA = (2 * DEPTH) ** 0.25
ROPE_THETA = 10000.0
RMS_EPS = 1e-6
LN_EPS = 1e-5
L2_EPS = 1e-6

MLA_HEADS = 8
MLA_NOPE = 128
MLA_ROPE = 64
MLA_V = 128
Q_LORA = 768
KV_LORA = 256

DIL_HEADS = 8
DIL_HD = 128
DILATED_GROUPS = ((128, 1), (512, 4), (2048, 16))

GDN_HEADS = 8
GDN_DK = 128
GDN_DV = 128
CONV_K = 5
GDN_CHUNK = 128

RET_HEADS = 8
RET_DK = 64
RET_DV = 128
RET_DECAY_BASE = 5.0
RET_CHUNK = 256

LANES = 128
VMEM_LIMIT = 56 * 1024 * 1024


def _cparams(sem):
    return pltpu.CompilerParams(dimension_semantics=sem, vmem_limit_bytes=VMEM_LIMIT)


def _dot(a, b):
    return jnp.dot(a, b, preferred_element_type=F32)


def _dot_nt(a, b):
    return lax.dot_general(a, b, (((1,), (1,)), ((), ())), preferred_element_type=F32)


def _dot_tn(a, b):
    return lax.dot_general(a, b, (((0,), (0,)), ((), ())), preferred_element_type=F32)


def _const_spec(shape):
    nd = len(shape)
    return pl.BlockSpec(shape, lambda *_: (0,) * nd, pipeline_mode=pl.Buffered(1))


def _rope_pairs64(x, cos, sin_signed):
    lane = lax.broadcasted_iota(jnp.int32, x.shape, 1)
    swapped = jnp.where((lane % 64) < 32, pltpu.roll(x, 96, 1), pltpu.roll(x, 32, 1))
    return x * cos + swapped * sin_signed


def _rope_full128(x, cos, sin_signed):
    return x * cos + pltpu.roll(x, 64, 1) * sin_signed


def _splice_kernel(w_ref, o_ref, *, head_end, tail_start, gap):
    w = w_ref[0]
    parts = [w[:, :head_end], w[:, tail_start:]]
    if gap:
        parts.insert(1, jnp.zeros((w.shape[0], gap), w.dtype))
    o_ref[...] = jnp.concatenate(parts, axis=1).astype(o_ref.dtype)


def _splice_columns(w, head_end, tail_start, gap, tr=128):
    _, rows, cols = w.shape
    total = head_end + gap + cols - tail_start
    return pl.pallas_call(
        functools.partial(_splice_kernel, head_end=head_end, tail_start=tail_start, gap=gap),
        grid=(rows // tr,),
        in_specs=[pl.BlockSpec((1, tr, cols), lambda i: (0, i, 0))],
        out_specs=pl.BlockSpec((tr, total), lambda i: (i, 0)),
        out_shape=jax.ShapeDtypeStruct((rows, total), BF16),
        compiler_params=_cparams(("parallel",)),
        name="regroup_weight",
    )(w)


def _silu(x):
    return x * jax.nn.sigmoid(x)


def _rope_tables_kernel(pos_ref, inv_ref, sgn_ref, c64_ref, s64_ref, c128_ref, s128_ref):
    ang = pos_ref[0] * inv_ref[...]
    cos, sin = jnp.cos(ang), jnp.sin(ang)
    c32, s32 = cos[:, :32], sin[:, :32]
    c64, s64 = cos[:, 32:96], sin[:, 32:96]
    c64_ref[0] = jnp.concatenate([c32] * 4, axis=1)
    s64_ref[0] = jnp.concatenate([s32] * 4, axis=1) * sgn_ref[0:1, :]
    c128_ref[0] = jnp.concatenate([c64] * 2, axis=1)
    s128_ref[0] = jnp.concatenate([s64] * 2, axis=1) * sgn_ref[1:2, :]


def _rope_tables(positions):
    bsz, seq = positions.shape
    ts = 512
    inv32 = ROPE_THETA ** (-jnp.arange(0, 64, 2, dtype=F32) / 64)
    inv64 = ROPE_THETA ** (-jnp.arange(0, 128, 2, dtype=F32) / 128)
    inv = jnp.concatenate([inv32, inv64, jnp.zeros((32,), F32)])[None, :]
    sgn = jnp.asarray(np.stack([np.tile(np.repeat([-1.0, 1.0], 32), 2), np.repeat([-1.0, 1.0], 64)]), F32)
    pos = positions.astype(F32)[..., None]
    tab = jax.ShapeDtypeStruct((bsz, seq, LANES), F32)
    tspec = pl.BlockSpec((1, ts, LANES), lambda b, s: (b, s, 0))
    return pl.pallas_call(
        _rope_tables_kernel,
        grid=(bsz, seq // ts),
        in_specs=[pl.BlockSpec((1, ts, 1), lambda b, s: (b, s, 0)), _const_spec((1, LANES)), _const_spec((2, LANES))],
        out_specs=[tspec] * 4,
        out_shape=[tab] * 4,
        compiler_params=_cparams(("parallel", "parallel")),
        name="rope_tables",
    )(pos, inv, sgn)


def _rms(x, w):
    return x * lax.rsqrt(jnp.mean(x * x, axis=-1, keepdims=True) + RMS_EPS) * w


IN0_COLS = np.cumsum((0, Q_LORA, KV_LORA, 2 * MLA_ROPE, 3 * DIL_HEADS * DIL_HD, MLA_HEADS * MLA_V + DIL_HEADS * DIL_HD))


def _in0_kernel(x_ref, w_ref, wuq_ref, wukv_ref, qnorm_ref, kvnorm_ref,
                c64_ref, s64_ref, c128_ref, s128_ref,
                qn_ref, qr_ref, kv_ref, kr_ref, qb_ref, kb_ref, vb_ref, g_ref):
    x = x_ref[0].astype(BF16)
    c64, s64, c128, s128 = c64_ref[0], s64_ref[0], c128_ref[0], s128_ref[0]

    def proj(i):
        return _dot(x, w_ref[:, IN0_COLS[i]:IN0_COLS[i + 1]])

    cq = _rms(proj(0), qnorm_ref[...])
    qa = _dot(cq.astype(BF16), wuq_ref[...]) * ((MLA_NOPE + MLA_ROPE) ** -0.5 * LOG2E)
    nope = MLA_HEADS * MLA_NOPE
    qn_ref[0] = qa[:, :nope].astype(BF16)
    for j in range(MLA_HEADS * MLA_ROPE // LANES):
        blk = qa[:, nope + j * LANES:nope + (j + 1) * LANES]
        qr_ref[0, :, j * LANES:(j + 1) * LANES] = _rope_pairs64(blk, c64, s64).astype(BF16)
    ckv = _rms(proj(1), kvnorm_ref[...])
    kv_ref[0] = _dot(ckv.astype(BF16), wukv_ref[...]).astype(BF16)
    kr_ref[0] = _rope_pairs64(proj(2), c64, s64).astype(BF16)
    hb = proj(3)
    width = DIL_HEADS * DIL_HD
    for h in range(DIL_HEADS):
        sl = slice(h * DIL_HD, (h + 1) * DIL_HD)
        qb_ref[0, :, sl] = _rope_full128(hb[:, sl], c128, s128) * (DIL_HD ** -0.5 * LOG2E)
        kb_ref[0, :, sl] = _rope_full128(hb[:, width + h * DIL_HD:width + (h + 1) * DIL_HD], c128, s128)
    vb_ref[0] = hb[:, 2 * width:]
    g_ref[0] = _silu(proj(4)).astype(BF16)


def _in0(x, tabs, w_in, q_norm, w_uq, kv_norm, w_ukv, tm=512):
    bsz, seq, _ = x.shape
    c64, s64, c128, s128 = tabs
    kr0 = Q_LORA + KV_LORA
    w_all = _splice_columns(w_in, kr0 + MLA_ROPE, kr0, 0)
    wuq3 = w_uq.reshape(Q_LORA, MLA_HEADS, MLA_NOPE + MLA_ROPE)
    wuq = jnp.concatenate([wuq3[:, :, :MLA_NOPE].reshape(Q_LORA, -1), wuq3[:, :, MLA_NOPE:].reshape(Q_LORA, -1)],
                          axis=1).astype(BF16)
    wukv = w_ukv.astype(BF16)
    weights = (w_all, wuq, wukv, q_norm.reshape(1, -1), kv_norm.reshape(1, -1))

    def rows(width):
        return pl.BlockSpec((1, tm, width), lambda b, s: (b, s, 0))

    out_widths = (1024, 512, 2048, LANES, 1024, 1024, 1024, 2048)
    out_dtypes = (BF16, BF16, BF16, BF16, F32, F32, F32, BF16)
    return pl.pallas_call(
        _in0_kernel,
        grid=(bsz, seq // tm),
        in_specs=[rows(D_MODEL)] + [_const_spec(w.shape) for w in weights] + [rows(LANES)] * 4,
        out_specs=[rows(w) for w in out_widths],
        out_shape=[jax.ShapeDtypeStruct((bsz, seq, w), dt) for w, dt in zip(out_widths, out_dtypes)],
        compiler_params=_cparams(("parallel", "parallel")),
        name="in_proj0",
    )(x, *weights, c64, s64, c128, s128)


MLA_SUB = 256
MLA_ONES_ROWS = 16
MLA_AHEAD = 3
LOG2E = 1.4426950408889634


def _mla_kernel(qn_ref, qr_ref, kn_ref, kr_ref, v_ref, o_ref, k_s, vt_s):
    h = pl.program_id(1)

    @pl.when(pl.program_id(2) == 0)
    def _():
        k_s[:, :MLA_NOPE] = kn_ref[0]
        k_s[:, MLA_NOPE:] = kr_ref[0]
        vt_s[:MLA_V, :] = v_ref[0].astype(F32).T.astype(BF16)
        vt_s[MLA_V:, :] = jnp.ones((MLA_ONES_ROWS, vt_s.shape[1]), BF16)

    lane = lax.broadcasted_iota(jnp.int32, (MLA_SUB, LANES), 1)
    mine = (lane // MLA_ROPE) == (h % 2)
    nsub = qn_ref.shape[1] // MLA_SUB

    def scores(i):
        rows = slice(i * MLA_SUB, (i + 1) * MLA_SUB)
        qr = jnp.where(mine, qr_ref[0, rows, :], jnp.zeros((MLA_SUB, LANES), BF16))
        return _dot_nt(k_s[...], jnp.concatenate([qn_ref[0, rows, :], qr], axis=1)).astype(BF16)

    ahead = [scores(i) for i in range(min(MLA_AHEAD, nsub))]
    for i in range(nsub):
        s = ahead.pop(0)
        if i + MLA_AHEAD < nsub:
            ahead.append(scores(i + MLA_AHEAD))
        m = jnp.max(s, axis=0, keepdims=True)
        p = jnp.exp2(s - m)
        ot = _dot(vt_s[...], p)
        o = ot[:MLA_V] / ot[MLA_V:MLA_V + 1]
        o_ref[0, i * MLA_SUB:(i + 1) * MLA_SUB, :] = o.T.astype(o_ref.dtype)


def _mla(qn, qr, kv, kr, tq=2048):
    bsz, seq, _ = qn.shape
    return pl.pallas_call(
        _mla_kernel,
        grid=(bsz, MLA_HEADS, seq // tq),
        in_specs=[
            pl.BlockSpec((1, tq, LANES), lambda b, h, i: (b, i, h)),
            pl.BlockSpec((1, tq, LANES), lambda b, h, i: (b, i, h // 2)),
            pl.BlockSpec((1, seq, LANES), lambda b, h, i: (b, 0, 2 * h)),
            pl.BlockSpec((1, seq, LANES), lambda b, h, i: (b, 0, 0)),
            pl.BlockSpec((1, seq, LANES), lambda b, h, i: (b, 0, 2 * h + 1)),
        ],
        out_specs=pl.BlockSpec((1, tq, LANES), lambda b, h, i: (b, i, h)),
        out_shape=jax.ShapeDtypeStruct((bsz, seq, MLA_HEADS * MLA_V), BF16),
        scratch_shapes=[pltpu.VMEM((seq, MLA_NOPE + LANES), BF16), pltpu.VMEM((MLA_V + MLA_ONES_ROWS, seq), BF16)],
        compiler_params=_cparams(("parallel", "parallel", "arbitrary")),
        name="mla_attention",
    )(qn, qr, kv, kr, kv)


DIL_TQ = 128
DIL_RADIUS = 64
DIL_TILES_PER_STEP = 16


def _dil_bias_table():
    i = np.arange(DIL_TQ)[:, None]
    j = np.arange(2 * DIL_TQ)[None, :]
    tabs = [np.where(np.abs(j - off - i) <= DIL_RADIUS, 0.0, -np.inf) for off in (0, DIL_RADIUS, 2 * DIL_RADIUS)]
    return jnp.asarray(np.stack(tabs), F32)


def _dil_kernel(q_ref, k_ref, v_ref, bias_ref, o_ref, qd_s, kd_s, vd_s, m_s, l_s, acc_s):
    seq = q_ref.shape[1]
    vd_s[:, :, DIL_HD:] = jnp.ones((len(DILATED_GROUPS), seq, DIL_HD), BF16)

    def rows_of(start, size, dil):
        return pl.ds(start, size) if dil == 1 else pl.ds(start, size, stride=dil)

    for g, (_, dil) in enumerate(DILATED_GROUPS):
        length = seq // dil
        for src, dst in ((q_ref, qd_s), (k_ref, kd_s), (v_ref, vd_s)):
            for r in range(dil):
                dst[g, r * length:(r + 1) * length, :DIL_HD] = src[0, rows_of(r, length, dil), :].astype(BF16)

    def tiles_step(g, dil, which, order):
        length = seq // dil
        tiles = length // DIL_TQ
        kw = min(length, DIL_TQ + 2 * DIL_RADIUS)
        n = len(which)
        ws = [jnp.clip(t * DIL_TQ - DIL_RADIUS, 0, length - kw) for _, t in which]
        qrows = [pl.ds(pl.multiple_of(r * length + t * DIL_TQ, DIL_TQ), DIL_TQ) for r, t in which]
        krows = [pl.ds(pl.multiple_of(r * length + ws[i], DIL_RADIUS), kw) for i, (r, _) in enumerate(which)]
        if dil == 1:
            out_rows = qrows
        else:
            out_rows = [rows_of(r + t * (DIL_TQ * dil), DIL_TQ, dil) for r, t in which]
        if tiles == 1:
            bias = [bias_ref[0, :, :kw] for _ in which]
        else:
            bias = [bias_ref[jnp.where(t == 0, 0, jnp.where(t == tiles - 1, 2, 1))] for _, t in which]
        s = [(_dot_nt(qd_s[g, qrows[i], :], kd_s[g, krows[i], :]) + bias[i]).astype(BF16) for i in range(n)]
        mk = [jnp.max(t, axis=-1, keepdims=True) for t in s]
        m = [jnp.broadcast_to(t.astype(F32), (DIL_TQ, DIL_HD)) for t in mk]
        p = [jnp.exp2(s[i] - mk[i]) for i in range(n)]
        ol = [_dot(p[i], vd_s[g, krows[i], :]) for i in range(n)]
        o = [t[:, :DIL_HD] for t in ol]
        l = [t[:, DIL_HD:] for t in ol]
        for i in range(n):
            if order == 0:
                m_s[out_rows[i], :] = m[i]
                l_s[out_rows[i], :] = l[i]
                acc_s[out_rows[i], :] = o[i]
                continue
            m_old = m_s[out_rows[i], :]
            m_new = jnp.maximum(m_old, m[i])
            a_old = jnp.exp2(m_old - m_new)
            a_new = jnp.exp2(m[i] - m_new)
            l_new = a_old * l_s[out_rows[i], :] + a_new * l[i]
            acc_new = a_old * acc_s[out_rows[i], :] + a_new * o[i]
            if order == len(DILATED_GROUPS) - 1:
                o_ref[0, out_rows[i], :] = (acc_new / l_new).astype(o_ref.dtype)
            else:
                m_s[out_rows[i], :] = m_new
                l_s[out_rows[i], :] = l_new
                acc_s[out_rows[i], :] = acc_new

    for order, g in enumerate(reversed(range(len(DILATED_GROUPS)))):
        window, dil = DILATED_GROUPS[g]
        assert window // (2 * dil) == DIL_RADIUS
        assert order < len(DILATED_GROUPS) - 1 or dil == 1
        tiles = seq // dil // DIL_TQ

        def body(it, carry, g=g, dil=dil, tiles=tiles, order=order):
            idx = [it * DIL_TILES_PER_STEP + j for j in range(DIL_TILES_PER_STEP)]
            tiles_step(g, dil, [(i // tiles, i % tiles) for i in idx], order)
            return carry

        lax.fori_loop(0, dil * tiles // DIL_TILES_PER_STEP, body, 0)


def _dilated(qb, kb, vb):
    bsz, seq, _ = qb.shape
    spec = pl.BlockSpec((1, seq, DIL_HD), lambda b, h: (b, 0, h))
    bias = _dil_bias_table()
    grouped = pltpu.VMEM((len(DILATED_GROUPS), seq, DIL_HD), BF16)
    stat = pltpu.VMEM((seq, DIL_HD), F32)
    return pl.pallas_call(
        _dil_kernel,
        grid=(bsz, DIL_HEADS),
        in_specs=[spec] * 3 + [_const_spec(bias.shape)],
        out_specs=spec,
        out_shape=jax.ShapeDtypeStruct((bsz, seq, DIL_HEADS * DIL_HD), BF16),
        scratch_shapes=[grouped, grouped, pltpu.VMEM((len(DILATED_GROUPS), seq, 2 * DIL_HD), BF16), stat, stat, stat],
        compiler_params=_cparams(("parallel", "parallel")),
        name="dilated_attention",
    )(qb, kb, vb, bias)


OUT_SUB = 256


def _out_kernel(o1_ref, o2_ref, g_ref, x_ref, w_ref, lnw_ref, lnb_ref, out_ref):
    half = o1_ref.shape[-1]
    for i in range(o1_ref.shape[1] // OUT_SUB):
        rows = slice(i * OUT_SUB, (i + 1) * OUT_SUB)
        y1 = o1_ref[0, rows, :] * g_ref[0, rows, :half]
        y2 = o2_ref[0, rows, :] * g_ref[0, rows, half:]
        r = DEEPNORM_ALPHA * x_ref[0, rows, :] + _dot(y1, w_ref[:half, :]) + _dot(y2, w_ref[half:, :])
        mu = jnp.mean(r, axis=-1, keepdims=True)
        rc = r - mu
        var = jnp.mean(rc * rc, axis=-1, keepdims=True)
        out_ref[0, rows, :] = rc * lax.rsqrt(var + LN_EPS) * lnw_ref[...] + lnb_ref[...]


def _out_proj(o1, o2, gate, x, w_out, ln_w, ln_b, tm=512):
    bsz, seq, _ = x.shape
    half = o1.shape[-1]

    def rows(width):
        return pl.BlockSpec((1, tm, width), lambda b, s: (b, s, 0))

    w = w_out.astype(BF16)
    return pl.pallas_call(
        _out_kernel,
        grid=(bsz, seq // tm),
        in_specs=[rows(half), rows(half), rows(2 * half), rows(D_MODEL), _const_spec(w.shape),
                  _const_spec((1, D_MODEL)), _const_spec((1, D_MODEL))],
        out_specs=rows(D_MODEL),
        out_shape=jax.ShapeDtypeStruct((bsz, seq, D_MODEL), F32),
        compiler_params=_cparams(("parallel", "parallel")),
        name="out_proj",
    )(o1, o2, gate, x, w, ln_w.reshape(1, -1), ln_b.reshape(1, -1))


def _layer0(x, tabs, w_in, q_norm, w_uq, kv_norm, w_ukv, w_out, ln_w, ln_b):
    qn, qr, kv, kr, qb, kb, vb, gate = _in0(x, tabs, w_in, q_norm, w_uq, kv_norm, w_ukv)
    o_a = _mla(qn, qr, kv, kr)
    o_b = _dilated(qb, kb, vb)
    return _out_proj(o_a, o_b, gate, x, w_out, ln_w, ln_b)


GATE_BETA = 0
GATE_GC = 2 * GDN_HEADS


def _softplus(x):
    return jnp.maximum(x, 0.0) + jnp.log1p(jnp.exp(-jnp.abs(x)))


IN1_COLS = np.cumsum((0, 2 * GDN_HEADS * GDN_DK + GDN_HEADS * GDN_DV, LANES, 2 * RET_HEADS * RET_DK,
                      RET_HEADS * RET_DV, GDN_HEADS * GDN_DV + RET_HEADS * RET_DV))


def _in1_kernel(x_ref, w_ref, gp_ref, c64_ref, s64_ref,
                qkvc_ref, gate_ref, qr_ref, kr_ref, vr_ref, g_ref):
    x = x_ref[0].astype(BF16)
    tm = x.shape[0]
    c64, s64 = c64_ref[0], s64_ref[0]

    def proj(i):
        return _dot(x, w_ref[:, IN1_COLS[i]:IN1_COLS[i + 1]])

    qkvc_ref[0] = proj(0).astype(qkvc_ref.dtype)
    hg = proj(1)
    beta = jax.nn.sigmoid(hg)
    g = -jnp.exp(gp_ref[0:1, :]) * _softplus(hg + gp_ref[1:2, :])
    pos = lax.broadcasted_iota(jnp.int32, hg.shape, 0) % GDN_CHUNK
    pre, suf = g, g
    step = 1
    while step < GDN_CHUNK:
        pre = pre + jnp.where(pos >= step, pltpu.roll(pre, step, 0), 0.0)
        suf = suf + jnp.where(pos < GDN_CHUNK - step, pltpu.roll(suf, tm - step, 0), 0.0)
        step *= 2
    lane = lax.broadcasted_iota(jnp.int32, hg.shape, 1)
    gc = jnp.where(lane < GATE_GC + GDN_HEADS, pre, suf)
    gate_ref[0] = jnp.where(lane < GATE_GC, beta, gc)
    hqk = proj(2)
    width = RET_HEADS * RET_DK
    for j in range(width // LANES):
        qr_ref[0, :, j * LANES:(j + 1) * LANES] = _rope_pairs64(hqk[:, j * LANES:(j + 1) * LANES], c64, s64).astype(BF16)
        kblk = hqk[:, width + j * LANES:width + (j + 1) * LANES]
        kr_ref[0, :, j * LANES:(j + 1) * LANES] = (_rope_pairs64(kblk, c64, s64) * (RET_DK ** -0.5)).astype(BF16)
    vr_ref[0] = proj(3).astype(BF16)
    g_ref[0] = _silu(proj(4)).astype(BF16)


def _in1(x, tabs, w_in, a_log, dt_bias, tm=512):
    bsz, seq, _ = x.shape
    c64, s64, _, _ = tabs
    conv_ch = 2 * GDN_HEADS * GDN_DK + GDN_HEADS * GDN_DV
    gate_end = conv_ch + 4 * GDN_HEADS
    w_all = _splice_columns(w_in, gate_end, gate_end, LANES - 4 * GDN_HEADS)
    pad = (GATE_GC, LANES - GATE_GC - 2 * GDN_HEADS)
    gp = jnp.stack([jnp.pad(a_log.reshape(-1), pad), jnp.pad(dt_bias.reshape(-1), pad)]).astype(F32)
    weights = (w_all, gp)

    def rows(width):
        return pl.BlockSpec((1, tm, width), lambda b, s: (b, s, 0))

    out_widths = (conv_ch, LANES, 512, 512, 1024, 2048)
    out_dtypes = (BF16, F32, BF16, BF16, BF16, BF16)
    return pl.pallas_call(
        _in1_kernel,
        grid=(bsz, seq // tm),
        in_specs=[rows(D_MODEL)] + [_const_spec(w.shape) for w in weights] + [rows(LANES)] * 2,
        out_specs=[rows(w) for w in out_widths],
        out_shape=[jax.ShapeDtypeStruct((bsz, seq, w), dt) for w, dt in zip(out_widths, out_dtypes)],
        compiler_params=_cparams(("parallel", "parallel")),
        name="in_proj1",
    )(x, *weights, c64, s64)


CONV_PAD = 8
GDN_P1_CHUNKS = 8
GDN_LEVELS = 7


def _gdn_level_masks():
    i = np.arange(GDN_CHUNK)[:, None]
    j = np.arange(GDN_CHUNK)[None, :]
    masks = [((i >> (k + 1)) == (j >> (k + 1))) & ((i >> k) != (j >> k)) for k in range(GDN_LEVELS)]
    return jnp.asarray(np.stack(masks), F32)


GDN_HPS = 1
CONV_ROWS = 256
G_BETA, G_GC = 0, 2


def _gdn2_kernel(qraw_ref, kraw_ref, vraw_ref, wq_ref, wk_ref, wv_ref, grow_ref, norm_ref, lvl_ref, o_ref,
                 xpad_s, q_s, k_s, v_s, w_s, u_s, att_s, qg_s, mt_s, bt_s, gl_s, od_s):
    seq = q_s.shape[1]
    nchunk = seq // GDN_CHUNK

    zeros = jnp.zeros((3, CONV_PAD, LANES), F32)
    xpad_s[:, 0:CONV_PAD, :] = zeros
    xpad_s[:, CONV_PAD + seq:, :] = zeros

    def l2norm(t):
        return t * lax.rsqrt(jnp.sum(t * t, axis=-1, keepdims=True) + L2_EPS)

    finishers = (lambda t: l2norm(t) * (GDN_DK ** -0.5), l2norm, lambda t: t)

    def conv_silu_rows(hh, row0, nrows):
        lanes = slice(hh * LANES, (hh + 1) * LANES)
        for a, (w_ref, dst) in enumerate(((wq_ref, q_s), (wk_ref, k_s), (wv_ref, v_s))):
            for blk in range(row0 // CONV_ROWS, (row0 + nrows) // CONV_ROWS):
                acc = None
                for j in range(CONV_K):
                    off = blk * CONV_ROWS + CONV_PAD - (CONV_K - 1) // 2 + j
                    term = xpad_s[a, off:off + CONV_ROWS, :] * w_ref[j:j + 1, lanes]
                    acc = term if acc is None else acc + term
                dst[hh, blk * CONV_ROWS:(blk + 1) * CONV_ROWS, :] = finishers[a](_silu(acc))

    ii = lax.broadcasted_iota(jnp.int32, (GDN_CHUNK, GDN_CHUNK), 0)
    jj = lax.broadcasted_iota(jnp.int32, (GDN_CHUNK, GDN_CHUNK), 1)
    causal = (ii >= jj, ii <= jj)
    strict = (ii > jj, ii < jj)

    def chunk_rows(c):
        return slice(c * GDN_CHUNK, (c + 1) * GDN_CHUNK)

    def token_major(row):
        return jnp.broadcast_to(row, (LANES, GDN_CHUNK)).T

    def phase1(it, hh):
        chains = [(it * GDN_P1_CHUNKS + j, d) for j in range(GDN_P1_CHUNKS) for d in range(2)]
        n_ch = len(chains)
        rows = [chunk_rows(c) for c, _ in chains]
        kf = [k_s[hh, r, :] for r in rows]
        kc = [t.astype(BF16) for t in kf]
        grow = [grow_ref[0, hh, G_GC + d, pl.ds(c, 1), :] for c, d in chains]
        gcb = [token_major(t) for t in grow]
        bcb = [token_major(grow_ref[0, hh, G_BETA + d, pl.ds(c, 1), :]) for c, d in chains]
        kb = [kf[i] * bcb[i] for i in range(n_ch)]
        kk = [_dot_nt(kb[i].astype(BF16), kc[i]) for i in range(n_ch)]
        decay = [jnp.exp(jnp.where(causal[d], gcb[i] - grow[i], -jnp.inf)) for i, (_, d) in enumerate(chains)]
        a = [jnp.where(strict[d], kk[i] * decay[i], 0.0) for i, (_, d) in enumerate(chains)]
        egc = [jnp.exp(t) for t in gcb]
        wpre = [kb[i] * egc[i] for i in range(n_ch)]
        upre = [v_s[hh, rows[i], :] * bcb[i] for i in range(n_ch)]
        qf = [q_s[hh, rows[2 * j], :] for j in range(GDN_P1_CHUNKS)]
        qkc = [_dot_nt(qf[j].astype(BF16), kc[2 * j]) for j in range(GDN_P1_CHUNKS)]
        gl = [gcb[i][GDN_CHUNK - 1:GDN_CHUNK, :] if d == 0 else gcb[i][0:1, :] for i, (_, d) in enumerate(chains)]
        kd = [(kf[i] * jnp.exp(gl[i] - gcb[i])).astype(BF16) for i in range(n_ch)]
        for i, (c, d) in enumerate(chains):
            att_s[hh, d, rows[i], :] = (qkc[i // 2] * decay[i]).astype(BF16)
            qg_s[hh, d, rows[i], :] = (qf[i // 2] * egc[i]).astype(BF16)
            gl_s[hh, d, c] = jnp.broadcast_to(jnp.exp(gl[i]), (8, LANES))
        ab = [t.astype(BF16) for t in a]
        n = [-(t * lvl_ref[0]) for t in a]
        for lvl in range(1, GDN_LEVELS):
            ak = [t * lvl_ref[lvl].astype(BF16) for t in ab]
            y = [ak[i].astype(F32) + _dot(ak[i], n[i].astype(BF16)) for i in range(n_ch)]
            n = [n[i] - y[i] - _dot(n[i].astype(BF16), y[i].astype(BF16)) for i in range(n_ch)]
        nb = [t.astype(BF16) for t in n]
        w = [(wpre[i] + _dot(nb[i], wpre[i].astype(BF16))).astype(BF16) for i in range(n_ch)]
        u = [upre[i] + _dot(nb[i], upre[i].astype(BF16)) for i in range(n_ch)]
        mt = [_dot_tn(w[i], kd[i]) for i in range(n_ch)]
        bt = [_dot_tn(u[i].astype(BF16), kd[i]) for i in range(n_ch)]
        for i, (c, d) in enumerate(chains):
            w_s[hh, d, rows[i], :] = w[i]
            u_s[hh, d, rows[i], :] = u[i]
            mt_s[hh, d, c] = (-mt[i]).astype(BF16)
            bt_s[hh, d, c] = bt[i]

    for hh in range(GDN_HPS):
        lanes = slice(hh * LANES, (hh + 1) * LANES)
        for a, raw_ref in enumerate((qraw_ref, kraw_ref, vraw_ref)):
            xpad_s[a, CONV_PAD:CONV_PAD + seq, :] = raw_ref[0, :, lanes].astype(F32)
        for it in range(nchunk // GDN_P1_CHUNKS):
            conv_silu_rows(hh, it * GDN_P1_CHUNKS * GDN_CHUNK, GDN_P1_CHUNKS * GDN_CHUNK)
            phase1(it, hh)

    chains = [(hh, d) for hh in range(GDN_HPS) for d in range(2)]
    state = [jnp.zeros((GDN_DV, GDN_DK), F32) for _ in chains]
    def finish(pending):
        for (hh, d), r, m in pending:
            vnew = (u_s[hh, d, r, :] - m[:GDN_CHUNK]).astype(BF16)
            od_s[hh, d, r, :] = m[GDN_CHUNK:] + _dot(att_s[hh, d, r, :], vnew)

    pending = []
    for i in range(nchunk):
        cs = [i if d == 0 else nchunk - 1 - i for _, d in chains]
        rows = [slice(c * GDN_CHUNK, (c + 1) * GDN_CHUNK) for c in cs]
        stb = [t.astype(BF16) for t in state]
        state = [state[j] * gl_s[hh, d, cs[j], 0:1, :] + _dot(stb[j], mt_s[hh, d, cs[j]]) + bt_s[hh, d, cs[j]]
                 for j, (hh, d) in enumerate(chains)]
        finish(pending)
        pending = [(chains[j], rows[j],
                    _dot_nt(jnp.concatenate([w_s[hh, d, rows[j], :], qg_s[hh, d, rows[j], :]], axis=0), stb[j]))
                   for j, (hh, d) in enumerate(chains)]
    finish(pending)

    for hh in range(GDN_HPS):
        for blk in range(seq // CONV_ROWS):
            r = slice(blk * CONV_ROWS, (blk + 1) * CONV_ROWS)
            o = od_s[hh, 0, r, :] + od_s[hh, 1, r, :]
            o_ref[0, r, hh * LANES:(hh + 1) * LANES] = (
                o * lax.rsqrt(jnp.mean(o * o, axis=-1, keepdims=True) + RMS_EPS) * norm_ref[...]).astype(o_ref.dtype)


def _gdn2(qkvc, conv_w, grow, gdn_norm):
    bsz, seq, _ = qkvc.shape
    groups = GDN_HEADS // GDN_HPS
    width = GDN_HPS * LANES
    nchunk = seq // GDN_CHUNK

    def col(off):
        return pl.BlockSpec((1, seq, width), lambda b, g: (b, 0, off + g))

    def wcol(off):
        return pl.BlockSpec((CONV_K, width), lambda b, g: (0, off + g))

    def per_head(shape, dtype):
        return pltpu.VMEM((GDN_HPS,) + shape, dtype)

    return pl.pallas_call(
        _gdn2_kernel,
        grid=(bsz, groups),
        in_specs=[col(0), col(groups), col(2 * groups), wcol(0), wcol(groups), wcol(2 * groups),
                  pl.BlockSpec((1, GDN_HPS, 4, nchunk, GDN_CHUNK), lambda b, g: (b, g, 0, 0, 0)),
                  _const_spec((1, GDN_DV)), _const_spec((GDN_LEVELS, GDN_CHUNK, GDN_CHUNK))],
        out_specs=pl.BlockSpec((1, seq, width), lambda b, g: (b, 0, g)),
        out_shape=jax.ShapeDtypeStruct((bsz, seq, GDN_HEADS * GDN_DV), BF16),
        scratch_shapes=[pltpu.VMEM((3, seq + 2 * CONV_PAD, LANES), F32),
                        per_head((seq, LANES), F32), per_head((seq, LANES), F32), per_head((seq, LANES), F32),
                        per_head((2, seq, LANES), BF16), per_head((2, seq, LANES), F32),
                        per_head((2, seq, LANES), BF16), per_head((2, seq, LANES), BF16),
                        per_head((2, nchunk, GDN_DV, GDN_DK), BF16), per_head((2, nchunk, GDN_DV, GDN_DK), F32),
                        per_head((2, nchunk, 8, LANES), F32), per_head((2, seq, LANES), F32)],
        compiler_params=_cparams(("parallel", "parallel")),
        name="gated_deltanet",
    )(qkvc, qkvc, qkvc, conv_w, conv_w, conv_w, grow, gdn_norm.reshape(1, -1), _gdn_level_masks())


def _ret_tables():
    heads = np.arange(RET_HEADS, dtype=np.float64)
    log_gamma = np.log1p(-(2.0 ** (-RET_DECAY_BASE - heads)))[:, None]
    idx = np.arange(RET_CHUNK, dtype=np.float64)
    dmat = np.exp(log_gamma[:, :, None] * np.abs(idx[:, None] - idx[None, :])[None])
    xi = np.exp(log_gamma * (idx + 1.0))
    zeta = np.exp(log_gamma * (RET_CHUNK - 1.0 - idx))
    gch = np.exp(log_gamma * RET_CHUNK)

    def cb(t):
        return jnp.asarray(np.broadcast_to(t[:, :, None], t.shape + (LANES,)), F32)

    return jnp.asarray(dmat, F32), cb(xi), cb(zeta), cb(gch)


def _ret_kernel(q_ref, k_ref, v_ref, d_ref, xi_ref, zeta_ref, gch_ref, nw_ref, nb_ref, o_ref, kvf_s, kvb_s):
    seq = q_ref.shape[1]
    nchunk = seq // RET_CHUNK
    h = pl.program_id(1)
    xi, zeta, gch = xi_ref[0], zeta_ref[0], gch_ref[0]
    lane = lax.broadcasted_iota(jnp.int32, (RET_CHUNK, LANES), 1)
    mine = (lane // RET_DK) == (h % 2)

    def rows(c):
        return slice(c * RET_CHUNK, (c + 1) * RET_CHUNK)

    for c in range(nchunk):
        kc = k_ref[0, rows(c), :].astype(F32)
        vc = v_ref[0, rows(c), :]
        kvf_s[c] = _dot_tn((kc * zeta).astype(BF16), vc)
        kvb_s[c] = _dot_tn((kc * xi).astype(BF16), vc)
    state = jnp.zeros((LANES, RET_DV), F32)
    for c in range(nchunk):
        nxt = state * gch + kvf_s[c]
        kvf_s[c] = state
        state = nxt
    state = jnp.zeros((LANES, RET_DV), F32)
    for c in reversed(range(nchunk)):
        nxt = state * gch + kvb_s[c]
        kvb_s[c] = state
        state = nxt
    for c in range(nchunk):
        q = jnp.where(mine, q_ref[0, rows(c), :], jnp.zeros((RET_CHUNK, LANES), BF16))
        qf = q.astype(F32)
        s = _dot_nt(q, k_ref[0, rows(c), :]) * d_ref[0]
        o = _dot(s.astype(BF16), v_ref[0, rows(c), :])
        o = o + _dot((qf * xi).astype(BF16), kvf_s[c].astype(BF16))
        o = o + _dot((qf * zeta).astype(BF16), kvb_s[c].astype(BF16))
        mu = jnp.mean(o, axis=-1, keepdims=True)
        oc = o - mu
        var = jnp.mean(oc * oc, axis=-1, keepdims=True)
        o_ref[0, rows(c), :] = (oc * lax.rsqrt(var + LN_EPS) * nw_ref[...] + nb_ref[...]).astype(o_ref.dtype)


def _retention(qr, kr, vr, norm_w, norm_b):
    bsz, seq, _ = vr.shape
    dmat, xi, zeta, gch = _ret_tables()
    pair = pl.BlockSpec((1, seq, LANES), lambda b, h: (b, 0, h // 2))
    head = pl.BlockSpec((1, seq, RET_DV), lambda b, h: (b, 0, h))

    def per_head(shape):
        return pl.BlockSpec((1,) + shape, lambda b, h: (h, 0, 0))

    vec = pl.BlockSpec((1, RET_DV), lambda b, h: (0, h))
    state = pltpu.VMEM((seq // RET_CHUNK, LANES, RET_DV), F32)
    return pl.pallas_call(
        _ret_kernel,
        grid=(bsz, RET_HEADS),
        in_specs=[pair, pair, head, per_head((RET_CHUNK, RET_CHUNK)), per_head((RET_CHUNK, LANES)),
                  per_head((RET_CHUNK, LANES)), per_head((1, LANES)), vec, vec],
        out_specs=head,
        out_shape=jax.ShapeDtypeStruct((bsz, seq, RET_HEADS * RET_DV), BF16),
        scratch_shapes=[state, state],
        compiler_params=_cparams(("parallel", "parallel")),
        name="retention",
    )(qr, kr, vr, dmat, xi, zeta, gch, norm_w.reshape(1, -1), norm_b.reshape(1, -1))


def _layer1(x, tabs, w_in, conv_w, a_log, dt_bias, gdn_norm, ret_norm_w, ret_norm_b, w_out, ln_w, ln_b):
    bsz, seq, _ = x.shape
    qkvc, gates, qr, kr, vr, gate = _in1(x, tabs, w_in, a_log, dt_bias)
    gsl = gates[:, :, :4 * GDN_HEADS].reshape(bsz, seq, 4, GDN_HEADS)
    grow = gsl.transpose(0, 3, 2, 1).reshape(bsz, GDN_HEADS, 4, seq // GDN_CHUNK, GDN_CHUNK)
    o_c = _gdn2(qkvc, conv_w, grow, gdn_norm)
    o_d = _retention(qr, kr, vr, ret_norm_w, ret_norm_b)
    return _out_proj(o_c, o_d, gate, x, w_out, ln_w, ln_b)


def kernel(x, positions, even_w_in, even_q_norm, even_w_uq, even_kv_norm, even_w_ukv, even_w_out, even_ln_w, even_ln_b, odd_w_in, odd_conv_w, odd_a_log, odd_dt_bias, odd_gdn_norm, odd_ret_norm_w, odd_ret_norm_b, odd_w_out, odd_ln_w, odd_ln_b):
    tabs = _rope_tables(positions)
    x = _layer0(x, tabs, even_w_in, even_q_norm[0], even_w_uq[0], even_kv_norm[0], even_w_ukv[0],
                even_w_out[0], even_ln_w[0], even_ln_b[0])
    x = _layer1(x, tabs, odd_w_in, odd_conv_w[0], odd_a_log[0], odd_dt_bias[0], odd_gdn_norm[0],
                odd_ret_norm_w[0], odd_ret_norm_b[0], odd_w_out[0], odd_ln_w[0], odd_ln_b[0])
    return x
```
